```python
import jax, jax.numpy as jnp
from jax import lax
import numpy as np

D_MODEL = 2048
BATCH = 2
SEQ = 16384
DEPTH = 1

FOX_HEADS = 8
FOX_HEAD_DIM = 128
FOX_WIDTH = FOX_HEADS * FOX_HEAD_DIM
ATTN_BLOCK = 128
MLSTM_HEADS = 4
MLSTM_QK_DIM = 128
MLSTM_V_DIM = 256
MLSTM_QK_WIDTH = MLSTM_HEADS * MLSTM_QK_DIM
MLSTM_V_WIDTH = MLSTM_HEADS * MLSTM_V_DIM
MLSTM_CONV = 4
MLSTM_CHUNK = 128
MIX_WIDTH = FOX_WIDTH + MLSTM_V_WIDTH
IN_SPLITS = (FOX_WIDTH, FOX_WIDTH, FOX_WIDTH, FOX_HEADS, MLSTM_QK_WIDTH, MLSTM_QK_WIDTH, MLSTM_V_WIDTH, MLSTM_HEADS, MLSTM_HEADS, MLSTM_V_WIDTH)
IN_WIDTH = 3 * FOX_WIDTH + FOX_HEADS + 2 * MLSTM_QK_WIDTH + 2 * MLSTM_V_WIDTH + 2 * MLSTM_HEADS
PEER_HEADS = 8
PEER_KEY_DIM = 256
PEER_N_KEYS = 128
PEER_TOPK = 16
PEER_N_EXPERTS = PEER_N_KEYS * PEER_N_KEYS
PEER_TOKEN_BLOCK = 128
NORM_EPS = 1e-6

kernel_name = "fox_mlstm_peer_hybrid_block"


def rmsnorm(x, w):
    xf = x.astype(jnp.float32)
    y = xf * lax.rsqrt(jnp.mean(xf * xf, axis=-1, keepdims=True) + NORM_EPS)
    return (y * w).astype(x.dtype)


def head_rmsnorm(h, w):
    B, S, H, dh = h.shape
    hf = h.astype(jnp.float32)
    hf = hf * lax.rsqrt(jnp.mean(hf * hf, axis=-1, keepdims=True) + NORM_EPS)
    return (hf.reshape(B, S, H * dh) * w).astype(h.dtype)


def causal_depthwise_conv(x, w, b):
    K = w.shape[0]
    S = x.shape[1]
    xp = jnp.pad(x, ((0, 0), (K - 1, 0), (0, 0)))
    y = b
    for j in range(K):
        y = y + w[j] * xp[:, j:j + S]
    return y


def forgetting_attention(q, k, v, f_logit):
    B, S, H, Dh = q.shape
    nb = S // ATTN_BLOCK
    F = jnp.cumsum(jax.nn.log_sigmoid(f_logit.astype(jnp.float32)), axis=1).transpose(0, 2, 1)
    qh = q.transpose(0, 2, 1, 3) * (Dh ** -0.5)
    kh = k.transpose(0, 2, 1, 3)
    vh = v.transpose(0, 2, 1, 3)
    q_blocks = qh.reshape(B, H, nb, ATTN_BLOCK, Dh).transpose(2, 0, 1, 3, 4)
    F_blocks = F.reshape(B, H, nb, ATTN_BLOCK).transpose(2, 0, 1, 3)
    q_pos = jnp.arange(S).reshape(nb, ATTN_BLOCK)
    k_pos = jnp.arange(S)

    def block(args):
        qb, Fb, pos = args
        s = jnp.einsum('bhqd,bhkd->bhqk', qb, kh, preferred_element_type=jnp.float32)
        s = s + Fb[..., :, None] - F[..., None, :]
        s = jnp.where(pos[:, None] >= k_pos[None, :], s, -jnp.inf)
        p = jax.nn.softmax(s, axis=-1)
        return jnp.einsum('bhqk,bhkd->bhqd', p.astype(vh.dtype), vh)

    out = lax.map(block, (q_blocks, F_blocks, q_pos))
    return out.transpose(1, 0, 3, 2, 4).reshape(B, S, H, Dh)


def mlstm_chunkwise(q, k, v, i_pre, f_pre):
    B, S, H, dk = q.shape
    dv = v.shape[-1]
    L = MLSTM_CHUNK
    nc = S // L
    f32 = jnp.float32

    def to_chunks(a):
        a = jnp.moveaxis(a, 2, 1)
        return a.reshape(B, H, nc, L, *a.shape[3:])

    qc = to_chunks(q).astype(f32)
    kc = to_chunks(k).astype(f32) * (dk ** -0.5)
    vc = to_chunks(v).astype(f32)
    ic = to_chunks(i_pre).astype(f32)
    b = jnp.cumsum(jax.nn.log_sigmoid(to_chunks(f_pre).astype(f32)), axis=-1)
    b_last = b[..., -1]

    w_end = b_last[..., None] - b + ic
    m_loc = jnp.max(w_end, axis=-1)
    a_end = jnp.exp(w_end - m_loc[..., None])
    C_loc = jnp.einsum('bhcl,bhcld,bhcle->bhcde', a_end, kc, vc)
    n_loc = jnp.einsum('bhcl,bhcld->bhcd', a_end, kc)

    def step(carry, inp):
        C, n, m = carry
        Cl, nl, ml, bl = inp
        m_new = jnp.maximum(bl + m, ml)
        a_prev = jnp.exp(bl + m - m_new)
        a_loc = jnp.exp(ml - m_new)
        C_new = a_prev[..., None, None] * C + a_loc[..., None, None] * Cl
        n_new = a_prev[..., None] * n + a_loc[..., None] * nl
        return (C_new, n_new, m_new), (C, n, m)

    init = (jnp.zeros((B, H, dk, dv), f32), jnp.zeros((B, H, dk), f32), jnp.zeros((B, H), f32))
    xs = (jnp.moveaxis(C_loc, 2, 0), jnp.moveaxis(n_loc, 2, 0), jnp.moveaxis(m_loc, 2, 0), jnp.moveaxis(b_last, 2, 0))
    _, (C_prev, n_prev, m_prev) = lax.scan(step, init, xs)
    C_prev = jnp.moveaxis(C_prev, 0, 2)
    n_prev = jnp.moveaxis(n_prev, 0, 2)
    m_prev = jnp.moveaxis(m_prev, 0, 2)

    g = b + m_prev[..., None]
    D = b[..., :, None] - b[..., None, :] + ic[..., None, :]
    causal = jnp.tril(jnp.ones((L, L), dtype=bool))
    D = jnp.where(causal, D, -jnp.inf)
    m_t = jnp.maximum(g, jnp.max(D, axis=-1))
    Sm = jnp.einsum('bhctd,bhcsd->bhcts', qc, kc) * jnp.exp(D - m_t[..., None])
    inter = jnp.exp(g - m_t)
    num = jnp.einsum('bhcts,bhcse->bhcte', Sm, vc) + inter[..., None] * jnp.einsum('bhctd,bhcde->bhcte', qc, C_prev)
    den = jnp.sum(Sm, axis=-1) + inter * jnp.einsum('bhctd,bhcd->bhct', qc, n_prev)
    h = num / jnp.maximum(jnp.abs(den), jnp.exp(-m_t))[..., None]
    return h.reshape(B, H, S, dv).transpose(0, 2, 1, 3).astype(v.dtype)


def peer_ffn(x, w_q, keys, u, v):
    B, S, D = x.shape
    T = B * S
    xt = x.reshape(T, D)
    q = (xt @ w_q).reshape(T, PEER_HEADS, 2, PEER_KEY_DIM // 2)
    scores = jnp.einsum('thpd,pnd->thpn', q, keys, preferred_element_type=jnp.float32)
    s_top, i_top = lax.top_k(scores, PEER_TOPK)
    cand = (s_top[:, :, 0, :, None] + s_top[:, :, 1, None, :]).reshape(T, PEER_HEADS, PEER_TOPK * PEER_TOPK)
    cand_idx = (i_top[:, :, 0, :, None] * PEER_N_KEYS + i_top[:, :, 1, None, :]).reshape(T, PEER_HEADS, PEER_TOPK * PEER_TOPK)
    s_fin, pos = lax.top_k(cand, PEER_TOPK)
    idx = jnp.take_along_axis(cand_idx, pos, axis=-1)
    gate = jax.nn.softmax(s_fin, axis=-1)
    nblk = T // PEER_TOKEN_BLOCK

    def blk(args):
        xb, ib, gb = args
        ub = jnp.take(u, ib, axis=0)
        act = jax.nn.gelu(jnp.einsum('thkd,td->thk', ub, xb, preferred_element_type=jnp.float32), approximate=False) * gb
        vb = jnp.take(v, ib, axis=0)
        return jnp.einsum('thk,thkd->td', act.astype(vb.dtype), vb)

    y = lax.map(blk, (xt.reshape(nblk, PEER_TOKEN_BLOCK, D),
                      idx.reshape(nblk, PEER_TOKEN_BLOCK, PEER_HEADS, PEER_TOPK),
                      gate.reshape(nblk, PEER_TOKEN_BLOCK, PEER_HEADS, PEER_TOPK)))
    return y.reshape(B, S, D)


def setup_inputs(seed: int = 0) -> dict:
    key = jax.random.key(seed)
    ks = jax.random.split(key, 18)
    f32 = jnp.float32

    def nrm(k, shape, scale):
        return jax.random.normal(k, shape, f32) * scale

    return {
        "x": nrm(ks[0], (BATCH, SEQ, D_MODEL), 1.0),
        "norm1_w": 1.0 + nrm(ks[1], (DEPTH, D_MODEL), 0.02),
        "w_in": nrm(ks[2], (DEPTH, D_MODEL, IN_WIDTH), D_MODEL ** -0.5),
        "fox_f_bias": jnp.linspace(2.0, 6.0, FOX_HEADS, dtype=f32)[None] + nrm(ks[3], (DEPTH, FOX_HEADS), 0.1),
        "mlstm_conv_w": nrm(ks[4], (DEPTH, MLSTM_CONV, 2 * MLSTM_QK_WIDTH), MLSTM_CONV ** -0.5),
        "mlstm_conv_b": nrm(ks[5], (DEPTH, 2 * MLSTM_QK_WIDTH), 0.01),
        "mlstm_i_bias": nrm(ks[6], (DEPTH, MLSTM_HEADS), 0.1),
        "mlstm_f_bias": jnp.linspace(3.0, 6.0, MLSTM_HEADS, dtype=f32)[None] + nrm(ks[7], (DEPTH, MLSTM_HEADS), 0.1),
        "fox_out_norm_w": 1.0 + nrm(ks[8], (DEPTH, FOX_WIDTH), 0.02),
        "mlstm_out_norm_w": 1.0 + nrm(ks[9], (DEPTH, MLSTM_V_WIDTH), 0.02),
        "w_out": nrm(ks[10], (DEPTH, MIX_WIDTH, D_MODEL), MIX_WIDTH ** -0.5),
        "norm2_w": 1.0 + nrm(ks[11], (DEPTH, D_MODEL), 0.02),
        "peer_w_q": nrm(ks[12], (DEPTH, D_MODEL, PEER_HEADS * PEER_KEY_DIM), D_MODEL ** -0.5),
        "peer_keys": nrm(ks[13], (DEPTH, 2, PEER_N_KEYS, PEER_KEY_DIM // 2), (PEER_KEY_DIM // 2) ** -0.5),
        "peer_u": nrm(ks[14], (DEPTH, PEER_N_EXPERTS, D_MODEL), D_MODEL ** -0.5),
        "peer_v": nrm(ks[15], (DEPTH, PEER_N_EXPERTS, D_MODEL), PEER_HEADS ** -0.5),
        "final_norm_w": 1.0 + nrm(ks[16], (D_MODEL,), 0.02),
    }


def reference(x, norm1_w, w_in, fox_f_bias, mlstm_conv_w, mlstm_conv_b, mlstm_i_bias, mlstm_f_bias,
              fox_out_norm_w, mlstm_out_norm_w, w_out, norm2_w, peer_w_q, peer_keys, peer_u, peer_v,
              final_norm_w):
    B, S, _ = x.shape
    split_points = [int(p) for p in np.cumsum(IN_SPLITS)[:-1]]
    for l in range(DEPTH):
        h = rmsnorm(x, norm1_w[l])
        z = h @ w_in[l]
        fq, fk, fv, ff, mq, mk, mv, mi, mf, mo = jnp.split(z, split_points, axis=-1)

        att = forgetting_attention(fq.reshape(B, S, FOX_HEADS, FOX_HEAD_DIM),
                                   fk.reshape(B, S, FOX_HEADS, FOX_HEAD_DIM),
                                   fv.reshape(B, S, FOX_HEADS, FOX_HEAD_DIM),
                                   ff + fox_f_bias[l])

        mqk = jax.nn.silu(causal_depthwise_conv(jnp.concatenate([mq, mk], axis=-1), mlstm_conv_w[l], mlstm_conv_b[l]))
        mq_c, mk_c = jnp.split(mqk, 2, axis=-1)
        cell = mlstm_chunkwise(mq_c.reshape(B, S, MLSTM_HEADS, MLSTM_QK_DIM),
                               mk_c.reshape(B, S, MLSTM_HEADS, MLSTM_QK_DIM),
                               mv.reshape(B, S, MLSTM_HEADS, MLSTM_V_DIM),
                               mi + mlstm_i_bias[l], mf + mlstm_f_bias[l])
        cell = jax.nn.sigmoid(mo.astype(jnp.float32)).astype(cell.dtype).reshape(B, S, MLSTM_HEADS, MLSTM_V_DIM) * cell

        mixed = jnp.concatenate([head_rmsnorm(att, fox_out_norm_w[l]),
                                 head_rmsnorm(cell, mlstm_out_norm_w[l])], axis=-1)
        x = x + mixed @ w_out[l]

        x = x + peer_ffn(rmsnorm(x, norm2_w[l]), peer_w_q[l], peer_keys[l], peer_u[l], peer_v[l])
    return rmsnorm(x, final_norm_w)
```

```python
import functools
import math

import numpy as np
import jax
import jax.numpy as jnp
from jax import lax
from jax.experimental import pallas as pl
from jax.experimental.pallas import tpu as pltpu

F32 = jnp.float32
BF16 = jnp.bfloat16

D_MODEL = 2048
FOX_HEADS = 8
FOX_HEAD_DIM = 128
FOX_WIDTH = FOX_HEADS * FOX_HEAD_DIM
MLSTM_HEADS = 4
MLSTM_QK_DIM = 128
MLSTM_V_DIM = 256
MLSTM_QK_WIDTH = MLSTM_HEADS * MLSTM_QK_DIM
MLSTM_V_WIDTH = MLSTM_HEADS * MLSTM_V_DIM
MLSTM_CONV = 4
MLSTM_CHUNK = 128
PEER_HEADS = 8
PEER_KEY_DIM = 256
PEER_N_KEYS = 128
PEER_TOPK = 16
PEER_N_EXPERTS = PEER_N_KEYS * PEER_N_KEYS
NORM_EPS = 1e-6

LANES = 128
SUBLANES = 8
GATE_LANES = LANES
GATE_ROWS = 16
VMEM_LIMIT = 56 * 1024 * 1024

_CAND_COLS = tuple(PEER_TOPK // (r + 1) for r in range(PEER_TOPK))
_N_CAND = sum(_CAND_COLS)
_N_CAND_PAD = -(-_N_CAND // SUBLANES) * SUBLANES


def _cparams(sem):
    return pltpu.CompilerParams(dimension_semantics=sem, vmem_limit_bytes=VMEM_LIMIT)


def _inproj_body(x_ref, nw_ref, w_ref, wg_ref, z_ref, zc_ref, g_ref, h_scr, *, n_main, q_blocks, q_scale):
    j = pl.program_id(1)

    @pl.when(j == 0)
    def _():
        x = x_ref[...]
        ms = jnp.mean(x * x, axis=-1, keepdims=True)
        hb = ((x * lax.rsqrt(ms + NORM_EPS)) * nw_ref[...]).astype(BF16)
        h_scr[...] = hb
        g_ref[...] = jnp.dot(hb, wg_ref[...], preferred_element_type=F32)

    z = jnp.dot(h_scr[...], w_ref[...], preferred_element_type=F32)

    @pl.when(j < n_main)
    def _():
        scale = jnp.where(j < q_blocks, q_scale, 1.0).astype(F32)
        z_ref[...] = (z * scale).astype(BF16)

    @pl.when(j >= n_main)
    def _():
        zc_ref[...] = z


def _inproj(x2, norm_w, w_main, w_gate, *, tm=512, tn=512):
    T = x2.shape[0]
    n_cols = w_main.shape[1]
    n_conv = 2 * MLSTM_QK_WIDTH
    n_main = (n_cols - n_conv) // tn
    n_blocks = n_cols // tn
    body = functools.partial(_inproj_body, n_main=n_main, q_blocks=FOX_WIDTH // tn,
                             q_scale=FOX_HEAD_DIM ** -0.5)
    return pl.pallas_call(
        body,
        grid=(T // tm, n_blocks),
        in_specs=[
            pl.BlockSpec((tm, D_MODEL), lambda i, j: (i, 0)),
            pl.BlockSpec((1, D_MODEL), lambda i, j: (0, 0)),
            pl.BlockSpec((D_MODEL, tn), lambda i, j: (0, j)),
            pl.BlockSpec((D_MODEL, GATE_LANES), lambda i, j: (0, 0)),
        ],
        out_specs=[
            pl.BlockSpec((tm, tn), lambda i, j: (i, jnp.minimum(j, n_main - 1))),
            pl.BlockSpec((tm, tn), lambda i, j: (i, jnp.maximum(j - n_main, 0))),
            pl.BlockSpec((tm, GATE_LANES), lambda i, j: (i, 0)),
        ],
        out_shape=[
            jax.ShapeDtypeStruct((T, n_cols - n_conv), BF16),
            jax.ShapeDtypeStruct((T, n_conv), F32),
            jax.ShapeDtypeStruct((T, GATE_LANES), F32),
        ],
        scratch_shapes=[pltpu.VMEM((tm, D_MODEL), BF16)],
        compiler_params=_cparams(("parallel", "arbitrary")),
        name="inproj",
    )(x2, norm_w, w_main, w_gate)


def _gates_body(g_ref, bias_ref, tri_ref, gc_ref, gr_ref, carry_scr, *, rows):
    c = pl.program_id(1)

    @pl.when(c == 0)
    def _():
        carry_scr[...] = jnp.zeros_like(carry_scr)

    lane = lax.broadcasted_iota(jnp.int32, (LANES, GATE_LANES), 1)
    is_glob = lane < FOX_HEADS
    is_ls = is_glob | ((lane >= FOX_HEADS + MLSTM_HEADS) & (lane < FOX_HEADS + 2 * MLSTM_HEADS))
    tri = tri_ref[...]
    for s in range(rows // LANES):
        sl = slice(s * LANES, (s + 1) * LANES)
        g = g_ref[sl, :] + bias_ref[...]
        ls = jnp.minimum(g, 0.0) - jnp.log1p(jnp.exp(-jnp.abs(g)))
        v = jnp.where(is_ls, ls, 0.0)
        hi = v.astype(BF16)
        r1 = v - hi.astype(F32)
        mid = r1.astype(BF16)
        lo = (r1 - mid.astype(F32)).astype(BF16)
        cs = (jnp.dot(tri, hi, preferred_element_type=F32)
              + jnp.dot(tri, mid, preferred_element_type=F32)
              + jnp.dot(tri, lo, preferred_element_type=F32))
        glob = cs + carry_scr[...]
        carry_scr[...] = glob[LANES - 1:LANES, :]
        out = jnp.where(is_glob, glob, jnp.where(is_ls, cs, g))
        gc_ref[sl, :] = out
        gr_ref[:, sl] = out.T[0:GATE_ROWS, :]


def _gates(g, bias, tri, *, batch, seq, rows=1024):
    T = g.shape[0]
    nblk = seq // rows
    return pl.pallas_call(
        functools.partial(_gates_body, rows=rows),
        grid=(batch, nblk),
        in_specs=[
            pl.BlockSpec((rows, GATE_LANES), lambda b, c: (b * nblk + c, 0)),
            pl.BlockSpec((1, GATE_LANES), lambda b, c: (0, 0)),
            pl.BlockSpec((LANES, LANES), lambda b, c: (0, 0)),
        ],
        out_specs=[
            pl.BlockSpec((rows, GATE_LANES), lambda b, c: (b * nblk + c, 0)),
            pl.BlockSpec((GATE_ROWS, rows), lambda b, c: (0, b * nblk + c)),
        ],
        out_shape=[
            jax.ShapeDtypeStruct((T, GATE_LANES), F32),
            jax.ShapeDtypeStruct((GATE_ROWS, T), F32),
        ],
        scratch_shapes=[pltpu.VMEM((1, GATE_LANES), F32)],
        compiler_params=_cparams(("parallel", "arbitrary")),
        name="gates",
    )(g, bias, tri)


def _fox_body(q_ref, k_ref, v_ref, gc_ref, gr_ref, nw_ref, o_ref, *, tq):
    h = pl.program_id(1)
    qi = pl.program_id(2)
    q = q_ref[...]
    lane = lax.broadcasted_iota(jnp.int32, (tq, GATE_LANES), 1)
    fq = jnp.sum(jnp.where(lane == h, gc_ref[...], 0.0), axis=-1, keepdims=True)

    def step(ki, carry, masked):
        m, l, acc = carry
        start = pl.multiple_of(ki * tq, tq)
        k = k_ref[pl.ds(start, tq), :]
        v = v_ref[pl.ds(start, tq), :]
        s = lax.dot_general(q, k, (((1,), (1,)), ((), ())), preferred_element_type=F32)
        fk = gr_ref[pl.ds(h, 1), pl.ds(start, tq)]
        s = s + (fq - fk)
        if masked:
            row = lax.broadcasted_iota(jnp.int32, (tq, tq), 0)
            col = lax.broadcasted_iota(jnp.int32, (tq, tq), 1)
            s = jnp.where(row >= col, s, -jnp.inf)
        m_new = jnp.maximum(m, jnp.max(s, axis=-1, keepdims=True))
        alpha = jnp.exp(m - m_new)
        p = jnp.exp(s - m_new)
        l = alpha * l + jnp.sum(p, axis=-1, keepdims=True)
        acc = alpha * acc + jnp.dot(p.astype(BF16), v, preferred_element_type=F32)
        return m_new, l, acc

    init = (jnp.full((tq, 1), -jnp.inf, F32), jnp.zeros((tq, 1), F32), jnp.zeros((tq, FOX_HEAD_DIM), F32))
    carry = lax.fori_loop(0, qi, lambda ki, c: step(ki, c, False), init)
    _, l, acc = step(qi, carry, True)
    out = acc / l
    ms = jnp.mean(out * out, axis=-1, keepdims=True)
    o_ref[...] = ((out * lax.rsqrt(ms + NORM_EPS)) * nw_ref[pl.ds(h, 1), :]).astype(BF16)


def _fox(z, gc, gr, nw, *, batch, seq, tq=512):
    T = z.shape[0]
    nq = seq // tq
    kcol = FOX_WIDTH // FOX_HEAD_DIM
    return pl.pallas_call(
        functools.partial(_fox_body, tq=tq),
        grid=(batch, FOX_HEADS, nq),
        in_specs=[
            pl.BlockSpec((tq, FOX_HEAD_DIM), lambda b, h, i: (b * nq + i, h)),
            pl.BlockSpec((seq, FOX_HEAD_DIM), lambda b, h, i: (b, kcol + h)),
            pl.BlockSpec((seq, FOX_HEAD_DIM), lambda b, h, i: (b, 2 * kcol + h)),
            pl.BlockSpec((tq, GATE_LANES), lambda b, h, i: (b * nq + i, 0)),
            pl.BlockSpec((GATE_ROWS, seq), lambda b, h, i: (0, b)),
            pl.BlockSpec((FOX_HEADS, FOX_HEAD_DIM), lambda b, h, i: (0, 0)),
        ],
        out_specs=pl.BlockSpec((tq, FOX_HEAD_DIM), lambda b, h, i: (b * nq + i, h)),
        out_shape=jax.ShapeDtypeStruct((T, FOX_WIDTH), BF16),
        compiler_params=_cparams(("parallel", "parallel", "arbitrary")),
        name="fox",
    )(z, z, z, gc, gr, nw)


def _mlstm_body(zc_ref, zprev_ref, v_ref, o_ref, gc_ref, gr_ref, cw_ref, cb_ref, nw_ref, out_ref,
                full_scr, c_scr, n_scr, m_scr):
    c = pl.program_id(1)
    L = MLSTM_CHUNK
    dk = MLSTM_QK_DIM
    dv = MLSTM_V_DIM

    @pl.when(c == 0)
    def _():
        c_scr[...] = jnp.zeros_like(c_scr)
        n_scr[...] = jnp.zeros_like(n_scr)
        m_scr[...] = jnp.zeros_like(m_scr)

    full_scr[0:SUBLANES, :] = jnp.where(c == 0, 0.0, zprev_ref[...])
    full_scr[SUBLANES:SUBLANES + L, :] = zc_ref[...]
    y = cb_ref[...]
    for j in range(MLSTM_CONV):
        y = y + cw_ref[j:j + 1, :] * full_scr[pl.ds(SUBLANES - (MLSTM_CONV - 1) + j, L), :]
    qk_all = y * jax.nn.sigmoid(y)

    gcb = gc_ref[...]
    grb = gr_ref[...]
    row = lax.broadcasted_iota(jnp.int32, (L, L), 0)
    col = lax.broadcasted_iota(jnp.int32, (L, L), 1)
    causal = row >= col
    for hh in range(MLSTM_HEADS):
        qh = qk_all[:, hh * dk:(hh + 1) * dk]
        kh = qk_all[:, MLSTM_QK_WIDTH + hh * dk:MLSTM_QK_WIDTH + (hh + 1) * dk] * (dk ** -0.5)
        vh = v_ref[:, hh * dv:(hh + 1) * dv]
        oh = o_ref[:, hh * dv:(hh + 1) * dv].astype(F32)
        li = FOX_HEADS + hh
        lf = FOX_HEADS + MLSTM_HEADS + hh
        i_col = gcb[:, li:li + 1]
        b_col = gcb[:, lf:lf + 1]
        i_row = grb[li:li + 1, :]
        b_row = grb[lf:lf + 1, :]
        b_last = b_row[:, L - 1:L]

        c_prev = c_scr[hh]
        n_prev = n_scr[hh]
        m_prev = m_scr[hh][:, 0:1]

        m_loc = jnp.max(b_last - b_row + i_row, axis=-1, keepdims=True)
        a_end = jnp.exp(b_last - b_col + i_col - m_loc)
        kw = kh * a_end
        c_loc = lax.dot_general(kw.astype(BF16), vh, (((0,), (0,)), ((), ())), preferred_element_type=F32)
        n_loc = jnp.sum(kw, axis=0, keepdims=True)

        g_col = b_col + m_prev
        dmat = jnp.where(causal, b_col - b_row + i_row, -jnp.inf)
        m_t = jnp.maximum(g_col, jnp.max(dmat, axis=-1, keepdims=True))
        qb = qh.astype(BF16)
        qk = lax.dot_general(qb, kh.astype(BF16), (((1,), (1,)), ((), ())), preferred_element_type=F32)
        sm = qk * jnp.exp(dmat - m_t)
        inter = jnp.exp(g_col - m_t)
        num = (jnp.dot(sm.astype(BF16), vh, preferred_element_type=F32)
               + inter * jnp.dot(qb, c_prev.astype(BF16), preferred_element_type=F32))
        den = jnp.sum(sm, axis=-1, keepdims=True) + inter * jnp.sum(qh * n_prev, axis=-1, keepdims=True)
        cell = num / jnp.maximum(jnp.abs(den), jnp.exp(-m_t))
        gated = jax.nn.sigmoid(oh) * cell
        ms = jnp.mean(gated * gated, axis=-1, keepdims=True)
        out_ref[:, hh * dv:(hh + 1) * dv] = (
            (gated * lax.rsqrt(ms + NORM_EPS)) * nw_ref[:, hh * dv:(hh + 1) * dv]).astype(BF16)

        m_new = jnp.maximum(b_last + m_prev, m_loc)
        a_prev = jnp.exp(b_last + m_prev - m_new)
        a_loc = jnp.exp(m_loc - m_new)
        c_scr[hh] = a_prev * c_prev + a_loc * c_loc
        n_scr[hh] = a_prev * n_prev + a_loc * n_loc
        m_scr[hh] = jnp.broadcast_to(m_new, (1, LANES))


def _mlstm(zc, z, gc, gr, conv_w, conv_b, nw, *, batch, seq):
    T = zc.shape[0]
    L = MLSTM_CHUNK
    nc = seq // L
    per = L // SUBLANES
    vcol = 3 * FOX_WIDTH // MLSTM_V_WIDTH
    return pl.pallas_call(
        _mlstm_body,
        grid=(batch, nc),
        in_specs=[
            pl.BlockSpec((L, 2 * MLSTM_QK_WIDTH), lambda b, c: (b * nc + c, 0)),
            pl.BlockSpec((SUBLANES, 2 * MLSTM_QK_WIDTH), lambda b, c: (jnp.maximum((b * nc + c) * per - 1, 0), 0)),
            pl.BlockSpec((L, MLSTM_V_WIDTH), lambda b, c: (b * nc + c, vcol)),
            pl.BlockSpec((L, MLSTM_V_WIDTH), lambda b, c: (b * nc + c, vcol + 1)),
            pl.BlockSpec((L, GATE_LANES), lambda b, c: (b * nc + c, 0)),
            pl.BlockSpec((GATE_ROWS, L), lambda b, c: (0, b * nc + c)),
            pl.BlockSpec((MLSTM_CONV, 2 * MLSTM_QK_WIDTH), lambda b, c: (0, 0)),
            pl.BlockSpec((1, 2 * MLSTM_QK_WIDTH), lambda b, c: (0, 0)),
            pl.BlockSpec((1, MLSTM_V_WIDTH), lambda b, c: (0, 0)),
        ],
        out_specs=pl.BlockSpec((L, MLSTM_V_WIDTH), lambda b, c: (b * nc + c, 0)),
        out_shape=jax.ShapeDtypeStruct((T, MLSTM_V_WIDTH), BF16),
        scratch_shapes=[
            pltpu.VMEM((SUBLANES + L, 2 * MLSTM_QK_WIDTH), F32),
            pltpu.VMEM((MLSTM_HEADS, MLSTM_QK_DIM, MLSTM_V_DIM), F32),
            pltpu.VMEM((MLSTM_HEADS, 1, MLSTM_QK_DIM), F32),
            pltpu.VMEM((MLSTM_HEADS, 1, LANES), F32),
        ],
        compiler_params=_cparams(("parallel", "arbitrary")),
        name="mlstm",
    )(zc, zc, z, z, gc, gr, conv_w, conv_b, nw)


def _outproj_body(att_ref, cell_ref, x_ref, wa_ref, wb_ref, n2_ref, x1_ref, h2t_ref):
    y = (jnp.dot(att_ref[...], wa_ref[...], preferred_element_type=F32)
         + jnp.dot(cell_ref[...], wb_ref[...], preferred_element_type=F32))
    x1 = x_ref[...] + y
    x1_ref[...] = x1
    ms = jnp.mean(x1 * x1, axis=-1, keepdims=True)
    h2 = (x1 * lax.rsqrt(ms + NORM_EPS)) * n2_ref[...]
    h2t_ref[...] = h2.T.astype(BF16)


def _outproj(att, cell, x2, wa, wb, n2, *, tm=512):
    T = x2.shape[0]
    return pl.pallas_call(
        _outproj_body,
        grid=(T // tm,),
        in_specs=[
            pl.BlockSpec((tm, FOX_WIDTH), lambda i: (i, 0)),
            pl.BlockSpec((tm, MLSTM_V_WIDTH), lambda i: (i, 0)),
            pl.BlockSpec((tm, D_MODEL), lambda i: (i, 0)),
            pl.BlockSpec((FOX_WIDTH, D_MODEL), lambda i: (0, 0)),
            pl.BlockSpec((MLSTM_V_WIDTH, D_MODEL), lambda i: (0, 0)),
            pl.BlockSpec((1, D_MODEL), lambda i: (0, 0)),
        ],
        out_specs=[
            pl.BlockSpec((tm, D_MODEL), lambda i: (i, 0)),
            pl.BlockSpec((D_MODEL, tm), lambda i: (0, i)),
        ],
        out_shape=[
            jax.ShapeDtypeStruct((T, D_MODEL), F32),
            jax.ShapeDtypeStruct((D_MODEL, T), BF16),
        ],
        compiler_params=_cparams(("parallel",)),
        name="outproj",
    )(att, cell, x2, wa, wb, n2)


def _extract16(s, n_rows, tb, store_row):
    iota = lax.broadcasted_iota(jnp.int32, (n_rows, tb), 0).astype(F32)

    def body(r, carry):
        sw, rank = carry
        mx = jnp.max(sw, axis=0, keepdims=True)
        first = jnp.min(jnp.where(sw == mx, iota, float(n_rows)), axis=0, keepdims=True)
        hit = iota == first
        store_row(r, mx)
        return jnp.where(hit, -jnp.inf, sw), jnp.where(hit, r.astype(F32), rank)

    _, rank = lax.fori_loop(0, PEER_TOPK, body, (s, jnp.full((n_rows, tb), float(PEER_TOPK), F32)))
    return rank


def _peer_sel_body(h2t_ref, wqt_ref, keys_ref, grp_ref, e1_ref, c1_ref, n0_ref, w0_ref,
                   qt_scr, vals_scr, cand_scr, pe_scr, *, tb):
    half = PEER_KEY_DIM // 2
    qt_scr[...] = jnp.dot(wqt_ref[...], h2t_ref[...], preferred_element_type=F32).astype(BF16)

    for h in range(PEER_HEADS):
        s0 = jnp.dot(keys_ref[0], qt_scr[(2 * h) * half:(2 * h + 1) * half, :], preferred_element_type=F32)
        s1 = jnp.dot(keys_ref[1], qt_scr[(2 * h + 1) * half:(2 * h + 2) * half, :], preferred_element_type=F32)

        def store0(r, mx):
            vals_scr[0, pl.ds(r, 1), :] = mx

        def store1(r, mx):
            vals_scr[1, pl.ds(r, 1), :] = mx

        rank0 = _extract16(s0, PEER_N_KEYS, tb, store0)
        rank1 = _extract16(s1, PEER_N_KEYS, tb, store1)
        a = vals_scr[0]
        b = vals_scr[1]
        ea = jnp.exp(a - a[0:1, :])
        eb = jnp.exp(b - b[0:1, :])
        off = 0
        for r in range(PEER_TOPK):
            ncol = _CAND_COLS[r]
            cand_scr[off:off + ncol, :] = a[r:r + 1, :] + b[0:ncol, :]
            pe_scr[off:off + ncol, :] = ea[r:r + 1, :] * eb[0:ncol, :]
            off += ncol
        cand_scr[_N_CAND:_N_CAND_PAD, :] = jnp.full((_N_CAND_PAD - _N_CAND, tb), -jnp.inf, F32)
        pe_scr[_N_CAND:_N_CAND_PAD, :] = jnp.zeros((_N_CAND_PAD - _N_CAND, tb), F32)

        crank = _extract16(cand_scr[...], _N_CAND_PAD, tb, lambda r, mx: None)
        sel = (crank < float(PEER_TOPK)).astype(F32)
        zsum = jnp.sum(sel * pe_scr[...], axis=0, keepdims=True)
        nr = jnp.dot(grp_ref[...], sel.astype(BF16), preferred_element_type=F32)
        n0 = jnp.zeros((PEER_N_KEYS, tb), F32)
        for r in range(PEER_TOPK):
            n0 = jnp.where(rank0 == float(r), nr[r:r + 1, :], n0)
        n0_ref[h] = n0
        w0_ref[h] = jnp.exp(s0 - a[0:1, :]) / zsum
        e1_ref[h] = jnp.exp(s1 - b[0:1, :])
        c1_ref[h] = rank1


def _peer_sel(h2t, wqt, keys, grp, *, tb=256):
    T = h2t.shape[1]
    tab = jax.ShapeDtypeStruct((PEER_HEADS, PEER_N_KEYS, T), F32)
    tab_spec = pl.BlockSpec((PEER_HEADS, PEER_N_KEYS, tb), lambda i: (0, 0, i))
    return pl.pallas_call(
        functools.partial(_peer_sel_body, tb=tb),
        grid=(T // tb,),
        in_specs=[
            pl.BlockSpec((D_MODEL, tb), lambda i: (0, i)),
            pl.BlockSpec((PEER_HEADS * PEER_KEY_DIM, D_MODEL), lambda i: (0, 0)),
            pl.BlockSpec((2, PEER_N_KEYS, PEER_KEY_DIM // 2), lambda i: (0, 0, 0)),
            pl.BlockSpec((PEER_TOPK, _N_CAND_PAD), lambda i: (0, 0)),
        ],
        out_specs=[tab_spec, tab_spec, tab_spec, tab_spec],
        out_shape=[tab, tab, tab, tab],
        scratch_shapes=[
            pltpu.VMEM((PEER_HEADS * PEER_KEY_DIM, tb), BF16),
            pltpu.VMEM((2, PEER_TOPK, tb), F32),
            pltpu.VMEM((_N_CAND_PAD, tb), F32),
            pltpu.VMEM((_N_CAND_PAD, tb), F32),
        ],
        compiler_params=_cparams(("parallel",)),
        name="peer_sel",
    )(h2t, wqt, keys, grp)


def _gelu(x):
    return 0.5 * x * (1.0 + lax.erf(x * math.sqrt(0.5)))


def _peer_dense_body(u_ref, vt_ref, h2t_ref, e1_ref, c1_ref, n0_ref, w0_ref, yt_ref, at_scr, *, ec, tb):
    k = pl.program_id(1)
    slabs = ec // PEER_N_KEYS

    @pl.when(k == 0)
    def _():
        yt_ref[...] = jnp.zeros_like(yt_ref)

    st = jnp.dot(u_ref[...], h2t_ref[...], preferred_element_type=F32)
    for ii in range(slabs):
        i = k * slabs + ii
        g = jnp.zeros((PEER_N_KEYS, tb), F32)
        for h in range(PEER_HEADS):
            n_row = n0_ref[h, pl.ds(i, 1), :]
            w_row = w0_ref[h, pl.ds(i, 1), :]
            g = g + jnp.where(c1_ref[h] < n_row, e1_ref[h] * w_row, 0.0)
        sl = slice(ii * PEER_N_KEYS, (ii + 1) * PEER_N_KEYS)
        at_scr[sl, :] = (_gelu(st[sl, :]) * g).astype(BF16)
    yt_ref[...] += jnp.dot(vt_ref[...], at_scr[...], preferred_element_type=F32)


def _peer_dense(u, vt, h2t, e1, c1, n0, w0, *, tb=512, ec=512):
    T = h2t.shape[1]
    tab_spec = pl.BlockSpec((PEER_HEADS, PEER_N_KEYS, tb), lambda i, k: (0, 0, i))
    return pl.pallas_call(
        functools.partial(_peer_dense_body, ec=ec, tb=tb),
        grid=(T // tb, PEER_N_EXPERTS // ec),
        in_specs=[
            pl.BlockSpec((ec, D_MODEL), lambda i, k: (k, 0)),
            pl.BlockSpec((D_MODEL, ec), lambda i, k: (0, k)),
            pl.BlockSpec((D_MODEL, tb), lambda i, k: (0, i)),
            tab_spec, tab_spec, tab_spec, tab_spec,
        ],
        out_specs=pl.BlockSpec((D_MODEL, tb), lambda i, k: (0, i)),
        out_shape=jax.ShapeDtypeStruct((D_MODEL, T), F32),
        scratch_shapes=[pltpu.VMEM((ec, tb), BF16)],
        compiler_params=_cparams(("parallel", "arbitrary")),
        name="peer_dense",
    )(u, vt, h2t, e1, c1, n0, w0)


def _final_body(x1_ref, yt_ref, w_ref, o_ref):
    x2 = x1_ref[...] + yt_ref[...].T
    ms = jnp.mean(x2 * x2, axis=-1, keepdims=True)
    o_ref[...] = (x2 * lax.rsqrt(ms + NORM_EPS)) * w_ref[...]


def _final(x1, yt, w, *, tm=512):
    T = x1.shape[0]
    return pl.pallas_call(
        _final_body,
        grid=(T // tm,),
        in_specs=[
            pl.BlockSpec((tm, D_MODEL), lambda i: (i, 0)),
            pl.BlockSpec((D_MODEL, tm), lambda i: (0, i)),
            pl.BlockSpec((1, D_MODEL), lambda i: (0, 0)),
        ],
        out_specs=pl.BlockSpec((tm, D_MODEL), lambda i: (i, 0)),
        out_shape=jax.ShapeDtypeStruct((T, D_MODEL), F32),
        compiler_params=_cparams(("parallel",)),
        name="final",
    )(x1, yt, w)


def _group_matrix():
    g = np.zeros((PEER_TOPK, _N_CAND_PAD), np.float32)
    off = 0
    for r, ncol in enumerate(_CAND_COLS):
        g[r, off:off + ncol] = 1.0
        off += ncol
    return g


def _layer(x2, norm1_w, w_in, fox_f_bias, conv_w, conv_b, i_bias, f_bias, fox_nw, mlstm_nw, w_out, norm2_w,
           w_q, keys, u, v, *, batch, seq):
    splits = np.cumsum((FOX_WIDTH, FOX_WIDTH, FOX_WIDTH, FOX_HEADS, MLSTM_QK_WIDTH, MLSTM_QK_WIDTH,
                        MLSTM_V_WIDTH, MLSTM_HEADS, MLSTM_HEADS, MLSTM_V_WIDTH))[:-1]
    fq, fk, fv, ff, mq, mk, mv, mi, mf, mo = jnp.split(w_in, [int(p) for p in splits], axis=1)
    w_main = jnp.concatenate([fq, fk, fv, mv, mo, mq, mk], axis=1).astype(BF16)
    gate_pad = GATE_LANES - FOX_HEADS - 2 * MLSTM_HEADS
    w_gate = jnp.pad(jnp.concatenate([ff, mi, mf], axis=1), ((0, 0), (0, gate_pad))).astype(BF16)
    gate_bias = jnp.pad(jnp.concatenate([fox_f_bias, i_bias, f_bias]), (0, gate_pad)).reshape(1, GATE_LANES)
    tri = jnp.asarray(np.tril(np.ones((LANES, LANES), np.float32)), BF16)

    z, zc, g = _inproj(x2, norm1_w.reshape(1, D_MODEL), w_main, w_gate)
    gc, gr = _gates(g, gate_bias, tri, batch=batch, seq=seq)
    att = _fox(z, gc, gr, fox_nw.reshape(FOX_HEADS, FOX_HEAD_DIM), batch=batch, seq=seq)
    cell = _mlstm(zc, z, gc, gr, conv_w, conv_b.reshape(1, -1), mlstm_nw.reshape(1, -1), batch=batch, seq=seq)
    x1, h2t = _outproj(att, cell, x2, w_out[:FOX_WIDTH].astype(BF16), w_out[FOX_WIDTH:].astype(BF16),
                       norm2_w.reshape(1, D_MODEL))
    e1, c1, n0, w0 = _peer_sel(h2t, w_q.T.astype(BF16), keys.astype(BF16), jnp.asarray(_group_matrix(), BF16))
    yt = _peer_dense(u.astype(BF16), v.T.astype(BF16), h2t, e1, c1, n0, w0)
    return x1, yt


def kernel(x, norm1_w, w_in, fox_f_bias, mlstm_conv_w, mlstm_conv_b, mlstm_i_bias, mlstm_f_bias, fox_out_norm_w,
           mlstm_out_norm_w, w_out, norm2_w, peer_w_q, peer_keys, peer_u, peer_v, final_norm_w):
    batch, seq, _ = x.shape
    assert w_in.shape[0] == 1, "single-layer block: the final norm is fused with the last residual add"
    x2 = x.reshape(batch * seq, D_MODEL)
    x1, yt = _layer(x2, norm1_w[0], w_in[0], fox_f_bias[0], mlstm_conv_w[0], mlstm_conv_b[0], mlstm_i_bias[0],
                    mlstm_f_bias[0], fox_out_norm_w[0], mlstm_out_norm_w[0], w_out[0], norm2_w[0],
                    peer_w_q[0], peer_keys[0], peer_u[0], peer_v[0], batch=batch, seq=seq)
    out = _final(x1, yt, final_norm_w.reshape(1, D_MODEL))
    return out.reshape(batch, seq, D_MODEL)
```

```python
import functools
import math

import numpy as np
import jax
import jax.numpy as jnp
from jax import lax
from jax.experimental import pallas as pl
from jax.experimental.pallas import tpu as pltpu

F32 = jnp.float32
BF16 = jnp.bfloat16

D_MODEL = 2048
FOX_HEADS = 8
FOX_HEAD_DIM = 128
FOX_WIDTH = FOX_HEADS * FOX_HEAD_DIM
MLSTM_HEADS = 4
MLSTM_QK_DIM = 128
MLSTM_V_DIM = 256
MLSTM_QK_WIDTH = MLSTM_HEADS * MLSTM_QK_DIM
MLSTM_V_WIDTH = MLSTM_HEADS * MLSTM_V_DIM
MLSTM_CONV = 4
MLSTM_CHUNK = 128
PEER_HEADS = 8
PEER_KEY_DIM = 256
PEER_N_KEYS = 128
PEER_TOPK = 16
PEER_N_EXPERTS = PEER_N_KEYS * PEER_N_KEYS
NORM_EPS = 1e-6
LOG2E = math.log2(math.e)

LANES = 128
SUBLANES = 8
GATE_LANES = LANES
GATE_ROWS = 16
VMEM_LIMIT = 56 * 1024 * 1024

_CAND_COLS = tuple(PEER_TOPK // (r + 1) for r in range(PEER_TOPK))
_N_CAND = sum(_CAND_COLS)
_N_CAND_PAD = -(-_N_CAND // SUBLANES) * SUBLANES


def _cparams(sem):
    return pltpu.CompilerParams(dimension_semantics=sem, vmem_limit_bytes=VMEM_LIMIT)


def _inproj_body(x_ref, nw_ref, w_ref, wg_ref, z_ref, zc_ref, g_ref, h_scr, *, n_main, q_blocks, q_scale):
    j = pl.program_id(1)

    @pl.when(j == 0)
    def _():
        x = x_ref[...]
        ms = jnp.mean(x * x, axis=-1, keepdims=True)
        hb = ((x * lax.rsqrt(ms + NORM_EPS)) * nw_ref[...]).astype(BF16)
        h_scr[...] = hb
        g_ref[...] = jnp.dot(hb, wg_ref[...], preferred_element_type=F32)

    z = jnp.dot(h_scr[...], w_ref[...], preferred_element_type=F32)

    @pl.when(j < n_main)
    def _():
        scale = jnp.where(j < q_blocks, q_scale, 1.0).astype(F32)
        z_ref[...] = (z * scale).astype(BF16)

    @pl.when(j >= n_main)
    def _():
        zc_ref[...] = z


def _inproj(x2, norm_w, w_main, w_gate, *, tm=512, tn=512):
    T = x2.shape[0]
    n_cols = w_main.shape[1]
    n_conv = 2 * MLSTM_QK_WIDTH
    n_main = (n_cols - n_conv) // tn
    n_blocks = n_cols // tn
    body = functools.partial(_inproj_body, n_main=n_main, q_blocks=FOX_WIDTH // tn,
                             q_scale=FOX_HEAD_DIM ** -0.5 * LOG2E)
    return pl.pallas_call(
        body,
        grid=(T // tm, n_blocks),
        in_specs=[
            pl.BlockSpec((tm, D_MODEL), lambda i, j: (i, 0)),
            pl.BlockSpec((1, D_MODEL), lambda i, j: (0, 0)),
            pl.BlockSpec((D_MODEL, tn), lambda i, j: (0, j)),
            pl.BlockSpec((D_MODEL, GATE_LANES), lambda i, j: (0, 0)),
        ],
        out_specs=[
            pl.BlockSpec((tm, tn), lambda i, j: (i, jnp.minimum(j, n_main - 1))),
            pl.BlockSpec((tm, tn), lambda i, j: (i, jnp.maximum(j - n_main, 0))),
            pl.BlockSpec((tm, GATE_LANES), lambda i, j: (i, 0)),
        ],
        out_shape=[
            jax.ShapeDtypeStruct((T, n_cols - n_conv), BF16),
            jax.ShapeDtypeStruct((T, n_conv), F32),
            jax.ShapeDtypeStruct((T, GATE_LANES), F32),
        ],
        scratch_shapes=[pltpu.VMEM((tm, D_MODEL), BF16)],
        compiler_params=_cparams(("parallel", "arbitrary")),
        name="inproj",
    )(x2, norm_w, w_main, w_gate)


def _split3(v):
    hi = v.astype(BF16)
    r1 = v - hi.astype(F32)
    mid = r1.astype(BF16)
    lo = (r1 - mid.astype(F32)).astype(BF16)
    return hi, mid, lo


def _gates_body(g_ref, bias_ref, tri_ref, pq_ref, pk_ref, gc_ref, gr_ref, qx_ref, kx_ref, carry_scr, *, rows):
    c = pl.program_id(1)

    @pl.when(c == 0)
    def _():
        carry_scr[...] = jnp.zeros_like(carry_scr)

    lane = lax.broadcasted_iota(jnp.int32, (LANES, GATE_LANES), 1)
    is_glob = lane < FOX_HEADS
    is_ls = is_glob | ((lane >= FOX_HEADS + MLSTM_HEADS) & (lane < FOX_HEADS + 2 * MLSTM_HEADS))
    tri = tri_ref[...]
    for s in range(rows // LANES):
        sl = slice(s * LANES, (s + 1) * LANES)
        g = g_ref[sl, :] + bias_ref[...]
        ls = jnp.minimum(g, 0.0) - jnp.log1p(jnp.exp(-jnp.abs(g)))
        v = jnp.where(is_ls, ls, 0.0)
        hi, mid, lo = _split3(v)
        cs = (jnp.dot(tri, hi, preferred_element_type=F32)
              + jnp.dot(tri, mid, preferred_element_type=F32)
              + jnp.dot(tri, lo, preferred_element_type=F32))
        glob = cs + carry_scr[...]
        carry_scr[...] = glob[LANES - 1:LANES, :]
        out = jnp.where(is_glob, glob, jnp.where(is_ls, cs, g))
        gc_ref[sl, :] = out
        gr_ref[:, sl] = out.T[0:GATE_ROWS, :]
        fhi, fmid, flo = _split3(glob * LOG2E)
        pieces = jnp.concatenate([fhi, fmid, flo, jnp.ones((LANES, GATE_LANES), BF16)], axis=1)
        qx_ref[sl, :] = jnp.dot(pieces, pq_ref[...], preferred_element_type=F32).astype(BF16)
        kx_ref[sl, :] = jnp.dot(pieces, pk_ref[...], preferred_element_type=F32).astype(BF16)


def _placement_matrices():
    n = 3
    pq = np.zeros((4 * GATE_LANES, FOX_WIDTH), np.float32)
    pk = np.zeros((4 * GATE_LANES, FOX_WIDTH), np.float32)
    for h in range(FOX_HEADS):
        for p in range(n):
            pq[p * GATE_LANES + h, h * FOX_HEAD_DIM + p] = 1.0
            pk[n * GATE_LANES + h, h * FOX_HEAD_DIM + p] = 1.0
            pq[n * GATE_LANES + h, h * FOX_HEAD_DIM + n + p] = 1.0
            pk[p * GATE_LANES + h, h * FOX_HEAD_DIM + n + p] = -1.0
    return pq, pk


def _gates(g, bias, tri, pq, pk, *, batch, seq, rows=1024):
    T = g.shape[0]
    nblk = seq // rows
    return pl.pallas_call(
        functools.partial(_gates_body, rows=rows),
        grid=(batch, nblk),
        in_specs=[
            pl.BlockSpec((rows, GATE_LANES), lambda b, c: (b * nblk + c, 0)),
            pl.BlockSpec((1, GATE_LANES), lambda b, c: (0, 0)),
            pl.BlockSpec((LANES, LANES), lambda b, c: (0, 0)),
            pl.BlockSpec((4 * GATE_LANES, FOX_WIDTH), lambda b, c: (0, 0)),
            pl.BlockSpec((4 * GATE_LANES, FOX_WIDTH), lambda b, c: (0, 0)),
        ],
        out_specs=[
            pl.BlockSpec((rows, GATE_LANES), lambda b, c: (b * nblk + c, 0)),
            pl.BlockSpec((GATE_ROWS, rows), lambda b, c: (0, b * nblk + c)),
            pl.BlockSpec((rows, FOX_WIDTH), lambda b, c: (b * nblk + c, 0)),
            pl.BlockSpec((rows, FOX_WIDTH), lambda b, c: (b * nblk + c, 0)),
        ],
        out_shape=[
            jax.ShapeDtypeStruct((T, GATE_LANES), F32),
            jax.ShapeDtypeStruct((GATE_ROWS, T), F32),
            jax.ShapeDtypeStruct((T, FOX_WIDTH), BF16),
            jax.ShapeDtypeStruct((T, FOX_WIDTH), BF16),
        ],
        scratch_shapes=[pltpu.VMEM((1, GATE_LANES), F32)],
        compiler_params=_cparams(("parallel", "arbitrary")),
        name="gates",
    )(g, bias, tri, pq, pk)


def _fox_body(q_ref, qx_ref, k_ref, kx_ref, v_ref, nw_ref, o_ref, *, tk):
    h = pl.program_id(1)
    qi = pl.program_id(2)
    q_halves = [jnp.concatenate([q_ref[r * tk:(r + 1) * tk, :], qx_ref[r * tk:(r + 1) * tk, :]], axis=1)
                for r in range(2)]
    row = lax.broadcasted_iota(jnp.int32, (tk, tk), 0)
    col = lax.broadcasted_iota(jnp.int32, (tk, tk), 1)

    def load_kv(ki):
        start = pl.multiple_of(ki * tk, tk)
        k = jnp.concatenate([k_ref[pl.ds(start, tk), :], kx_ref[pl.ds(start, tk), :]], axis=1)
        return k, v_ref[pl.ds(start, tk), :]

    def logits(q, k, masked):
        s = lax.dot_general(q, k, (((1,), (1,)), ((), ())), preferred_element_type=F32)
        return jnp.where(row >= col, s, -jnp.inf) if masked else s

    def weights(s, carry):
        m, l, acc = carry
        m_new = jnp.maximum(m, jnp.max(s, axis=-1, keepdims=True))
        alpha = jnp.exp2(m - m_new)
        p = jnp.exp2(s - m_new)
        return m_new, alpha * l + jnp.sum(p, axis=-1, keepdims=True), alpha * acc, p.astype(BF16)

    def update(q, k, v, carry, masked):
        m, l, acc, p = weights(logits(q, k, masked), carry)
        return m, l, acc + jnp.dot(p, v, preferred_element_type=F32)

    def both(ki, carries):
        k, v = load_kv(ki)
        s = [logits(q_halves[r], k, False) for r in range(2)]
        w = [weights(s[r], carries[r]) for r in range(2)]
        return tuple((m, l, acc + jnp.dot(p, v, preferred_element_type=F32)) for m, l, acc, p in w)

    init = (jnp.full((tk, 1), -jnp.inf, F32), jnp.zeros((tk, 1), F32), jnp.zeros((tk, FOX_HEAD_DIM), F32))
    ca, cb = lax.fori_loop(0, 2 * qi, both, (init, init))
    k, v = load_kv(2 * qi)
    ca = update(q_halves[0], k, v, ca, True)
    cb = update(q_halves[1], k, v, cb, False)
    k, v = load_kv(2 * qi + 1)
    cb = update(q_halves[1], k, v, cb, True)
    for r, (_, l, acc) in enumerate((ca, cb)):
        out = acc / l
        ms = jnp.mean(out * out, axis=-1, keepdims=True)
        o_ref[r * tk:(r + 1) * tk, :] = ((out * lax.rsqrt(ms + NORM_EPS)) * nw_ref[pl.ds(h, 1), :]).astype(BF16)


def _fox(z, qx, kx, nw, *, batch, seq, tk=512):
    T = z.shape[0]
    tq = 2 * tk
    nq = seq // tq
    kcol = FOX_WIDTH // FOX_HEAD_DIM
    return pl.pallas_call(
        functools.partial(_fox_body, tk=tk),
        grid=(batch, FOX_HEADS, nq),
        in_specs=[
            pl.BlockSpec((tq, FOX_HEAD_DIM), lambda b, h, i: (b * nq + i, h)),
            pl.BlockSpec((tq, FOX_HEAD_DIM), lambda b, h, i: (b * nq + i, h)),
            pl.BlockSpec((seq, FOX_HEAD_DIM), lambda b, h, i: (b, kcol + h)),
            pl.BlockSpec((seq, FOX_HEAD_DIM), lambda b, h, i: (b, h)),
            pl.BlockSpec((seq, FOX_HEAD_DIM), lambda b, h, i: (b, 2 * kcol + h)),
            pl.BlockSpec((FOX_HEADS, FOX_HEAD_DIM), lambda b, h, i: (0, 0)),
        ],
        out_specs=pl.BlockSpec((tq, FOX_HEAD_DIM), lambda b, h, i: (b * nq + i, h)),
        out_shape=jax.ShapeDtypeStruct((T, FOX_WIDTH), BF16),
        compiler_params=_cparams(("parallel", "parallel", "arbitrary")),
        name="fox",
    )(z, qx, z, kx, z, nw)


def _mlstm_body(zc_ref, zprev_ref, v_ref, o_ref, gc_ref, gr_ref, cw_ref, cb_ref, nw_ref, out_ref,
                full_scr, c_scr, n_scr, m_scr):
    c = pl.program_id(1)
    L = MLSTM_CHUNK
    dk = MLSTM_QK_DIM
    dv = MLSTM_V_DIM

    @pl.when(c == 0)
    def _():
        c_scr[...] = jnp.zeros_like(c_scr)
        n_scr[...] = jnp.zeros_like(n_scr)
        m_scr[...] = jnp.zeros_like(m_scr)

    full_scr[0:SUBLANES, :] = jnp.where(c == 0, 0.0, zprev_ref[...])
    full_scr[SUBLANES:SUBLANES + L, :] = zc_ref[...]
    y = cb_ref[...]
    for j in range(MLSTM_CONV):
        y = y + cw_ref[j:j + 1, :] * full_scr[pl.ds(SUBLANES - (MLSTM_CONV - 1) + j, L), :]
    qk_all = y * jax.nn.sigmoid(y)

    gcb = gc_ref[...]
    grb = gr_ref[...]
    row = lax.broadcasted_iota(jnp.int32, (L, L), 0)
    col = lax.broadcasted_iota(jnp.int32, (L, L), 1)
    causal = row >= col
    for hh in range(MLSTM_HEADS):
        qh = qk_all[:, hh * dk:(hh + 1) * dk]
        kh = qk_all[:, MLSTM_QK_WIDTH + hh * dk:MLSTM_QK_WIDTH + (hh + 1) * dk] * (dk ** -0.5)
        vh = v_ref[:, hh * dv:(hh + 1) * dv]
        oh = o_ref[:, hh * dv:(hh + 1) * dv].astype(F32)
        li = FOX_HEADS + hh
        lf = FOX_HEADS + MLSTM_HEADS + hh
        i_col = gcb[:, li:li + 1]
        b_col = gcb[:, lf:lf + 1]
        i_row = grb[li:li + 1, :]
        b_row = grb[lf:lf + 1, :]
        b_last = b_row[:, L - 1:L]

        c_prev = c_scr[hh]
        n_prev = n_scr[hh]
        m_prev = m_scr[hh][:, 0:1]

        m_loc = jnp.max(b_last - b_row + i_row, axis=-1, keepdims=True)
        a_end = jnp.exp(b_last - b_col + i_col - m_loc)
        kw = kh * a_end
        c_loc = lax.dot_general(kw.astype(BF16), vh, (((0,), (0,)), ((), ())), preferred_element_type=F32)
        n_loc = jnp.sum(kw, axis=0, keepdims=True)

        g_col = b_col + m_prev
        dmat = jnp.where(causal, b_col - b_row + i_row, -jnp.inf)
        m_t = jnp.maximum(g_col, jnp.max(dmat, axis=-1, keepdims=True))
        qb = qh.astype(BF16)
        qk = lax.dot_general(qb, kh.astype(BF16), (((1,), (1,)), ((), ())), preferred_element_type=F32)
        sm = qk * jnp.exp(dmat - m_t)
        inter = jnp.exp(g_col - m_t)
        num = (jnp.dot(sm.astype(BF16), vh, preferred_element_type=F32)
               + inter * jnp.dot(qb, c_prev.astype(BF16), preferred_element_type=F32))
        den = jnp.sum(sm, axis=-1, keepdims=True) + inter * jnp.sum(qh * n_prev, axis=-1, keepdims=True)
        cell = num / jnp.maximum(jnp.abs(den), jnp.exp(-m_t))
        gated = jax.nn.sigmoid(oh) * cell
        ms = jnp.mean(gated * gated, axis=-1, keepdims=True)
        out_ref[:, hh * dv:(hh + 1) * dv] = (
            (gated * lax.rsqrt(ms + NORM_EPS)) * nw_ref[:, hh * dv:(hh + 1) * dv]).astype(BF16)

        m_new = jnp.maximum(b_last + m_prev, m_loc)
        a_prev = jnp.exp(b_last + m_prev - m_new)
        a_loc = jnp.exp(m_loc - m_new)
        c_scr[hh] = a_prev * c_prev + a_loc * c_loc
        n_scr[hh] = a_prev * n_prev + a_loc * n_loc
        m_scr[hh] = jnp.broadcast_to(m_new, (1, LANES))


def _mlstm(zc, z, gc, gr, conv_w, conv_b, nw, *, batch, seq):
    T = zc.shape[0]
    L = MLSTM_CHUNK
    nc = seq // L
    per = L // SUBLANES
    vcol = 3 * FOX_WIDTH // MLSTM_V_WIDTH
    return pl.pallas_call(
        _mlstm_body,
        grid=(batch, nc),
        in_specs=[
            pl.BlockSpec((L, 2 * MLSTM_QK_WIDTH), lambda b, c: (b * nc + c, 0)),
            pl.BlockSpec((SUBLANES, 2 * MLSTM_QK_WIDTH), lambda b, c: (jnp.maximum((b * nc + c) * per - 1, 0), 0)),
            pl.BlockSpec((L, MLSTM_V_WIDTH), lambda b, c: (b * nc + c, vcol)),
            pl.BlockSpec((L, MLSTM_V_WIDTH), lambda b, c: (b * nc + c, vcol + 1)),
            pl.BlockSpec((L, GATE_LANES), lambda b, c: (b * nc + c, 0)),
            pl.BlockSpec((GATE_ROWS, L), lambda b, c: (0, b * nc + c)),
            pl.BlockSpec((MLSTM_CONV, 2 * MLSTM_QK_WIDTH), lambda b, c: (0, 0)),
            pl.BlockSpec((1, 2 * MLSTM_QK_WIDTH), lambda b, c: (0, 0)),
            pl.BlockSpec((1, MLSTM_V_WIDTH), lambda b, c: (0, 0)),
        ],
        out_specs=pl.BlockSpec((L, MLSTM_V_WIDTH), lambda b, c: (b * nc + c, 0)),
        out_shape=jax.ShapeDtypeStruct((T, MLSTM_V_WIDTH), BF16),
        scratch_shapes=[
            pltpu.VMEM((SUBLANES + L, 2 * MLSTM_QK_WIDTH), F32),
            pltpu.VMEM((MLSTM_HEADS, MLSTM_QK_DIM, MLSTM_V_DIM), F32),
            pltpu.VMEM((MLSTM_HEADS, 1, MLSTM_QK_DIM), F32),
            pltpu.VMEM((MLSTM_HEADS, 1, LANES), F32),
        ],
        compiler_params=_cparams(("parallel", "arbitrary")),
        name="mlstm",
    )(zc, zc, z, z, gc, gr, conv_w, conv_b, nw)


def _outproj_body(att_ref, cell_ref, x_ref, wa_ref, wb_ref, n2_ref, x1_ref, h2t_ref):
    y = (jnp.dot(att_ref[...], wa_ref[...], preferred_element_type=F32)
         + jnp.dot(cell_ref[...], wb_ref[...], preferred_element_type=F32))
    x1 = x_ref[...] + y
    x1_ref[...] = x1
    ms = jnp.mean(x1 * x1, axis=-1, keepdims=True)
    h2 = (x1 * lax.rsqrt(ms + NORM_EPS)) * n2_ref[...]
    h2t_ref[...] = h2.T.astype(BF16)


def _outproj(att, cell, x2, wa, wb, n2, *, tm=512):
    T = x2.shape[0]
    return pl.pallas_call(
        _outproj_body,
        grid=(T // tm,),
        in_specs=[
            pl.BlockSpec((tm, FOX_WIDTH), lambda i: (i, 0)),
            pl.BlockSpec((tm, MLSTM_V_WIDTH), lambda i: (i, 0)),
            pl.BlockSpec((tm, D_MODEL), lambda i: (i, 0)),
            pl.BlockSpec((FOX_WIDTH, D_MODEL), lambda i: (0, 0)),
            pl.BlockSpec((MLSTM_V_WIDTH, D_MODEL), lambda i: (0, 0)),
            pl.BlockSpec((1, D_MODEL), lambda i: (0, 0)),
        ],
        out_specs=[
            pl.BlockSpec((tm, D_MODEL), lambda i: (i, 0)),
            pl.BlockSpec((D_MODEL, tm), lambda i: (0, i)),
        ],
        out_shape=[
            jax.ShapeDtypeStruct((T, D_MODEL), F32),
            jax.ShapeDtypeStruct((D_MODEL, T), BF16),
        ],
        compiler_params=_cparams(("parallel",)),
        name="outproj",
    )(att, cell, x2, wa, wb, n2)


def _extract16(s, n_rows, tb, store_row):
    iota = lax.broadcasted_iota(jnp.int32, (n_rows, tb), 0).astype(F32)

    def body(r, carry):
        sw, rank = carry
        mx = jnp.max(sw, axis=0, keepdims=True)
        first = jnp.min(jnp.where(sw == mx, iota, float(n_rows)), axis=0, keepdims=True)
        hit = iota == first
        store_row(r, mx)
        return jnp.where(hit, -jnp.inf, sw), jnp.where(hit, r.astype(F32), rank)

    _, rank = lax.fori_loop(0, PEER_TOPK, body, (s, jnp.full((n_rows, tb), float(PEER_TOPK), F32)))
    return rank


def _peer_sel_body(h2t_ref, wqt_ref, keys_ref, grp_ref, e1_ref, c1_ref, n0_ref, w0_ref,
                   qt_scr, vals_scr, cand_scr, pe_scr, *, tb):
    half = PEER_KEY_DIM // 2
    qt_scr[...] = jnp.dot(wqt_ref[...], h2t_ref[...], preferred_element_type=F32).astype(BF16)

    for h in range(PEER_HEADS):
        s0 = jnp.dot(keys_ref[0], qt_scr[(2 * h) * half:(2 * h + 1) * half, :], preferred_element_type=F32)
        s1 = jnp.dot(keys_ref[1], qt_scr[(2 * h + 1) * half:(2 * h + 2) * half, :], preferred_element_type=F32)

        def store0(r, mx):
            vals_scr[0, pl.ds(r, 1), :] = mx

        def store1(r, mx):
            vals_scr[1, pl.ds(r, 1), :] = mx

        rank0 = _extract16(s0, PEER_N_KEYS, tb, store0)
        rank1 = _extract16(s1, PEER_N_KEYS, tb, store1)
        a = vals_scr[0]
        b = vals_scr[1]
        ea = jnp.exp(a - a[0:1, :])
        eb = jnp.exp(b - b[0:1, :])
        off = 0
        for r in range(PEER_TOPK):
            ncol = _CAND_COLS[r]
            cand_scr[off:off + ncol, :] = a[r:r + 1, :] + b[0:ncol, :]
            pe_scr[off:off + ncol, :] = ea[r:r + 1, :] * eb[0:ncol, :]
            off += ncol
        cand_scr[_N_CAND:_N_CAND_PAD, :] = jnp.full((_N_CAND_PAD - _N_CAND, tb), -jnp.inf, F32)
        pe_scr[_N_CAND:_N_CAND_PAD, :] = jnp.zeros((_N_CAND_PAD - _N_CAND, tb), F32)

        crank = _extract16(cand_scr[...], _N_CAND_PAD, tb, lambda r, mx: None)
        sel = (crank < float(PEER_TOPK)).astype(F32)
        zsum = jnp.sum(sel * pe_scr[...], axis=0, keepdims=True)
        nr = jnp.dot(grp_ref[...], sel.astype(BF16), preferred_element_type=F32)
        n0 = jnp.zeros((PEER_N_KEYS, tb), F32)
        for r in range(PEER_TOPK):
            n0 = jnp.where(rank0 == float(r), nr[r:r + 1, :], n0)
        n0_ref[h] = n0
        w0_ref[h] = jnp.exp(s0 - a[0:1, :]) * (0.5 / zsum)
        e1_ref[h] = jnp.exp(s1 - b[0:1, :]).astype(BF16)
        c1_ref[h] = rank1.astype(BF16)


def _peer_sel(h2t, wqt, keys, grp, *, tb=256):
    T = h2t.shape[1]
    tab = jax.ShapeDtypeStruct((PEER_HEADS, PEER_N_KEYS, T), F32)
    tab16 = jax.ShapeDtypeStruct((PEER_HEADS, PEER_N_KEYS, T), BF16)
    tab_spec = pl.BlockSpec((PEER_HEADS, PEER_N_KEYS, tb), lambda i: (0, 0, i))
    return pl.pallas_call(
        functools.partial(_peer_sel_body, tb=tb),
        grid=(T // tb,),
        in_specs=[
            pl.BlockSpec((D_MODEL, tb), lambda i: (0, i)),
            pl.BlockSpec((PEER_HEADS * PEER_KEY_DIM, D_MODEL), lambda i: (0, 0)),
            pl.BlockSpec((2, PEER_N_KEYS, PEER_KEY_DIM // 2), lambda i: (0, 0, 0)),
            pl.BlockSpec((PEER_TOPK, _N_CAND_PAD), lambda i: (0, 0)),
        ],
        out_specs=[tab_spec, tab_spec, tab_spec, tab_spec],
        out_shape=[tab16, tab16, tab, tab],
        scratch_shapes=[
            pltpu.VMEM((PEER_HEADS * PEER_KEY_DIM, tb), BF16),
            pltpu.VMEM((2, PEER_TOPK, tb), F32),
            pltpu.VMEM((_N_CAND_PAD, tb), F32),
            pltpu.VMEM((_N_CAND_PAD, tb), F32),
        ],
        compiler_params=_cparams(("parallel",)),
        name="peer_sel",
    )(h2t, wqt, keys, grp)


def _peer_dense_body(u_ref, vt_ref, h2t_ref, e1_ref, c1_ref, n0_ref, w0_ref, yt_ref, st_a, st_b, *, ec, tb, nk):
    k = pl.program_id(1)
    slabs = ec // PEER_N_KEYS

    def pre_activations(st_w):
        st_w[...] = jnp.dot(u_ref[...], h2t_ref[...], preferred_element_type=F32)

    def activate_and_project(st_r):
        parts = []
        for ii in range(slabs):
            g = jnp.zeros((PEER_N_KEYS, tb), BF16)
            for h in range(PEER_HEADS):
                n_row = n0_ref[h, ii:ii + 1, :].astype(BF16)
                w_row = w0_ref[h, ii:ii + 1, :].astype(BF16)
                g = g + jnp.where(c1_ref[h] < n_row, e1_ref[h] * w_row, jnp.zeros((), BF16))
            x = st_r[ii * PEER_N_KEYS:(ii + 1) * PEER_N_KEYS, :]
            act = x * (1.0 + lax.erf(x * math.sqrt(0.5)))
            parts.append(act.astype(BF16) * g)
        at = jnp.concatenate(parts, axis=0)
        yt_ref[...] += jnp.dot(vt_ref[...], at, preferred_element_type=F32)

    @pl.when(k == 0)
    def _():
        yt_ref[...] = jnp.zeros_like(yt_ref)
        pre_activations(st_a)

    @pl.when((k > 0) & (k < nk) & (k % 2 == 1))
    def _():
        pre_activations(st_b)
        activate_and_project(st_a)

    @pl.when((k > 0) & (k < nk) & (k % 2 == 0))
    def _():
        pre_activations(st_a)
        activate_and_project(st_b)

    @pl.when(k == nk)
    def _():
        activate_and_project(st_b if nk % 2 == 0 else st_a)


def _peer_dense(u, vt, h2t, e1, c1, n0, w0, *, tb=512, ec=1024):
    T = h2t.shape[1]
    nk = PEER_N_EXPERTS // ec
    slabs = ec // PEER_N_KEYS
    assert slabs == SUBLANES, "one f32 sublane tile of per-slab gate rows per expert chunk"
    tab_spec = pl.BlockSpec((PEER_HEADS, PEER_N_KEYS, tb), lambda i, k: (0, 0, i))
    row_spec = pl.BlockSpec((PEER_HEADS, slabs, tb), lambda i, k: (0, jnp.maximum(k - 1, 0), i))
    return pl.pallas_call(
        functools.partial(_peer_dense_body, ec=ec, tb=tb, nk=nk),
        grid=(T // tb, nk + 1),
        in_specs=[
            pl.BlockSpec((ec, D_MODEL), lambda i, k: (jnp.minimum(k, nk - 1), 0)),
            pl.BlockSpec((D_MODEL, ec), lambda i, k: (0, jnp.maximum(k - 1, 0))),
            pl.BlockSpec((D_MODEL, tb), lambda i, k: (0, i)),
            tab_spec, tab_spec, row_spec, row_spec,
        ],
        out_specs=pl.BlockSpec((D_MODEL, tb), lambda i, k: (0, i)),
        out_shape=jax.ShapeDtypeStruct((D_MODEL, T), F32),
        scratch_shapes=[pltpu.VMEM((ec, tb), F32), pltpu.VMEM((ec, tb), F32)],
        compiler_params=_cparams(("parallel", "arbitrary")),
        name="peer_dense",
    )(u, vt, h2t, e1, c1, n0, w0)


def _final_body(x1_ref, yt_ref, w_ref, o_ref):
    x2 = x1_ref[...] + yt_ref[...].T
    ms = jnp.mean(x2 * x2, axis=-1, keepdims=True)
    o_ref[...] = (x2 * lax.rsqrt(ms + NORM_EPS)) * w_ref[...]


def _final(x1, yt, w, *, tm=512):
    T = x1.shape[0]
    return pl.pallas_call(
        _final_body,
        grid=(T // tm,),
        in_specs=[
            pl.BlockSpec((tm, D_MODEL), lambda i: (i, 0)),
            pl.BlockSpec((D_MODEL, tm), lambda i: (0, i)),
            pl.BlockSpec((1, D_MODEL), lambda i: (0, 0)),
        ],
        out_specs=pl.BlockSpec((tm, D_MODEL), lambda i: (i, 0)),
        out_shape=jax.ShapeDtypeStruct((T, D_MODEL), F32),
        compiler_params=_cparams(("parallel",)),
        name="final",
    )(x1, yt, w)


def _group_matrix():
    g = np.zeros((PEER_TOPK, _N_CAND_PAD), np.float32)
    off = 0
    for r, ncol in enumerate(_CAND_COLS):
        g[r, off:off + ncol] = 1.0
        off += ncol
    return g


def _layer(x2, norm1_w, w_in, fox_f_bias, conv_w, conv_b, i_bias, f_bias, fox_nw, mlstm_nw, w_out, norm2_w,
           w_q, keys, u, v, *, batch, seq):
    splits = np.cumsum((FOX_WIDTH, FOX_WIDTH, FOX_WIDTH, FOX_HEADS, MLSTM_QK_WIDTH, MLSTM_QK_WIDTH,
                        MLSTM_V_WIDTH, MLSTM_HEADS, MLSTM_HEADS, MLSTM_V_WIDTH))[:-1]
    fq, fk, fv, ff, mq, mk, mv, mi, mf, mo = jnp.split(w_in, [int(p) for p in splits], axis=1)
    w_main = jnp.concatenate([fq, fk, fv, mv, mo, mq, mk], axis=1).astype(BF16)
    gate_pad = GATE_LANES - FOX_HEADS - 2 * MLSTM_HEADS
    w_gate = jnp.pad(jnp.concatenate([ff, mi, mf], axis=1), ((0, 0), (0, gate_pad))).astype(BF16)
    gate_bias = jnp.pad(jnp.concatenate([fox_f_bias, i_bias, f_bias]), (0, gate_pad)).reshape(1, GATE_LANES)
    tri = jnp.asarray(np.tril(np.ones((LANES, LANES), np.float32)), BF16)

    z, zc, g = _inproj(x2, norm1_w.reshape(1, D_MODEL), w_main, w_gate)
    pq, pk = _placement_matrices()
    gc, gr, qx, kx = _gates(g, gate_bias, tri, jnp.asarray(pq, BF16), jnp.asarray(pk, BF16), batch=batch, seq=seq)
    att = _fox(z, qx, kx, fox_nw.reshape(FOX_HEADS, FOX_HEAD_DIM), batch=batch, seq=seq)
    cell = _mlstm(zc, z, gc, gr, conv_w, conv_b.reshape(1, -1), mlstm_nw.reshape(1, -1), batch=batch, seq=seq)
    x1, h2t = _outproj(att, cell, x2, w_out[:FOX_WIDTH].astype(BF16), w_out[FOX_WIDTH:].astype(BF16),
                       norm2_w.reshape(1, D_MODEL))
    e1, c1, n0, w0 = _peer_sel(h2t, w_q.T.astype(BF16), keys.astype(BF16), jnp.asarray(_group_matrix(), BF16))
    yt = _peer_dense(u.astype(BF16), v.T.astype(BF16), h2t, e1, c1, n0, w0)
    return x1, yt


def kernel(x, norm1_w, w_in, fox_f_bias, mlstm_conv_w, mlstm_conv_b, mlstm_i_bias, mlstm_f_bias, fox_out_norm_w,
           mlstm_out_norm_w, w_out, norm2_w, peer_w_q, peer_keys, peer_u, peer_v, final_norm_w):
    batch, seq, _ = x.shape
    assert w_in.shape[0] == 1, "single-layer block: the final norm is fused with the last residual add"
    x2 = x.reshape(batch * seq, D_MODEL)
    x1, yt = _layer(x2, norm1_w[0], w_in[0], fox_f_bias[0], mlstm_conv_w[0], mlstm_conv_b[0], mlstm_i_bias[0],
                    mlstm_f_bias[0], fox_out_norm_w[0], mlstm_out_norm_w[0], w_out[0], norm2_w[0],
                    peer_w_q[0], peer_keys[0], peer_u[0], peer_v[0], batch=batch, seq=seq)
    out = _final(x1, yt, final_norm_w.reshape(1, D_MODEL))
    return out.reshape(batch, seq, D_MODEL)
```

```python
import functools
import math

import numpy as np
import jax
import jax.numpy as jnp
from jax import lax
from jax.experimental import pallas as pl
from jax.experimental.pallas import tpu as pltpu

F32 = jnp.float32
BF16 = jnp.bfloat16

D_MODEL = 2048
FOX_HEADS = 8
FOX_HEAD_DIM = 128
FOX_WIDTH = FOX_HEADS * FOX_HEAD_DIM
MLSTM_HEADS = 4
MLSTM_QK_DIM = 128
MLSTM_V_DIM = 256
MLSTM_QK_WIDTH = MLSTM_HEADS * MLSTM_QK_DIM
MLSTM_V_WIDTH = MLSTM_HEADS * MLSTM_V_DIM
MLSTM_CONV = 4
MLSTM_CHUNK = 128
PEER_HEADS = 8
PEER_KEY_DIM = 256
PEER_N_KEYS = 128
PEER_TOPK = 16
PEER_N_EXPERTS = PEER_N_KEYS * PEER_N_KEYS
NORM_EPS = 1e-6
LOG2E = math.log2(math.e)
NORM_MARGIN = 1.0 + 2.0 ** -7
F32_ZERO_LOG2 = 150.0
FAST_GAP_LOG2 = 90.0
BOUND_SLACK_LOG2 = 4.0
FOX_KEY_BLOCK = 512

LANES = 128
SUBLANES = 8
GATE_LANES = LANES
GATE_ROWS = 16
VMEM_LIMIT = 56 * 1024 * 1024

_CAND_COLS = tuple(PEER_TOPK // (r + 1) for r in range(PEER_TOPK))
_N_CAND = sum(_CAND_COLS)
_N_CAND_PAD = -(-_N_CAND // SUBLANES) * SUBLANES


def _cparams(sem):
    return pltpu.CompilerParams(dimension_semantics=sem, vmem_limit_bytes=VMEM_LIMIT)


def _inproj_body(x_ref, nw_ref, w_ref, wg_ref, z_ref, zc_ref, g_ref, h_scr, *, n_main, q_blocks, q_scale):
    j = pl.program_id(1)

    @pl.when(j == 0)
    def _():
        x = x_ref[...]
        ms = jnp.mean(x * x, axis=-1, keepdims=True)
        hb = ((x * lax.rsqrt(ms + NORM_EPS)) * nw_ref[...]).astype(BF16)
        h_scr[...] = hb
        g_ref[...] = jnp.dot(hb, wg_ref[...], preferred_element_type=F32)

    z = jnp.dot(h_scr[...], w_ref[...], preferred_element_type=F32)

    @pl.when(j < n_main)
    def _():
        scale = jnp.where(j < q_blocks, q_scale, 1.0).astype(F32)
        z_ref[...] = (z * scale).astype(BF16)

    @pl.when(j >= n_main)
    def _():
        zc_ref[...] = z


def _inproj(x2, norm_w, w_main, w_gate, *, tm=512, tn=512):
    T = x2.shape[0]
    n_cols = w_main.shape[1]
    n_conv = 2 * MLSTM_QK_WIDTH
    n_main = (n_cols - n_conv) // tn
    n_blocks = n_cols // tn
    body = functools.partial(_inproj_body, n_main=n_main, q_blocks=FOX_WIDTH // tn,
                             q_scale=FOX_HEAD_DIM ** -0.5 * LOG2E)
    return pl.pallas_call(
        body,
        grid=(T // tm, n_blocks),
        in_specs=[
            pl.BlockSpec((tm, D_MODEL), lambda i, j: (i, 0)),
            pl.BlockSpec((1, D_MODEL), lambda i, j: (0, 0)),
            pl.BlockSpec((D_MODEL, tn), lambda i, j: (0, j)),
            pl.BlockSpec((D_MODEL, GATE_LANES), lambda i, j: (0, 0)),
        ],
        out_specs=[
            pl.BlockSpec((tm, tn), lambda i, j: (i, jnp.minimum(j, n_main - 1))),
            pl.BlockSpec((tm, tn), lambda i, j: (i, jnp.maximum(j - n_main, 0))),
            pl.BlockSpec((tm, GATE_LANES), lambda i, j: (i, 0)),
        ],
        out_shape=[
            jax.ShapeDtypeStruct((T, n_cols - n_conv), BF16),
            jax.ShapeDtypeStruct((T, n_conv), F32),
            jax.ShapeDtypeStruct((T, GATE_LANES), F32),
        ],
        scratch_shapes=[pltpu.VMEM((tm, D_MODEL), BF16)],
        compiler_params=_cparams(("parallel", "arbitrary")),
        name="inproj",
    )(x2, norm_w, w_main, w_gate)


def _split3(v):
    hi = v.astype(BF16)
    r1 = v - hi.astype(F32)
    mid = r1.astype(BF16)
    lo = (r1 - mid.astype(F32)).astype(BF16)
    return hi, mid, lo


def _fox_norms_body(q_ref, k_ref, grp_ref, qn_ref, kn_ref):
    for src, dst in ((q_ref, qn_ref), (k_ref, kn_ref)):
        x = src[...].astype(F32)
        ss = jnp.dot((x * x).astype(BF16), grp_ref[...], preferred_element_type=F32)
        dst[...] = jnp.sqrt(ss) * NORM_MARGIN


def _head_group_matrix():
    g = np.zeros((FOX_WIDTH, GATE_LANES), np.float32)
    for h in range(FOX_HEADS):
        g[h * FOX_HEAD_DIM:(h + 1) * FOX_HEAD_DIM, h] = 1.0
    return g


def _fox_norms(z, grp, *, rows=1024):
    T = z.shape[0]
    out = jax.ShapeDtypeStruct((T, GATE_LANES), F32)
    return pl.pallas_call(
        _fox_norms_body,
        grid=(T // rows,),
        in_specs=[
            pl.BlockSpec((rows, FOX_WIDTH), lambda i: (i, 0)),
            pl.BlockSpec((rows, FOX_WIDTH), lambda i: (i, 1)),
            pl.BlockSpec((FOX_WIDTH, GATE_LANES), lambda i: (0, 0)),
        ],
        out_specs=[pl.BlockSpec((rows, GATE_LANES), lambda i: (i, 0))] * 2,
        out_shape=[out, out],
        compiler_params=_cparams(("parallel",)),
        name="fox_norms",
    )(z, z, grp)


def _gates_body(g_ref, bias_ref, tri_ref, pq_ref, pk_ref, qn_ref, kmax_ref, gc_ref, gr_ref, qx_ref, kx_ref,
                carry_scr, *, rows):
    c = pl.program_id(1)

    @pl.when(c == 0)
    def _():
        carry_scr[...] = jnp.zeros_like(carry_scr)

    lane = lax.broadcasted_iota(jnp.int32, (LANES, GATE_LANES), 1)
    is_glob = lane < FOX_HEADS
    is_ls = is_glob | ((lane >= FOX_HEADS + MLSTM_HEADS) & (lane < FOX_HEADS + 2 * MLSTM_HEADS))
    tri = tri_ref[...]
    for s in range(rows // LANES):
        sl = slice(s * LANES, (s + 1) * LANES)
        g = g_ref[sl, :] + bias_ref[...]
        ls = jnp.minimum(g, 0.0) - jnp.log1p(jnp.exp(-jnp.abs(g)))
        v = jnp.where(is_ls, ls, 0.0)
        hi, mid, lo = _split3(v)
        cs = (jnp.dot(tri, hi, preferred_element_type=F32)
              + jnp.dot(tri, mid, preferred_element_type=F32)
              + jnp.dot(tri, lo, preferred_element_type=F32))
        glob = cs + carry_scr[...]
        carry_scr[...] = glob[LANES - 1:LANES, :]
        out = jnp.where(is_glob, glob, jnp.where(is_ls, cs, g))
        gc_ref[sl, :] = out
        gr_ref[:, sl] = out.T[0:GATE_ROWS, :]
        fhi, fmid, flo = _split3(glob * LOG2E)
        mhi, mmid, mlo = _split3(qn_ref[sl, :] * kmax_ref[0:1, :] + 1.0)
        pieces = jnp.concatenate([fhi, fmid, flo, jnp.ones((LANES, GATE_LANES), BF16), mhi, mmid, mlo], axis=1)
        qx_ref[sl, :] = jnp.dot(pieces, pq_ref[...], preferred_element_type=F32).astype(BF16)
        kx_ref[sl, :] = jnp.dot(pieces, pk_ref[...], preferred_element_type=F32).astype(BF16)


_N_PIECES = 3
_PIECE_GROUPS = 2 * _N_PIECES + 1
_STAB_SLOT = 2 * _N_PIECES


def _placement_matrices():
    n = _N_PIECES
    pq = np.zeros((_PIECE_GROUPS * GATE_LANES, FOX_WIDTH), np.float32)
    pk = np.zeros((_PIECE_GROUPS * GATE_LANES, FOX_WIDTH), np.float32)
    for h in range(FOX_HEADS):
        for p in range(n):
            pq[p * GATE_LANES + h, h * FOX_HEAD_DIM + p] = 1.0
            pk[n * GATE_LANES + h, h * FOX_HEAD_DIM + p] = 1.0
            pq[n * GATE_LANES + h, h * FOX_HEAD_DIM + n + p] = 1.0
            pk[p * GATE_LANES + h, h * FOX_HEAD_DIM + n + p] = -1.0
            pq[(n + 1 + p) * GATE_LANES + h, h * FOX_HEAD_DIM + _STAB_SLOT + p] = -1.0
            pk[n * GATE_LANES + h, h * FOX_HEAD_DIM + _STAB_SLOT + p] = 1.0
    return pq, pk


def _gates(g, bias, tri, pq, pk, qn, kmax, *, batch, seq, rows=1024):
    T = g.shape[0]
    nblk = seq // rows
    return pl.pallas_call(
        functools.partial(_gates_body, rows=rows),
        grid=(batch, nblk),
        in_specs=[
            pl.BlockSpec((rows, GATE_LANES), lambda b, c: (b * nblk + c, 0)),
            pl.BlockSpec((1, GATE_LANES), lambda b, c: (0, 0)),
            pl.BlockSpec((LANES, LANES), lambda b, c: (0, 0)),
            pl.BlockSpec((_PIECE_GROUPS * GATE_LANES, FOX_WIDTH), lambda b, c: (0, 0)),
            pl.BlockSpec((_PIECE_GROUPS * GATE_LANES, FOX_WIDTH), lambda b, c: (0, 0)),
            pl.BlockSpec((rows, GATE_LANES), lambda b, c: (b * nblk + c, 0)),
            pl.BlockSpec((SUBLANES, GATE_LANES), lambda b, c: (b, 0)),
        ],
        out_specs=[
            pl.BlockSpec((rows, GATE_LANES), lambda b, c: (b * nblk + c, 0)),
            pl.BlockSpec((GATE_ROWS, rows), lambda b, c: (0, b * nblk + c)),
            pl.BlockSpec((rows, FOX_WIDTH), lambda b, c: (b * nblk + c, 0)),
            pl.BlockSpec((rows, FOX_WIDTH), lambda b, c: (b * nblk + c, 0)),
        ],
        out_shape=[
            jax.ShapeDtypeStruct((T, GATE_LANES), F32),
            jax.ShapeDtypeStruct((GATE_ROWS, T), F32),
            jax.ShapeDtypeStruct((T, FOX_WIDTH), BF16),
            jax.ShapeDtypeStruct((T, FOX_WIDTH), BF16),
        ],
        scratch_shapes=[pltpu.VMEM((1, GATE_LANES), F32)],
        compiler_params=_cparams(("parallel", "arbitrary")),
        name="gates",
    )(g, bias, tri, pq, pk, qn, kmax)


def _fox_body(lo_ref, fast_ref, q_ref, qx_ref, k_ref, kx_ref, v_ref, nw_ref, o_ref, *, tk, nq):
    b = pl.program_id(0)
    h = pl.program_id(1)
    qi = pl.program_id(2)
    tile = (b * FOX_HEADS + h) * nq + qi
    lo = lo_ref[tile]
    row = lax.broadcasted_iota(jnp.int32, (tk, tk), 0)
    col = lax.broadcasted_iota(jnp.int32, (tk, tk), 1)

    def load_kv(ki):
        start = pl.multiple_of(ki * tk, tk)
        k = jnp.concatenate([k_ref[pl.ds(start, tk), :], kx_ref[pl.ds(start, tk), :]], axis=1)
        return k, v_ref[pl.ds(start, tk), :]

    def logits(q, k, masked):
        s = lax.dot_general(q, k, (((1,), (1,)), ((), ())), preferred_element_type=F32)
        return jnp.where(row >= col, s, -jnp.inf) if masked else s

    def finish(r, num, den):
        out = num / den
        ms = jnp.mean(out * out, axis=-1, keepdims=True)
        o_ref[r * tk:(r + 1) * tk, :] = ((out * lax.rsqrt(ms + NORM_EPS)) * nw_ref[pl.ds(h, 1), :]).astype(BF16)

    @pl.when(fast_ref[tile] == 1)
    def _():
        q_halves = [jnp.concatenate([q_ref[r * tk:(r + 1) * tk, :], qx_ref[r * tk:(r + 1) * tk, :]], axis=1)
                    for r in range(2)]
        ones_col = (lax.broadcasted_iota(jnp.int32, (tk, LANES), 1) == 0).astype(BF16)

        def weights(r, k, masked):
            return jnp.exp2(logits(q_halves[r], k, masked)).astype(BF16)

        def values(ki):
            return jnp.concatenate([load_kv(ki)[1], ones_col], axis=1)

        def add(acc, p, va):
            return acc + jnp.dot(p, va, preferred_element_type=F32)

        def pair(kp, accs):
            blocks = [2 * kp, 2 * kp + 1]
            ps = [[weights(r, load_kv(ki)[0], False) for r in range(2)] for ki in blocks]
            for j, ki in enumerate(blocks):
                va = values(ki)
                accs = tuple(add(accs[r], ps[j][r], va) for r in range(2))
            return accs

        zero = jnp.zeros((tk, 2 * FOX_HEAD_DIM), F32)
        aa, ab = lax.fori_loop(lo // 2, qi, pair, (zero, zero))
        k = load_kv(2 * qi)[0]
        pa, pb = weights(0, k, True), weights(1, k, False)
        pb2 = weights(1, load_kv(2 * qi + 1)[0], True)
        va = values(2 * qi)
        aa, ab = add(aa, pa, va), add(ab, pb, va)
        ab = add(ab, pb2, values(2 * qi + 1))
        for r, acc in enumerate((aa, ab)):
            finish(r, acc[:, 0:FOX_HEAD_DIM], acc[:, FOX_HEAD_DIM:FOX_HEAD_DIM + 1])

    @pl.when(fast_ref[tile] == 0)
    def _():
        _fox_online(q_ref, qx_ref, load_kv, logits, finish, lo, qi, tk)


def _fox_online(q_ref, qx_ref, load_kv, logits, finish, lo, qi, tk):
    lane = lax.broadcasted_iota(jnp.int32, (tk, LANES), 1)
    no_stab = (lane < _STAB_SLOT) | (lane >= _STAB_SLOT + _N_PIECES)
    q_halves = [jnp.concatenate([q_ref[r * tk:(r + 1) * tk, :],
                                 jnp.where(no_stab, qx_ref[r * tk:(r + 1) * tk, :], jnp.zeros((), BF16))], axis=1)
                for r in range(2)]

    def weights(s, carry):
        m, l, acc = carry
        cols = [s[:, c * LANES:(c + 1) * LANES] for c in range(tk // LANES)]
        m_new = jnp.maximum(m, jnp.max(functools.reduce(jnp.maximum, cols), axis=-1, keepdims=True))
        alpha = jnp.exp2(m - m_new)
        ps = [jnp.exp2(c - m_new) for c in cols]
        l = alpha * l + functools.reduce(jnp.add, ps)
        return m_new, l, alpha * acc, jnp.concatenate(ps, axis=1).astype(BF16)

    def update(q, k, v, carry, masked):
        m, l, acc, p = weights(logits(q, k, masked), carry)
        return m, l, acc + jnp.dot(p, v, preferred_element_type=F32)

    def both(ki, carries):
        k, v = load_kv(ki)
        s = [logits(q_halves[r], k, False) for r in range(2)]
        w = [weights(s[r], carries[r]) for r in range(2)]
        return tuple((m, l, acc + jnp.dot(p, v, preferred_element_type=F32)) for m, l, acc, p in w)

    init = (jnp.full((tk, LANES), -jnp.inf, F32), jnp.zeros((tk, LANES), F32), jnp.zeros((tk, FOX_HEAD_DIM), F32))
    ca, cb = lax.fori_loop(lo, 2 * qi, both, (init, init))
    k, v = load_kv(2 * qi)
    ca = update(q_halves[0], k, v, ca, True)
    cb = update(q_halves[1], k, v, cb, False)
    k, v = load_kv(2 * qi + 1)
    cb = update(q_halves[1], k, v, cb, True)
    for r, (_, l, acc) in enumerate((ca, cb)):
        finish(r, acc, jnp.sum(l, axis=-1, keepdims=True))


def _fox_plan(qn, kmax, gc, *, batch, seq, tk):
    tq = 2 * tk
    nq = seq // tq
    nk = seq // tk
    heads = slice(0, FOX_HEADS)
    qmax = jnp.max(qn.reshape(batch, nq, tq, GATE_LANES), axis=2)[..., heads]
    gap = 2.0 * qmax * kmax[:, None, heads] + BOUND_SLACK_LOG2
    f2 = (gc[:, heads] * LOG2E).reshape(batch, nk, tk, FOX_HEADS)
    f_first = f2[:, ::2, 0, :]
    f_last = f2[:, :, tk - 1, :]
    bound = gap[:, :, None, :] + f_first[:, :, None, :] - f_last[:, None, :, :]
    below_diag = jnp.arange(nk)[None, :, None] < 2 * jnp.arange(nq)[:, None, None]
    skip = (bound < -F32_ZERO_LOG2) & below_diag[None]
    lo = jnp.sum(jnp.cumprod(skip.astype(jnp.int32), axis=2), axis=2)
    fast = (gap <= FAST_GAP_LOG2).astype(jnp.int32)
    flat = lambda a: jnp.transpose(a, (0, 2, 1)).reshape(-1)
    return flat(lo), flat(fast)


def _fox(z, qx, kx, nw, lo, fast, *, batch, seq, tk=512):
    T = z.shape[0]
    tq = 2 * tk
    nq = seq // tq
    kcol = FOX_WIDTH // FOX_HEAD_DIM
    grid_spec = pltpu.PrefetchScalarGridSpec(
        num_scalar_prefetch=2,
        grid=(batch, FOX_HEADS, nq),
        in_specs=[
            pl.BlockSpec((tq, FOX_HEAD_DIM), lambda b, h, i, lo, fast: (b * nq + i, h)),
            pl.BlockSpec((tq, FOX_HEAD_DIM), lambda b, h, i, lo, fast: (b * nq + i, h)),
            pl.BlockSpec((seq, FOX_HEAD_DIM), lambda b, h, i, lo, fast: (b, kcol + h)),
            pl.BlockSpec((seq, FOX_HEAD_DIM), lambda b, h, i, lo, fast: (b, h)),
            pl.BlockSpec((seq, FOX_HEAD_DIM), lambda b, h, i, lo, fast: (b, 2 * kcol + h)),
            pl.BlockSpec((FOX_HEADS, FOX_HEAD_DIM), lambda b, h, i, lo, fast: (0, 0)),
        ],
        out_specs=pl.BlockSpec((tq, FOX_HEAD_DIM), lambda b, h, i, lo, fast: (b * nq + i, h)),
    )
    return pl.pallas_call(
        functools.partial(_fox_body, tk=tk, nq=nq),
        grid_spec=grid_spec,
        out_shape=jax.ShapeDtypeStruct((T, FOX_WIDTH), BF16),
        compiler_params=_cparams(("parallel", "parallel", "arbitrary")),
        name="fox",
    )(lo, fast, z, qx, z, kx, z, nw)


def _mlstm_body(zc_ref, zprev_ref, v_ref, o_ref, gc_ref, gr_ref, cw_ref, cb_ref, nw_ref, out_ref,
                full_scr, c_scr, n_scr, m_scr):
    c = pl.program_id(1)
    L = MLSTM_CHUNK
    dk = MLSTM_QK_DIM
    dv = MLSTM_V_DIM

    @pl.when(c == 0)
    def _():
        c_scr[...] = jnp.zeros_like(c_scr)
        n_scr[...] = jnp.zeros_like(n_scr)
        m_scr[...] = jnp.zeros_like(m_scr)

    full_scr[0:SUBLANES, :] = jnp.where(c == 0, 0.0, zprev_ref[...])
    full_scr[SUBLANES:SUBLANES + L, :] = zc_ref[...]
    y = cb_ref[...]
    for j in range(MLSTM_CONV):
        y = y + cw_ref[j:j + 1, :] * full_scr[pl.ds(SUBLANES - (MLSTM_CONV - 1) + j, L), :]
    qk_all = y * jax.nn.sigmoid(y)

    gcb = gc_ref[...]
    grb = gr_ref[...]
    row = lax.broadcasted_iota(jnp.int32, (L, L), 0)
    col = lax.broadcasted_iota(jnp.int32, (L, L), 1)
    causal = row >= col
    for hh in range(MLSTM_HEADS):
        qh = qk_all[:, hh * dk:(hh + 1) * dk]
        kh = qk_all[:, MLSTM_QK_WIDTH + hh * dk:MLSTM_QK_WIDTH + (hh + 1) * dk] * (dk ** -0.5)
        vh = v_ref[:, hh * dv:(hh + 1) * dv]
        oh = o_ref[:, hh * dv:(hh + 1) * dv].astype(F32)
        li = FOX_HEADS + hh
        lf = FOX_HEADS + MLSTM_HEADS + hh
        i_col = gcb[:, li:li + 1]
        b_col = gcb[:, lf:lf + 1]
        i_row = grb[li:li + 1, :]
        b_row = grb[lf:lf + 1, :]
        b_last = b_row[:, L - 1:L]

        c_prev = c_scr[hh]
        n_prev = n_scr[hh]
        m_prev = m_scr[hh][:, 0:1]

        m_loc = jnp.max(b_last - b_row + i_row, axis=-1, keepdims=True)
        a_end = jnp.exp(b_last - b_col + i_col - m_loc)
        kw = kh * a_end
        c_loc = lax.dot_general(kw.astype(BF16), vh, (((0,), (0,)), ((), ())), preferred_element_type=F32)
        n_loc = jnp.sum(kw, axis=0, keepdims=True)

        g_col = b_col + m_prev
        dmat = jnp.where(causal, b_col - b_row + i_row, -jnp.inf)
        m_t = jnp.maximum(g_col, jnp.max(dmat, axis=-1, keepdims=True))
        qb = qh.astype(BF16)
        qk = lax.dot_general(qb, kh.astype(BF16), (((1,), (1,)), ((), ())), preferred_element_type=F32)
        sm = qk * jnp.exp(dmat - m_t)
        inter = jnp.exp(g_col - m_t)
        num = (jnp.dot(sm.astype(BF16), vh, preferred_element_type=F32)
               + inter * jnp.dot(qb, c_prev.astype(BF16), preferred_element_type=F32))
        den = jnp.sum(sm, axis=-1, keepdims=True) + inter * jnp.sum(qh * n_prev, axis=-1, keepdims=True)
        cell = num / jnp.maximum(jnp.abs(den), jnp.exp(-m_t))
        gated = jax.nn.sigmoid(oh) * cell
        ms = jnp.mean(gated * gated, axis=-1, keepdims=True)
        out_ref[:, hh * dv:(hh + 1) * dv] = (
            (gated * lax.rsqrt(ms + NORM_EPS)) * nw_ref[:, hh * dv:(hh + 1) * dv]).astype(BF16)

        m_new = jnp.maximum(b_last + m_prev, m_loc)
        a_prev = jnp.exp(b_last + m_prev - m_new)
        a_loc = jnp.exp(m_loc - m_new)
        c_scr[hh] = a_prev * c_prev + a_loc * c_loc
        n_scr[hh] = a_prev * n_prev + a_loc * n_loc
        m_scr[hh] = jnp.broadcast_to(m_new, (1, LANES))


def _mlstm(zc, z, gc, gr, conv_w, conv_b, nw, *, batch, seq):
    T = zc.shape[0]
    L = MLSTM_CHUNK
    nc = seq // L
    per = L // SUBLANES
    vcol = 3 * FOX_WIDTH // MLSTM_V_WIDTH
    return pl.pallas_call(
        _mlstm_body,
        grid=(batch, nc),
        in_specs=[
            pl.BlockSpec((L, 2 * MLSTM_QK_WIDTH), lambda b, c: (b * nc + c, 0)),
            pl.BlockSpec((SUBLANES, 2 * MLSTM_QK_WIDTH), lambda b, c: (jnp.maximum((b * nc + c) * per - 1, 0), 0)),
            pl.BlockSpec((L, MLSTM_V_WIDTH), lambda b, c: (b * nc + c, vcol)),
            pl.BlockSpec((L, MLSTM_V_WIDTH), lambda b, c: (b * nc + c, vcol + 1)),
            pl.BlockSpec((L, GATE_LANES), lambda b, c: (b * nc + c, 0)),
            pl.BlockSpec((GATE_ROWS, L), lambda b, c: (0, b * nc + c)),
            pl.BlockSpec((MLSTM_CONV, 2 * MLSTM_QK_WIDTH), lambda b, c: (0, 0)),
            pl.BlockSpec((1, 2 * MLSTM_QK_WIDTH), lambda b, c: (0, 0)),
            pl.BlockSpec((1, MLSTM_V_WIDTH), lambda b, c: (0, 0)),
        ],
        out_specs=pl.BlockSpec((L, MLSTM_V_WIDTH), lambda b, c: (b * nc + c, 0)),
        out_shape=jax.ShapeDtypeStruct((T, MLSTM_V_WIDTH), BF16),
        scratch_shapes=[
            pltpu.VMEM((SUBLANES + L, 2 * MLSTM_QK_WIDTH), F32),
            pltpu.VMEM((MLSTM_HEADS, MLSTM_QK_DIM, MLSTM_V_DIM), F32),
            pltpu.VMEM((MLSTM_HEADS, 1, MLSTM_QK_DIM), F32),
            pltpu.VMEM((MLSTM_HEADS, 1, LANES), F32),
        ],
        compiler_params=_cparams(("parallel", "arbitrary")),
        name="mlstm",
    )(zc, zc, z, z, gc, gr, conv_w, conv_b, nw)


def _outproj_body(att_ref, cell_ref, x_ref, wa_ref, wb_ref, n2_ref, x1_ref, h2t_ref):
    y = (jnp.dot(att_ref[...], wa_ref[...], preferred_element_type=F32)
         + jnp.dot(cell_ref[...], wb_ref[...], preferred_element_type=F32))
    x1 = x_ref[...] + y
    x1_ref[...] = x1
    ms = jnp.mean(x1 * x1, axis=-1, keepdims=True)
    h2 = (x1 * lax.rsqrt(ms + NORM_EPS)) * n2_ref[...]
    h2t_ref[...] = h2.T.astype(BF16)


def _outproj(att, cell, x2, wa, wb, n2, *, tm=512):
    T = x2.shape[0]
    return pl.pallas_call(
        _outproj_body,
        grid=(T // tm,),
        in_specs=[
            pl.BlockSpec((tm, FOX_WIDTH), lambda i: (i, 0)),
            pl.BlockSpec((tm, MLSTM_V_WIDTH), lambda i: (i, 0)),
            pl.BlockSpec((tm, D_MODEL), lambda i: (i, 0)),
            pl.BlockSpec((FOX_WIDTH, D_MODEL), lambda i: (0, 0)),
            pl.BlockSpec((MLSTM_V_WIDTH, D_MODEL), lambda i: (0, 0)),
            pl.BlockSpec((1, D_MODEL), lambda i: (0, 0)),
        ],
        out_specs=[
            pl.BlockSpec((tm, D_MODEL), lambda i: (i, 0)),
            pl.BlockSpec((D_MODEL, tm), lambda i: (0, i)),
        ],
        out_shape=[
            jax.ShapeDtypeStruct((T, D_MODEL), F32),
            jax.ShapeDtypeStruct((D_MODEL, T), BF16),
        ],
        compiler_params=_cparams(("parallel",)),
        name="outproj",
    )(att, cell, x2, wa, wb, n2)


def _extract16(s, n_rows, tb, store_row):
    iota = lax.broadcasted_iota(jnp.int32, (n_rows, tb), 0).astype(F32)

    def body(r, carry):
        sw, rank = carry
        mx = jnp.max(sw, axis=0, keepdims=True)
        first = jnp.min(jnp.where(sw == mx, iota, float(n_rows)), axis=0, keepdims=True)
        hit = iota == first
        store_row(r, mx)
        return jnp.where(hit, -jnp.inf, sw), jnp.where(hit, lax.convert_element_type(r, F32), rank)

    _, rank = lax.fori_loop(0, PEER_TOPK, body, (s, jnp.full((n_rows, tb), float(PEER_TOPK), F32)))
    return rank


def _peer_sel_body(h2t_ref, wqt_ref, keys_ref, grp_ref, e1_ref, c1_ref, n0_ref, w0_ref,
                   qt_scr, vals_scr, cand_scr, pe_scr, *, tb):
    half = PEER_KEY_DIM // 2
    qt_scr[...] = jnp.dot(wqt_ref[...], h2t_ref[...], preferred_element_type=F32).astype(BF16)

    for h in range(PEER_HEADS):
        s0 = jnp.dot(keys_ref[0], qt_scr[(2 * h) * half:(2 * h + 1) * half, :], preferred_element_type=F32)
        s1 = jnp.dot(keys_ref[1], qt_scr[(2 * h + 1) * half:(2 * h + 2) * half, :], preferred_element_type=F32)

        def store0(r, mx):
            vals_scr[0, pl.ds(r, 1), :] = mx

        def store1(r, mx):
            vals_scr[1, pl.ds(r, 1), :] = mx

        rank0 = _extract16(s0, PEER_N_KEYS, tb, store0)
        rank1 = _extract16(s1, PEER_N_KEYS, tb, store1)
        a = vals_scr[0]
        b = vals_scr[1]
        ea = jnp.exp(a - a[0:1, :])
        eb = jnp.exp(b - b[0:1, :])
        off = 0
        for r in range(PEER_TOPK):
            ncol = _CAND_COLS[r]
            cand_scr[off:off + ncol, :] = a[r:r + 1, :] + b[0:ncol, :]
            pe_scr[off:off + ncol, :] = ea[r:r + 1, :] * eb[0:ncol, :]
            off += ncol
        cand_scr[_N_CAND:_N_CAND_PAD, :] = jnp.full((_N_CAND_PAD - _N_CAND, tb), -jnp.inf, F32)
        pe_scr[_N_CAND:_N_CAND_PAD, :] = jnp.zeros((_N_CAND_PAD - _N_CAND, tb), F32)

        crank = _extract16(cand_scr[...], _N_CAND_PAD, tb, lambda r, mx: None)
        sel = (crank < float(PEER_TOPK)).astype(F32)
        zsum = jnp.sum(sel * pe_scr[...], axis=0, keepdims=True)
        nr = jnp.dot(grp_ref[...], sel.astype(BF16), preferred_element_type=F32)
        n0 = jnp.zeros((PEER_N_KEYS, tb), F32)
        for r in range(PEER_TOPK):
            n0 = jnp.where(rank0 == float(r), nr[r:r + 1, :], n0)
        n0_ref[h] = n0
        w0_ref[h] = jnp.exp(s0 - a[0:1, :]) * (0.5 / zsum)
        e1_ref[h] = jnp.exp(s1 - b[0:1, :]).astype(BF16)
        c1_ref[h] = rank1.astype(BF16)


def _peer_sel(h2t, wqt, keys, grp, *, tb=256):
    T = h2t.shape[1]
    tab = jax.ShapeDtypeStruct((PEER_HEADS, PEER_N_KEYS, T), F32)
    tab16 = jax.ShapeDtypeStruct((PEER_HEADS, PEER_N_KEYS, T), BF16)
    tab_spec = pl.BlockSpec((PEER_HEADS, PEER_N_KEYS, tb), lambda i: (0, 0, i))
    return pl.pallas_call(
        functools.partial(_peer_sel_body, tb=tb),
        grid=(T // tb,),
        in_specs=[
            pl.BlockSpec((D_MODEL, tb), lambda i: (0, i)),
            pl.BlockSpec((PEER_HEADS * PEER_KEY_DIM, D_MODEL), lambda i: (0, 0)),
            pl.BlockSpec((2, PEER_N_KEYS, PEER_KEY_DIM // 2), lambda i: (0, 0, 0)),
            pl.BlockSpec((PEER_TOPK, _N_CAND_PAD), lambda i: (0, 0)),
        ],
        out_specs=[tab_spec, tab_spec, tab_spec, tab_spec],
        out_shape=[tab16, tab16, tab, tab],
        scratch_shapes=[
            pltpu.VMEM((PEER_HEADS * PEER_KEY_DIM, tb), BF16),
            pltpu.VMEM((2, PEER_TOPK, tb), F32),
            pltpu.VMEM((_N_CAND_PAD, tb), F32),
            pltpu.VMEM((_N_CAND_PAD, tb), F32),
        ],
        compiler_params=_cparams(("parallel",)),
        name="peer_sel",
    )(h2t, wqt, keys, grp)


def _peer_dense_body(u_ref, vt_ref, h2t_ref, e1_ref, c1_ref, n0_ref, w0_ref, yt_ref, st_a, st_b, *, ec, tb, nk):
    k = pl.program_id(1)
    slabs = ec // PEER_N_KEYS

    def pre_activations(st_w):
        st_w[...] = jnp.dot(u_ref[...], h2t_ref[...], preferred_element_type=F32)

    def activate_and_project(st_r):
        parts = []
        for ii in range(slabs):
            g = jnp.zeros((PEER_N_KEYS, tb), BF16)
            for h in range(PEER_HEADS):
                n_row = n0_ref[h, ii:ii + 1, :].astype(BF16)
                w_row = w0_ref[h, ii:ii + 1, :].astype(BF16)
                g = g + jnp.where(c1_ref[h] < n_row, e1_ref[h] * w_row, jnp.zeros((), BF16))
            x = st_r[ii * PEER_N_KEYS:(ii + 1) * PEER_N_KEYS, :]
            act = x * (1.0 + lax.erf(x * math.sqrt(0.5)))
            parts.append(act.astype(BF16) * g)
        at = jnp.concatenate(parts, axis=0)
        yt_ref[...] += jnp.dot(vt_ref[...], at, preferred_element_type=F32)

    @pl.when(k == 0)
    def _():
        yt_ref[...] = jnp.zeros_like(yt_ref)
        pre_activations(st_a)

    @pl.when((k > 0) & (k < nk) & (k % 2 == 1))
    def _():
        pre_activations(st_b)
        activate_and_project(st_a)

    @pl.when((k > 0) & (k < nk) & (k % 2 == 0))
    def _():
        pre_activations(st_a)
        activate_and_project(st_b)

    @pl.when(k == nk)
    def _():
        activate_and_project(st_b if nk % 2 == 0 else st_a)


def _peer_dense(u, vt, h2t, e1, c1, n0, w0, *, tb=512, ec=1024):
    T = h2t.shape[1]
    nk = PEER_N_EXPERTS // ec
    slabs = ec // PEER_N_KEYS
    assert slabs == SUBLANES, "one f32 sublane tile of per-slab gate rows per expert chunk"
    tab_spec = pl.BlockSpec((PEER_HEADS, PEER_N_KEYS, tb), lambda i, k: (0, 0, i))
    row_spec = pl.BlockSpec((PEER_HEADS, slabs, tb), lambda i, k: (0, jnp.maximum(k - 1, 0), i))
    return pl.pallas_call(
        functools.partial(_peer_dense_body, ec=ec, tb=tb, nk=nk),
        grid=(T // tb, nk + 1),
        in_specs=[
            pl.BlockSpec((ec, D_MODEL), lambda i, k: (jnp.minimum(k, nk - 1), 0)),
            pl.BlockSpec((D_MODEL, ec), lambda i, k: (0, jnp.maximum(k - 1, 0))),
            pl.BlockSpec((D_MODEL, tb), lambda i, k: (0, i)),
            tab_spec, tab_spec, row_spec, row_spec,
        ],
        out_specs=pl.BlockSpec((D_MODEL, tb), lambda i, k: (0, i)),
        out_shape=jax.ShapeDtypeStruct((D_MODEL, T), F32),
        scratch_shapes=[pltpu.VMEM((ec, tb), F32), pltpu.VMEM((ec, tb), F32)],
        compiler_params=_cparams(("parallel", "arbitrary")),
        name="peer_dense",
    )(u, vt, h2t, e1, c1, n0, w0)


def _final_body(x1_ref, yt_ref, w_ref, o_ref):
    x2 = x1_ref[...] + yt_ref[...].T
    ms = jnp.mean(x2 * x2, axis=-1, keepdims=True)
    o_ref[...] = (x2 * lax.rsqrt(ms + NORM_EPS)) * w_ref[...]


def _final(x1, yt, w, *, tm=512):
    T = x1.shape[0]
    return pl.pallas_call(
        _final_body,
        grid=(T // tm,),
        in_specs=[
            pl.BlockSpec((tm, D_MODEL), lambda i: (i, 0)),
            pl.BlockSpec((D_MODEL, tm), lambda i: (0, i)),
            pl.BlockSpec((1, D_MODEL), lambda i: (0, 0)),
        ],
        out_specs=pl.BlockSpec((tm, D_MODEL), lambda i: (i, 0)),
        out_shape=jax.ShapeDtypeStruct((T, D_MODEL), F32),
        compiler_params=_cparams(("parallel",)),
        name="final",
    )(x1, yt, w)


def _group_matrix():
    g = np.zeros((PEER_TOPK, _N_CAND_PAD), np.float32)
    off = 0
    for r, ncol in enumerate(_CAND_COLS):
        g[r, off:off + ncol] = 1.0
        off += ncol
    return g


def _layer(x2, norm1_w, w_in, fox_f_bias, conv_w, conv_b, i_bias, f_bias, fox_nw, mlstm_nw, w_out, norm2_w,
           w_q, keys, u, v, *, batch, seq):
    splits = np.cumsum((FOX_WIDTH, FOX_WIDTH, FOX_WIDTH, FOX_HEADS, MLSTM_QK_WIDTH, MLSTM_QK_WIDTH,
                        MLSTM_V_WIDTH, MLSTM_HEADS, MLSTM_HEADS, MLSTM_V_WIDTH))[:-1]
    fq, fk, fv, ff, mq, mk, mv, mi, mf, mo = jnp.split(w_in, [int(p) for p in splits], axis=1)
    w_main = jnp.concatenate([fq, fk, fv, mv, mo, mq, mk], axis=1).astype(BF16)
    gate_pad = GATE_LANES - FOX_HEADS - 2 * MLSTM_HEADS
    w_gate = jnp.pad(jnp.concatenate([ff, mi, mf], axis=1), ((0, 0), (0, gate_pad))).astype(BF16)
    gate_bias = jnp.pad(jnp.concatenate([fox_f_bias, i_bias, f_bias]), (0, gate_pad)).reshape(1, GATE_LANES)
    tri = jnp.asarray(np.tril(np.ones((LANES, LANES), np.float32)), BF16)

    z, zc, g = _inproj(x2, norm1_w.reshape(1, D_MODEL), w_main, w_gate)
    pq, pk = _placement_matrices()
    qn, kn = _fox_norms(z, jnp.asarray(_head_group_matrix(), BF16))
    kmax = jnp.max(kn.reshape(batch, seq, GATE_LANES), axis=1)
    gc, gr, qx, kx = _gates(g, gate_bias, tri, jnp.asarray(pq, BF16), jnp.asarray(pk, BF16), qn,
                            jnp.repeat(kmax, SUBLANES, axis=0), batch=batch, seq=seq)
    lo, fast = _fox_plan(qn, kmax, gc, batch=batch, seq=seq, tk=FOX_KEY_BLOCK)
    att = _fox(z, qx, kx, fox_nw.reshape(FOX_HEADS, FOX_HEAD_DIM), lo, fast, batch=batch, seq=seq, tk=FOX_KEY_BLOCK)
    cell = _mlstm(zc, z, gc, gr, conv_w, conv_b.reshape(1, -1), mlstm_nw.reshape(1, -1), batch=batch, seq=seq)
    x1, h2t = _outproj(att, cell, x2, w_out[:FOX_WIDTH].astype(BF16), w_out[FOX_WIDTH:].astype(BF16),
                       norm2_w.reshape(1, D_MODEL))
    e1, c1, n0, w0 = _peer_sel(h2t, w_q.T.astype(BF16), keys.astype(BF16), jnp.asarray(_group_matrix(), BF16))
    yt = _peer_dense(u.astype(BF16), v.T.astype(BF16), h2t, e1, c1, n0, w0)
    return x1, yt


def kernel(x, norm1_w, w_in, fox_f_bias, mlstm_conv_w, mlstm_conv_b, mlstm_i_bias, mlstm_f_bias, fox_out_norm_w,
           mlstm_out_norm_w, w_out, norm2_w, peer_w_q, peer_keys, peer_u, peer_v, final_norm_w):
    batch, seq, _ = x.shape
    assert w_in.shape[0] == 1, "single-layer block: the final norm is fused with the last residual add"
    x2 = x.reshape(batch * seq, D_MODEL)
    x1, yt = _layer(x2, norm1_w[0], w_in[0], fox_f_bias[0], mlstm_conv_w[0], mlstm_conv_b[0], mlstm_i_bias[0],
                    mlstm_f_bias[0], fox_out_norm_w[0], mlstm_out_norm_w[0], w_out[0], norm2_w[0],
                    peer_w_q[0], peer_keys[0], peer_u[0], peer_v[0], batch=batch, seq=seq)
    out = _final(x1, yt, final_norm_w.reshape(1, D_MODEL))
    return out.reshape(batch, seq, D_MODEL)
```

```python
import functools
import math

import numpy as np
import jax
import jax.numpy as jnp
from jax import lax
from jax.experimental import pallas as pl
from jax.experimental.pallas import tpu as pltpu

F32 = jnp.float32
BF16 = jnp.bfloat16

D_MODEL = 2048
FOX_HEADS = 8
FOX_HEAD_DIM = 128
FOX_WIDTH = FOX_HEADS * FOX_HEAD_DIM
MLSTM_HEADS = 4
MLSTM_QK_DIM = 128
MLSTM_V_DIM = 256
MLSTM_QK_WIDTH = MLSTM_HEADS * MLSTM_QK_DIM
MLSTM_V_WIDTH = MLSTM_HEADS * MLSTM_V_DIM
MLSTM_CONV = 4
MLSTM_CHUNK = 128
PEER_HEADS = 8
PEER_KEY_DIM = 256
PEER_N_KEYS = 128
PEER_TOPK = 16
PEER_N_EXPERTS = PEER_N_KEYS * PEER_N_KEYS
NORM_EPS = 1e-6
LOG2E = math.log2(math.e)
NORM_MARGIN = 1.0 + 2.0 ** -7
F32_ZERO_LOG2 = 150.0
FAST_GAP_LOG2 = 90.0
BOUND_SLACK_LOG2 = 4.0
FOX_KEY_BLOCK = 512

LANES = 128
SUBLANES = 8
GATE_LANES = LANES
GATE_ROWS = 16
VMEM_LIMIT = 56 * 1024 * 1024

_CAND_COLS = tuple(PEER_TOPK // (r + 1) for r in range(PEER_TOPK))
_N_CAND = sum(_CAND_COLS)
_N_CAND_PAD = -(-_N_CAND // SUBLANES) * SUBLANES


def _cparams(sem):
    return pltpu.CompilerParams(dimension_semantics=sem, vmem_limit_bytes=VMEM_LIMIT)


def _inproj_body(x_ref, nw_ref, w_ref, wg_ref, z_ref, zc_ref, g_ref, h_scr, *, n_main, q_blocks, q_scale):
    j = pl.program_id(1)

    @pl.when(j == 0)
    def _():
        x = x_ref[...]
        ms = jnp.mean(x * x, axis=-1, keepdims=True)
        hb = ((x * lax.rsqrt(ms + NORM_EPS)) * nw_ref[...]).astype(BF16)
        h_scr[...] = hb
        g_ref[...] = jnp.dot(hb, wg_ref[...], preferred_element_type=F32)

    z = jnp.dot(h_scr[...], w_ref[...], preferred_element_type=F32)

    @pl.when(j < n_main)
    def _():
        scale = jnp.where(j < q_blocks, q_scale, 1.0).astype(F32)
        z_ref[...] = (z * scale).astype(BF16)

    @pl.when(j >= n_main)
    def _():
        zc_ref[...] = z


def _inproj(x2, norm_w, w_main, w_gate, *, tm=1024, tn=512):
    T = x2.shape[0]
    n_cols = w_main.shape[1]
    n_conv = 2 * MLSTM_QK_WIDTH
    n_main = (n_cols - n_conv) // tn
    n_blocks = n_cols // tn
    body = functools.partial(_inproj_body, n_main=n_main, q_blocks=FOX_WIDTH // tn,
                             q_scale=FOX_HEAD_DIM ** -0.5 * LOG2E)
    return pl.pallas_call(
        body,
        grid=(T // tm, n_blocks),
        in_specs=[
            pl.BlockSpec((tm, D_MODEL), lambda i, j: (i, 0)),
            pl.BlockSpec((1, D_MODEL), lambda i, j: (0, 0)),
            pl.BlockSpec((D_MODEL, tn), lambda i, j: (0, j)),
            pl.BlockSpec((D_MODEL, GATE_LANES), lambda i, j: (0, 0)),
        ],
        out_specs=[
            pl.BlockSpec((tm, tn), lambda i, j: (i, jnp.minimum(j, n_main - 1))),
            pl.BlockSpec((tm, tn), lambda i, j: (i, jnp.maximum(j - n_main, 0))),
            pl.BlockSpec((tm, GATE_LANES), lambda i, j: (i, 0)),
        ],
        out_shape=[
            jax.ShapeDtypeStruct((T, n_cols - n_conv), BF16),
            jax.ShapeDtypeStruct((T, n_conv), F32),
            jax.ShapeDtypeStruct((T, GATE_LANES), F32),
        ],
        scratch_shapes=[pltpu.VMEM((tm, D_MODEL), BF16)],
        compiler_params=_cparams(("parallel", "arbitrary")),
        name="inproj",
    )(x2, norm_w, w_main, w_gate)


def _split3(v):
    hi = v.astype(BF16)
    r1 = v - hi.astype(F32)
    mid = r1.astype(BF16)
    lo = (r1 - mid.astype(F32)).astype(BF16)
    return hi, mid, lo


def _fox_norms_body(q_ref, k_ref, grp_ref, qn_ref, kn_ref):
    for src, dst in ((q_ref, qn_ref), (k_ref, kn_ref)):
        x = src[...].astype(F32)
        ss = jnp.dot((x * x).astype(BF16), grp_ref[...], preferred_element_type=F32)
        dst[...] = jnp.sqrt(ss) * NORM_MARGIN


def _head_group_matrix():
    g = np.zeros((FOX_WIDTH, GATE_LANES), np.float32)
    for h in range(FOX_HEADS):
        g[h * FOX_HEAD_DIM:(h + 1) * FOX_HEAD_DIM, h] = 1.0
    return g


def _fox_norms(z, grp, *, rows=1024):
    T = z.shape[0]
    out = jax.ShapeDtypeStruct((T, GATE_LANES), F32)
    return pl.pallas_call(
        _fox_norms_body,
        grid=(T // rows,),
        in_specs=[
            pl.BlockSpec((rows, FOX_WIDTH), lambda i: (i, 0)),
            pl.BlockSpec((rows, FOX_WIDTH), lambda i: (i, 1)),
            pl.BlockSpec((FOX_WIDTH, GATE_LANES), lambda i: (0, 0)),
        ],
        out_specs=[pl.BlockSpec((rows, GATE_LANES), lambda i: (i, 0))] * 2,
        out_shape=[out, out],
        compiler_params=_cparams(("parallel",)),
        name="fox_norms",
    )(z, z, grp)


def _gates_body(g_ref, bias_ref, tri_ref, pq_ref, pk_ref, qn_ref, kmax_ref, gc_ref, gr_ref, qx_ref, kx_ref,
                carry_scr, *, rows):
    c = pl.program_id(1)

    @pl.when(c == 0)
    def _():
        carry_scr[...] = jnp.zeros_like(carry_scr)

    lane = lax.broadcasted_iota(jnp.int32, (LANES, GATE_LANES), 1)
    is_glob = lane < FOX_HEADS
    is_ls = is_glob | ((lane >= FOX_HEADS + MLSTM_HEADS) & (lane < FOX_HEADS + 2 * MLSTM_HEADS))
    tri = tri_ref[...]
    for s in range(rows // LANES):
        sl = slice(s * LANES, (s + 1) * LANES)
        g = g_ref[sl, :] + bias_ref[...]
        ls = jnp.minimum(g, 0.0) - jnp.log1p(jnp.exp(-jnp.abs(g)))
        v = jnp.where(is_ls, ls, 0.0)
        hi, mid, lo = _split3(v)
        cs = (jnp.dot(tri, hi, preferred_element_type=F32)
              + jnp.dot(tri, mid, preferred_element_type=F32)
              + jnp.dot(tri, lo, preferred_element_type=F32))
        glob = cs + carry_scr[...]
        carry_scr[...] = glob[LANES - 1:LANES, :]
        out = jnp.where(is_glob, glob, jnp.where(is_ls, cs, g))
        gc_ref[sl, :] = out
        gr_ref[:, sl] = out.T[0:GATE_ROWS, :]
        fhi, fmid, flo = _split3(glob * LOG2E)
        mhi, mmid, mlo = _split3(qn_ref[sl, :] * kmax_ref[0:1, :] + 1.0)
        pieces = jnp.concatenate([fhi, fmid, flo, jnp.ones((LANES, GATE_LANES), BF16), mhi, mmid, mlo], axis=1)
        qx_ref[sl, :] = jnp.dot(pieces, pq_ref[...], preferred_element_type=F32).astype(BF16)
        kx_ref[sl, :] = jnp.dot(pieces, pk_ref[...], preferred_element_type=F32).astype(BF16)


_N_PIECES = 3
_PIECE_GROUPS = 2 * _N_PIECES + 1
_STAB_SLOT = 2 * _N_PIECES


def _placement_matrices():
    n = _N_PIECES
    pq = np.zeros((_PIECE_GROUPS * GATE_LANES, FOX_WIDTH), np.float32)
    pk = np.zeros((_PIECE_GROUPS * GATE_LANES, FOX_WIDTH), np.float32)
    for h in range(FOX_HEADS):
        for p in range(n):
            pq[p * GATE_LANES + h, h * FOX_HEAD_DIM + p] = 1.0
            pk[n * GATE_LANES + h, h * FOX_HEAD_DIM + p] = 1.0
            pq[n * GATE_LANES + h, h * FOX_HEAD_DIM + n + p] = 1.0
            pk[p * GATE_LANES + h, h * FOX_HEAD_DIM + n + p] = -1.0
            pq[(n + 1 + p) * GATE_LANES + h, h * FOX_HEAD_DIM + _STAB_SLOT + p] = -1.0
            pk[n * GATE_LANES + h, h * FOX_HEAD_DIM + _STAB_SLOT + p] = 1.0
    return pq, pk


def _gates(g, bias, tri, pq, pk, qn, kmax, *, batch, seq, rows=1024):
    T = g.shape[0]
    nblk = seq // rows
    return pl.pallas_call(
        functools.partial(_gates_body, rows=rows),
        grid=(batch, nblk),
        in_specs=[
            pl.BlockSpec((rows, GATE_LANES), lambda b, c: (b * nblk + c, 0)),
            pl.BlockSpec((1, GATE_LANES), lambda b, c: (0, 0)),
            pl.BlockSpec((LANES, LANES), lambda b, c: (0, 0)),
            pl.BlockSpec((_PIECE_GROUPS * GATE_LANES, FOX_WIDTH), lambda b, c: (0, 0)),
            pl.BlockSpec((_PIECE_GROUPS * GATE_LANES, FOX_WIDTH), lambda b, c: (0, 0)),
            pl.BlockSpec((rows, GATE_LANES), lambda b, c: (b * nblk + c, 0)),
            pl.BlockSpec((SUBLANES, GATE_LANES), lambda b, c: (b, 0)),
        ],
        out_specs=[
            pl.BlockSpec((rows, GATE_LANES), lambda b, c: (b * nblk + c, 0)),
            pl.BlockSpec((GATE_ROWS, rows), lambda b, c: (0, b * nblk + c)),
            pl.BlockSpec((rows, FOX_WIDTH), lambda b, c: (b * nblk + c, 0)),
            pl.BlockSpec((rows, FOX_WIDTH), lambda b, c: (b * nblk + c, 0)),
        ],
        out_shape=[
            jax.ShapeDtypeStruct((T, GATE_LANES), F32),
            jax.ShapeDtypeStruct((GATE_ROWS, T), F32),
            jax.ShapeDtypeStruct((T, FOX_WIDTH), BF16),
            jax.ShapeDtypeStruct((T, FOX_WIDTH), BF16),
        ],
        scratch_shapes=[pltpu.VMEM((1, GATE_LANES), F32)],
        compiler_params=_cparams(("parallel", "arbitrary")),
        name="gates",
    )(g, bias, tri, pq, pk, qn, kmax)


def _fox_body(lo_ref, fast_ref, q_ref, qx_ref, k_ref, kx_ref, v_ref, nw_ref, o_ref, *, tk, nq):
    b = pl.program_id(0)
    h = pl.program_id(1)
    qi = pl.program_id(2)
    tile = (b * FOX_HEADS + h) * nq + qi
    lo = lo_ref[tile]
    row = lax.broadcasted_iota(jnp.int32, (tk, tk), 0)
    col = lax.broadcasted_iota(jnp.int32, (tk, tk), 1)

    def load_kv(ki):
        start = pl.multiple_of(ki * tk, tk)
        k = jnp.concatenate([k_ref[pl.ds(start, tk), :], kx_ref[pl.ds(start, tk), :]], axis=1)
        return k, v_ref[pl.ds(start, tk), :]

    def logits(q, k, masked):
        s = lax.dot_general(q, k, (((1,), (1,)), ((), ())), preferred_element_type=F32)
        return jnp.where(row >= col, s, -jnp.inf) if masked else s

    def finish(r, num, den):
        out = num / den
        ms = jnp.mean(out * out, axis=-1, keepdims=True)
        o_ref[r * tk:(r + 1) * tk, :] = ((out * lax.rsqrt(ms + NORM_EPS)) * nw_ref[pl.ds(h, 1), :]).astype(BF16)

    @pl.when(fast_ref[tile] == 1)
    def _():
        q_halves = [jnp.concatenate([q_ref[r * tk:(r + 1) * tk, :], qx_ref[r * tk:(r + 1) * tk, :]], axis=1)
                    for r in range(2)]
        ones_col = (lax.broadcasted_iota(jnp.int32, (tk, LANES), 1) == 0).astype(BF16)

        def weights(r, k, masked):
            return jnp.exp2(logits(q_halves[r], k, masked)).astype(BF16)

        def values(ki):
            return jnp.concatenate([load_kv(ki)[1], ones_col], axis=1)

        def add(acc, p, va):
            return acc + jnp.dot(p, va, preferred_element_type=F32)

        def pair(kp, accs):
            blocks = [2 * kp, 2 * kp + 1]
            ps = [[weights(r, load_kv(ki)[0], False) for r in range(2)] for ki in blocks]
            for j, ki in enumerate(blocks):
                va = values(ki)
                accs = tuple(add(accs[r], ps[j][r], va) for r in range(2))
            return accs

        zero = jnp.zeros((tk, 2 * FOX_HEAD_DIM), F32)
        aa, ab = lax.fori_loop(lo // 2, qi, pair, (zero, zero))
        k = load_kv(2 * qi)[0]
        pa, pb = weights(0, k, True), weights(1, k, False)
        pb2 = weights(1, load_kv(2 * qi + 1)[0], True)
        va = values(2 * qi)
        aa, ab = add(aa, pa, va), add(ab, pb, va)
        ab = add(ab, pb2, values(2 * qi + 1))
        for r, acc in enumerate((aa, ab)):
            finish(r, acc[:, 0:FOX_HEAD_DIM], acc[:, FOX_HEAD_DIM:FOX_HEAD_DIM + 1])

    @pl.when(fast_ref[tile] == 0)
    def _():
        _fox_online(q_ref, qx_ref, load_kv, logits, finish, lo, qi, tk)


def _fox_online(q_ref, qx_ref, load_kv, logits, finish, lo, qi, tk):
    lane = lax.broadcasted_iota(jnp.int32, (tk, LANES), 1)
    no_stab = (lane < _STAB_SLOT) | (lane >= _STAB_SLOT + _N_PIECES)
    q_halves = [jnp.concatenate([q_ref[r * tk:(r + 1) * tk, :],
                                 jnp.where(no_stab, qx_ref[r * tk:(r + 1) * tk, :], jnp.zeros((), BF16))], axis=1)
                for r in range(2)]

    def weights(s, carry):
        m, l, acc = carry
        cols = [s[:, c * LANES:(c + 1) * LANES] for c in range(tk // LANES)]
        m_new = jnp.maximum(m, jnp.max(functools.reduce(jnp.maximum, cols), axis=-1, keepdims=True))
        alpha = jnp.exp2(m - m_new)
        ps = [jnp.exp2(c - m_new) for c in cols]
        l = alpha * l + functools.reduce(jnp.add, ps)
        return m_new, l, alpha * acc, jnp.concatenate(ps, axis=1).astype(BF16)

    def update(q, k, v, carry, masked):
        m, l, acc, p = weights(logits(q, k, masked), carry)
        return m, l, acc + jnp.dot(p, v, preferred_element_type=F32)

    def both(ki, carries):
        k, v = load_kv(ki)
        s = [logits(q_halves[r], k, False) for r in range(2)]
        w = [weights(s[r], carries[r]) for r in range(2)]
        return tuple((m, l, acc + jnp.dot(p, v, preferred_element_type=F32)) for m, l, acc, p in w)

    init = (jnp.full((tk, LANES), -jnp.inf, F32), jnp.zeros((tk, LANES), F32), jnp.zeros((tk, FOX_HEAD_DIM), F32))
    ca, cb = lax.fori_loop(lo, 2 * qi, both, (init, init))
    k, v = load_kv(2 * qi)
    ca = update(q_halves[0], k, v, ca, True)
    cb = update(q_halves[1], k, v, cb, False)
    k, v = load_kv(2 * qi + 1)
    cb = update(q_halves[1], k, v, cb, True)
    for r, (_, l, acc) in enumerate((ca, cb)):
        finish(r, acc, jnp.sum(l, axis=-1, keepdims=True))


def _fox_plan(qn, kmax, gc, *, batch, seq, tk):
    tq = 2 * tk
    nq = seq // tq
    nk = seq // tk
    heads = slice(0, FOX_HEADS)
    qmax = jnp.max(qn.reshape(batch, nq, tq, GATE_LANES), axis=2)[..., heads]
    gap = 2.0 * qmax * kmax[:, None, heads] + BOUND_SLACK_LOG2
    f2 = (gc[:, heads] * LOG2E).reshape(batch, nk, tk, FOX_HEADS)
    f_first = f2[:, ::2, 0, :]
    f_last = f2[:, :, tk - 1, :]
    bound = gap[:, :, None, :] + f_first[:, :, None, :] - f_last[:, None, :, :]
    below_diag = jnp.arange(nk)[None, :, None] < 2 * jnp.arange(nq)[:, None, None]
    skip = (bound < -F32_ZERO_LOG2) & below_diag[None]
    lo = jnp.sum(jnp.cumprod(skip.astype(jnp.int32), axis=2), axis=2)
    fast = (gap <= FAST_GAP_LOG2).astype(jnp.int32)
    flat = lambda a: jnp.transpose(a, (0, 2, 1)).reshape(-1)
    return flat(lo), flat(fast)


def _fox(z, qx, kx, nw, lo, fast, *, batch, seq, tk=512):
    T = z.shape[0]
    tq = 2 * tk
    nq = seq // tq
    kcol = FOX_WIDTH // FOX_HEAD_DIM
    grid_spec = pltpu.PrefetchScalarGridSpec(
        num_scalar_prefetch=2,
        grid=(batch, FOX_HEADS, nq),
        in_specs=[
            pl.BlockSpec((tq, FOX_HEAD_DIM), lambda b, h, i, lo, fast: (b * nq + i, h)),
            pl.BlockSpec((tq, FOX_HEAD_DIM), lambda b, h, i, lo, fast: (b * nq + i, h)),
            pl.BlockSpec((seq, FOX_HEAD_DIM), lambda b, h, i, lo, fast: (b, kcol + h)),
            pl.BlockSpec((seq, FOX_HEAD_DIM), lambda b, h, i, lo, fast: (b, h)),
            pl.BlockSpec((seq, FOX_HEAD_DIM), lambda b, h, i, lo, fast: (b, 2 * kcol + h)),
            pl.BlockSpec((FOX_HEADS, FOX_HEAD_DIM), lambda b, h, i, lo, fast: (0, 0)),
        ],
        out_specs=pl.BlockSpec((tq, FOX_HEAD_DIM), lambda b, h, i, lo, fast: (b * nq + i, h)),
    )
    return pl.pallas_call(
        functools.partial(_fox_body, tk=tk, nq=nq),
        grid_spec=grid_spec,
        out_shape=jax.ShapeDtypeStruct((T, FOX_WIDTH), BF16),
        compiler_params=_cparams(("parallel", "parallel", "arbitrary")),
        name="fox",
    )(lo, fast, z, qx, z, kx, z, nw)


def _mlstm_body(zc_ref, zprev_ref, v_ref, o_ref, gc_ref, gr_ref, cw_ref, cb_ref, nw_ref, out_ref,
                full_scr, c_scr, n_scr, m_scr):
    c = pl.program_id(1)
    L = MLSTM_CHUNK
    dk = MLSTM_QK_DIM
    dv = MLSTM_V_DIM

    @pl.when(c == 0)
    def _():
        c_scr[...] = jnp.zeros_like(c_scr)
        n_scr[...] = jnp.zeros_like(n_scr)
        m_scr[...] = jnp.zeros_like(m_scr)

    full_scr[0:SUBLANES, :] = jnp.where(c == 0, 0.0, zprev_ref[...])
    full_scr[SUBLANES:SUBLANES + L, :] = zc_ref[...]
    y = cb_ref[...]
    for j in range(MLSTM_CONV):
        y = y + cw_ref[j:j + 1, :] * full_scr[pl.ds(SUBLANES - (MLSTM_CONV - 1) + j, L), :]
    qk_all = y * jax.nn.sigmoid(y)

    gcb = gc_ref[...]
    grb = gr_ref[...]
    row = lax.broadcasted_iota(jnp.int32, (L, L), 0)
    col = lax.broadcasted_iota(jnp.int32, (L, L), 1)
    causal = row >= col
    for hh in range(MLSTM_HEADS):
        qh = qk_all[:, hh * dk:(hh + 1) * dk]
        kh = qk_all[:, MLSTM_QK_WIDTH + hh * dk:MLSTM_QK_WIDTH + (hh + 1) * dk] * (dk ** -0.5)
        vh = v_ref[:, hh * dv:(hh + 1) * dv]
        oh = o_ref[:, hh * dv:(hh + 1) * dv].astype(F32)
        li = FOX_HEADS + hh
        lf = FOX_HEADS + MLSTM_HEADS + hh
        i_col = gcb[:, li:li + 1]
        b_col = gcb[:, lf:lf + 1]
        i_row = grb[li:li + 1, :]
        b_row = grb[lf:lf + 1, :]
        b_last = b_row[:, L - 1:L]

        c_prev = c_scr[hh]
        n_prev = n_scr[hh]
        m_prev = m_scr[hh][:, 0:1]

        m_loc = jnp.max(b_last - b_row + i_row, axis=-1, keepdims=True)
        a_end = jnp.exp(b_last - b_col + i_col - m_loc)
        kw = kh * a_end
        c_loc = lax.dot_general(kw.astype(BF16), vh, (((0,), (0,)), ((), ())), preferred_element_type=F32)
        n_loc = jnp.sum(kw, axis=0, keepdims=True)

        g_col = b_col + m_prev
        dmat = jnp.where(causal, b_col - b_row + i_row, -jnp.inf)
        m_t = jnp.maximum(g_col, jnp.max(dmat, axis=-1, keepdims=True))
        qb = qh.astype(BF16)
        qk = lax.dot_general(qb, kh.astype(BF16), (((1,), (1,)), ((), ())), preferred_element_type=F32)
        sm = qk * jnp.exp(dmat - m_t)
        inter = jnp.exp(g_col - m_t)
        num = (jnp.dot(sm.astype(BF16), vh, preferred_element_type=F32)
               + inter * jnp.dot(qb, c_prev.astype(BF16), preferred_element_type=F32))
        den = jnp.sum(sm, axis=-1, keepdims=True) + inter * jnp.sum(qh * n_prev, axis=-1, keepdims=True)
        cell = num / jnp.maximum(jnp.abs(den), jnp.exp(-m_t))
        gated = jax.nn.sigmoid(oh) * cell
        ms = jnp.mean(gated * gated, axis=-1, keepdims=True)
        out_ref[:, hh * dv:(hh + 1) * dv] = (
            (gated * lax.rsqrt(ms + NORM_EPS)) * nw_ref[:, hh * dv:(hh + 1) * dv]).astype(BF16)

        m_new = jnp.maximum(b_last + m_prev, m_loc)
        a_prev = jnp.exp(b_last + m_prev - m_new)
        a_loc = jnp.exp(m_loc - m_new)
        c_scr[hh] = a_prev * c_prev + a_loc * c_loc
        n_scr[hh] = a_prev * n_prev + a_loc * n_loc
        m_scr[hh] = jnp.broadcast_to(m_new, (1, LANES))


def _mlstm(zc, z, gc, gr, conv_w, conv_b, nw, *, batch, seq):
    T = zc.shape[0]
    L = MLSTM_CHUNK
    nc = seq // L
    per = L // SUBLANES
    vcol = 3 * FOX_WIDTH // MLSTM_V_WIDTH
    return pl.pallas_call(
        _mlstm_body,
        grid=(batch, nc),
        in_specs=[
            pl.BlockSpec((L, 2 * MLSTM_QK_WIDTH), lambda b, c: (b * nc + c, 0)),
            pl.BlockSpec((SUBLANES, 2 * MLSTM_QK_WIDTH), lambda b, c: (jnp.maximum((b * nc + c) * per - 1, 0), 0)),
            pl.BlockSpec((L, MLSTM_V_WIDTH), lambda b, c: (b * nc + c, vcol)),
            pl.BlockSpec((L, MLSTM_V_WIDTH), lambda b, c: (b * nc + c, vcol + 1)),
            pl.BlockSpec((L, GATE_LANES), lambda b, c: (b * nc + c, 0)),
            pl.BlockSpec((GATE_ROWS, L), lambda b, c: (0, b * nc + c)),
            pl.BlockSpec((MLSTM_CONV, 2 * MLSTM_QK_WIDTH), lambda b, c: (0, 0)),
            pl.BlockSpec((1, 2 * MLSTM_QK_WIDTH), lambda b, c: (0, 0)),
            pl.BlockSpec((1, MLSTM_V_WIDTH), lambda b, c: (0, 0)),
        ],
        out_specs=pl.BlockSpec((L, MLSTM_V_WIDTH), lambda b, c: (b * nc + c, 0)),
        out_shape=jax.ShapeDtypeStruct((T, MLSTM_V_WIDTH), BF16),
        scratch_shapes=[
            pltpu.VMEM((SUBLANES + L, 2 * MLSTM_QK_WIDTH), F32),
            pltpu.VMEM((MLSTM_HEADS, MLSTM_QK_DIM, MLSTM_V_DIM), F32),
            pltpu.VMEM((MLSTM_HEADS, 1, MLSTM_QK_DIM), F32),
            pltpu.VMEM((MLSTM_HEADS, 1, LANES), F32),
        ],
        compiler_params=_cparams(("parallel", "arbitrary")),
        name="mlstm",
    )(zc, zc, z, z, gc, gr, conv_w, conv_b, nw)


def _outproj_body(att_ref, cell_ref, x_ref, wa_ref, wb_ref, n2_ref, x1_ref, h2t_ref):
    y = (jnp.dot(att_ref[...], wa_ref[...], preferred_element_type=F32)
         + jnp.dot(cell_ref[...], wb_ref[...], preferred_element_type=F32))
    x1 = x_ref[...] + y
    x1_ref[...] = x1
    ms = jnp.mean(x1 * x1, axis=-1, keepdims=True)
    h2 = (x1 * lax.rsqrt(ms + NORM_EPS)) * n2_ref[...]
    h2t_ref[...] = h2.T.astype(BF16)


def _outproj(att, cell, x2, wa, wb, n2, *, tm=512):
    T = x2.shape[0]
    return pl.pallas_call(
        _outproj_body,
        grid=(T // tm,),
        in_specs=[
            pl.BlockSpec((tm, FOX_WIDTH), lambda i: (i, 0)),
            pl.BlockSpec((tm, MLSTM_V_WIDTH), lambda i: (i, 0)),
            pl.BlockSpec((tm, D_MODEL), lambda i: (i, 0)),
            pl.BlockSpec((FOX_WIDTH, D_MODEL), lambda i: (0, 0)),
            pl.BlockSpec((MLSTM_V_WIDTH, D_MODEL), lambda i: (0, 0)),
            pl.BlockSpec((1, D_MODEL), lambda i: (0, 0)),
        ],
        out_specs=[
            pl.BlockSpec((tm, D_MODEL), lambda i: (i, 0)),
            pl.BlockSpec((D_MODEL, tm), lambda i: (0, i)),
        ],
        out_shape=[
            jax.ShapeDtypeStruct((T, D_MODEL), F32),
            jax.ShapeDtypeStruct((D_MODEL, T), BF16),
        ],
        compiler_params=_cparams(("parallel",)),
        name="outproj",
    )(att, cell, x2, wa, wb, n2)


def _row_iota(n_rows, tb):
    return lax.broadcasted_iota(jnp.int32, (n_rows, tb), 0).astype(F32)


def _extract16(s, val_ref, idx_ref):
    n_rows, tb = s.shape
    iota = _row_iota(n_rows, tb)

    def body(r, sw):
        mx = jnp.max(sw, axis=0, keepdims=True)
        first = jnp.min(jnp.where(sw == mx, iota, float(n_rows)), axis=0, keepdims=True)
        val_ref[pl.ds(r, 1), :] = mx
        idx_ref[pl.ds(r, 1), :] = first
        return jnp.where(iota == first, -jnp.inf, sw)

    lax.fori_loop(0, PEER_TOPK, body, s)


def _scatter_rounds(idx, values, n_rows, fill):
    tb = idx.shape[1]
    iota = _row_iota(n_rows, tb)
    table = jnp.full((n_rows, tb), fill, F32)
    for r in range(PEER_TOPK):
        v = values[r:r + 1, :] if values is not None else float(r)
        table = jnp.where(iota == idx[r:r + 1, :], v, table)
    return table


def _peer_sel_body(h2t_ref, wqt_ref, keys_ref, grp_ref, e1_ref, c1_ref, n0_ref, w0_ref,
                   qt_scr, vals_scr, idx_scr, cand_scr, pe_scr, *, tb):
    half = PEER_KEY_DIM // 2
    qt_scr[...] = jnp.dot(wqt_ref[...], h2t_ref[...], preferred_element_type=F32).astype(BF16)

    for h in range(PEER_HEADS):
        s0 = jnp.dot(keys_ref[0], qt_scr[(2 * h) * half:(2 * h + 1) * half, :], preferred_element_type=F32)
        s1 = jnp.dot(keys_ref[1], qt_scr[(2 * h + 1) * half:(2 * h + 2) * half, :], preferred_element_type=F32)

        _extract16(s0, vals_scr.at[0], idx_scr.at[0])
        _extract16(s1, vals_scr.at[1], idx_scr.at[1])
        a = vals_scr[0]
        b = vals_scr[1]
        ea = jnp.exp(a - a[0:1, :])
        eb = jnp.exp(b - b[0:1, :])
        off = 0
        for r in range(PEER_TOPK):
            ncol = _CAND_COLS[r]
            cand_scr[off:off + ncol, :] = a[r:r + 1, :] + b[0:ncol, :]
            pe_scr[off:off + ncol, :] = ea[r:r + 1, :] * eb[0:ncol, :]
            off += ncol
        cand_scr[_N_CAND:_N_CAND_PAD, :] = jnp.full((_N_CAND_PAD - _N_CAND, tb), -jnp.inf, F32)
        pe_scr[_N_CAND:_N_CAND_PAD, :] = jnp.zeros((_N_CAND_PAD - _N_CAND, tb), F32)

        _extract16(cand_scr[...], vals_scr.at[2], idx_scr.at[2])
        sel = _scatter_rounds(idx_scr[2], jnp.ones((PEER_TOPK, tb), F32), _N_CAND_PAD, 0.0)
        zsum = jnp.sum(sel * pe_scr[...], axis=0, keepdims=True)
        nr = jnp.dot(grp_ref[...], sel.astype(BF16), preferred_element_type=F32)
        n0_ref[h] = _scatter_rounds(idx_scr[0], nr, PEER_N_KEYS, 0.0)
        w0_ref[h] = jnp.exp(s0 - a[0:1, :]) * (0.5 / zsum)
        e1_ref[h] = jnp.exp(s1 - b[0:1, :]).astype(BF16)
        c1_ref[h] = _scatter_rounds(idx_scr[1], None, PEER_N_KEYS, float(PEER_TOPK)).astype(BF16)


def _peer_sel(h2t, wqt, keys, grp, *, tb=256):
    T = h2t.shape[1]
    tab = jax.ShapeDtypeStruct((PEER_HEADS, PEER_N_KEYS, T), F32)
    tab16 = jax.ShapeDtypeStruct((PEER_HEADS, PEER_N_KEYS, T), BF16)
    tab_spec = pl.BlockSpec((PEER_HEADS, PEER_N_KEYS, tb), lambda i: (0, 0, i))
    return pl.pallas_call(
        functools.partial(_peer_sel_body, tb=tb),
        grid=(T // tb,),
        in_specs=[
            pl.BlockSpec((D_MODEL, tb), lambda i: (0, i)),
            pl.BlockSpec((PEER_HEADS * PEER_KEY_DIM, D_MODEL), lambda i: (0, 0)),
            pl.BlockSpec((2, PEER_N_KEYS, PEER_KEY_DIM // 2), lambda i: (0, 0, 0)),
            pl.BlockSpec((PEER_TOPK, _N_CAND_PAD), lambda i: (0, 0)),
        ],
        out_specs=[tab_spec, tab_spec, tab_spec, tab_spec],
        out_shape=[tab16, tab16, tab, tab],
        scratch_shapes=[
            pltpu.VMEM((PEER_HEADS * PEER_KEY_DIM, tb), BF16),
            pltpu.VMEM((3, PEER_TOPK, tb), F32),
            pltpu.VMEM((3, PEER_TOPK, tb), F32),
            pltpu.VMEM((_N_CAND_PAD, tb), F32),
            pltpu.VMEM((_N_CAND_PAD, tb), F32),
        ],
        compiler_params=_cparams(("parallel",)),
        name="peer_sel",
    )(h2t, wqt, keys, grp)


def _peer_dense_body(u_ref, vt_ref, h2t_ref, e1_ref, c1_ref, n0_ref, w0_ref, yt_ref, st_a, st_b, *, ec, tb, nk):
    k = pl.program_id(1)
    slabs = ec // PEER_N_KEYS

    def pre_activations(st_w):
        st_w[...] = jnp.dot(u_ref[...], h2t_ref[...], preferred_element_type=F32)

    def activate_and_project(st_r):
        parts = []
        for ii in range(slabs):
            g = jnp.zeros((PEER_N_KEYS, tb), BF16)
            for h in range(PEER_HEADS):
                n_row = n0_ref[h, ii:ii + 1, :].astype(BF16)
                w_row = w0_ref[h, ii:ii + 1, :].astype(BF16)
                g = g + jnp.where(c1_ref[h] < n_row, e1_ref[h] * w_row, jnp.zeros((), BF16))
            x = st_r[ii * PEER_N_KEYS:(ii + 1) * PEER_N_KEYS, :]
            act = x * (1.0 + lax.erf(x * math.sqrt(0.5)))
            parts.append(act.astype(BF16) * g)
        at = jnp.concatenate(parts, axis=0)
        yt_ref[...] += jnp.dot(vt_ref[...], at, preferred_element_type=F32)

    @pl.when(k == 0)
    def _():
        yt_ref[...] = jnp.zeros_like(yt_ref)
        pre_activations(st_a)

    @pl.when((k > 0) & (k < nk) & (k % 2 == 1))
    def _():
        pre_activations(st_b)
        activate_and_project(st_a)

    @pl.when((k > 0) & (k < nk) & (k % 2 == 0))
    def _():
        pre_activations(st_a)
        activate_and_project(st_b)

    @pl.when(k == nk)
    def _():
        activate_and_project(st_b if nk % 2 == 0 else st_a)


def _peer_dense(u, vt, h2t, e1, c1, n0, w0, *, tb=512, ec=1024):
    T = h2t.shape[1]
    nk = PEER_N_EXPERTS // ec
    slabs = ec // PEER_N_KEYS
    assert slabs == SUBLANES, "one f32 sublane tile of per-slab gate rows per expert chunk"
    tab_spec = pl.BlockSpec((PEER_HEADS, PEER_N_KEYS, tb), lambda i, k: (0, 0, i))
    row_spec = pl.BlockSpec((PEER_HEADS, slabs, tb), lambda i, k: (0, jnp.maximum(k - 1, 0), i))
    return pl.pallas_call(
        functools.partial(_peer_dense_body, ec=ec, tb=tb, nk=nk),
        grid=(T // tb, nk + 1),
        in_specs=[
            pl.BlockSpec((ec, D_MODEL), lambda i, k: (jnp.minimum(k, nk - 1), 0)),
            pl.BlockSpec((D_MODEL, ec), lambda i, k: (0, jnp.maximum(k - 1, 0))),
            pl.BlockSpec((D_MODEL, tb), lambda i, k: (0, i)),
            tab_spec, tab_spec, row_spec, row_spec,
        ],
        out_specs=pl.BlockSpec((D_MODEL, tb), lambda i, k: (0, i)),
        out_shape=jax.ShapeDtypeStruct((D_MODEL, T), F32),
        scratch_shapes=[pltpu.VMEM((ec, tb), F32), pltpu.VMEM((ec, tb), F32)],
        compiler_params=_cparams(("parallel", "arbitrary")),
        name="peer_dense",
    )(u, vt, h2t, e1, c1, n0, w0)


def _final_body(x1_ref, yt_ref, w_ref, o_ref):
    x2 = x1_ref[...] + yt_ref[...].T
    ms = jnp.mean(x2 * x2, axis=-1, keepdims=True)
    o_ref[...] = (x2 * lax.rsqrt(ms + NORM_EPS)) * w_ref[...]


def _final(x1, yt, w, *, tm=512):
    T = x1.shape[0]
    return pl.pallas_call(
        _final_body,
        grid=(T // tm,),
        in_specs=[
            pl.BlockSpec((tm, D_MODEL), lambda i: (i, 0)),
            pl.BlockSpec((D_MODEL, tm), lambda i: (0, i)),
            pl.BlockSpec((1, D_MODEL), lambda i: (0, 0)),
        ],
        out_specs=pl.BlockSpec((tm, D_MODEL), lambda i: (i, 0)),
        out_shape=jax.ShapeDtypeStruct((T, D_MODEL), F32),
        compiler_params=_cparams(("parallel",)),
        name="final",
    )(x1, yt, w)


def _group_matrix():
    g = np.zeros((PEER_TOPK, _N_CAND_PAD), np.float32)
    off = 0
    for r, ncol in enumerate(_CAND_COLS):
        g[r, off:off + ncol] = 1.0
        off += ncol
    return g


def _layer(x2, norm1_w, w_in, fox_f_bias, conv_w, conv_b, i_bias, f_bias, fox_nw, mlstm_nw, w_out, norm2_w,
           w_q, keys, u, v, *, batch, seq):
    splits = np.cumsum((FOX_WIDTH, FOX_WIDTH, FOX_WIDTH, FOX_HEADS, MLSTM_QK_WIDTH, MLSTM_QK_WIDTH,
                        MLSTM_V_WIDTH, MLSTM_HEADS, MLSTM_HEADS, MLSTM_V_WIDTH))[:-1]
    fq, fk, fv, ff, mq, mk, mv, mi, mf, mo = jnp.split(w_in, [int(p) for p in splits], axis=1)
    w_main = jnp.concatenate([fq, fk, fv, mv, mo, mq, mk], axis=1).astype(BF16)
    gate_pad = GATE_LANES - FOX_HEADS - 2 * MLSTM_HEADS
    w_gate = jnp.pad(jnp.concatenate([ff, mi, mf], axis=1), ((0, 0), (0, gate_pad))).astype(BF16)
    gate_bias = jnp.pad(jnp.concatenate([fox_f_bias, i_bias, f_bias]), (0, gate_pad)).reshape(1, GATE_LANES)
    tri = jnp.asarray(np.tril(np.ones((LANES, LANES), np.float32)), BF16)

    z, zc, g = _inproj(x2, norm1_w.reshape(1, D_MODEL), w_main, w_gate)
    pq, pk = _placement_matrices()
    qn, kn = _fox_norms(z, jnp.asarray(_head_group_matrix(), BF16))
    kmax = jnp.max(kn.reshape(batch, seq, GATE_LANES), axis=1)
    gc, gr, qx, kx = _gates(g, gate_bias, tri, jnp.asarray(pq, BF16), jnp.asarray(pk, BF16), qn,
                            jnp.repeat(kmax, SUBLANES, axis=0), batch=batch, seq=seq)
    lo, fast = _fox_plan(qn, kmax, gc, batch=batch, seq=seq, tk=FOX_KEY_BLOCK)
    att = _fox(z, qx, kx, fox_nw.reshape(FOX_HEADS, FOX_HEAD_DIM), lo, fast, batch=batch, seq=seq, tk=FOX_KEY_BLOCK)
    cell = _mlstm(zc, z, gc, gr, conv_w, conv_b.reshape(1, -1), mlstm_nw.reshape(1, -1), batch=batch, seq=seq)
    x1, h2t = _outproj(att, cell, x2, w_out[:FOX_WIDTH].astype(BF16), w_out[FOX_WIDTH:].astype(BF16),
                       norm2_w.reshape(1, D_MODEL))
    e1, c1, n0, w0 = _peer_sel(h2t, w_q.T.astype(BF16), keys.astype(BF16), jnp.asarray(_group_matrix(), BF16))
    yt = _peer_dense(u.astype(BF16), v.T.astype(BF16), h2t, e1, c1, n0, w0)
    return x1, yt


def kernel(x, norm1_w, w_in, fox_f_bias, mlstm_conv_w, mlstm_conv_b, mlstm_i_bias, mlstm_f_bias, fox_out_norm_w,
           mlstm_out_norm_w, w_out, norm2_w, peer_w_q, peer_keys, peer_u, peer_v, final_norm_w):
    batch, seq, _ = x.shape
    assert w_in.shape[0] == 1, "single-layer block: the final norm is fused with the last residual add"
    x2 = x.reshape(batch * seq, D_MODEL)
    x1, yt = _layer(x2, norm1_w[0], w_in[0], fox_f_bias[0], mlstm_conv_w[0], mlstm_conv_b[0], mlstm_i_bias[0],
                    mlstm_f_bias[0], fox_out_norm_w[0], mlstm_out_norm_w[0], w_out[0], norm2_w[0],
                    peer_w_q[0], peer_keys[0], peer_u[0], peer_v[0], batch=batch, seq=seq)
    out = _final(x1, yt, final_norm_w.reshape(1, D_MODEL))
    return out.reshape(batch, seq, D_MODEL)
```

```python
import functools
import math

import numpy as np
import jax
import jax.numpy as jnp
from jax import lax
from jax.experimental import pallas as pl
from jax.experimental.pallas import tpu as pltpu

F32 = jnp.float32
BF16 = jnp.bfloat16

D_MODEL = 2048
FOX_HEADS = 8
FOX_HEAD_DIM = 128
FOX_WIDTH = FOX_HEADS * FOX_HEAD_DIM
MLSTM_HEADS = 4
MLSTM_QK_DIM = 128
MLSTM_V_DIM = 256
MLSTM_QK_WIDTH = MLSTM_HEADS * MLSTM_QK_DIM
MLSTM_V_WIDTH = MLSTM_HEADS * MLSTM_V_DIM
MLSTM_CONV = 4
MLSTM_CHUNK = 128
PEER_HEADS = 8
PEER_KEY_DIM = 256
PEER_N_KEYS = 128
PEER_TOPK = 16
PEER_N_EXPERTS = PEER_N_KEYS * PEER_N_KEYS
NORM_EPS = 1e-6
LOG2E = math.log2(math.e)
NORM_MARGIN = 1.0 + 2.0 ** -7
F32_ZERO_LOG2 = 150.0
FAST_GAP_LOG2 = 90.0
BOUND_SLACK_LOG2 = 4.0
FOX_KEY_BLOCK = 512

LANES = 128
SUBLANES = 8
GATE_LANES = LANES
GATE_ROWS = 16
VMEM_LIMIT = 56 * 1024 * 1024

_CAND_COLS = tuple(PEER_TOPK // (r + 1) for r in range(PEER_TOPK))
_N_CAND = sum(_CAND_COLS)
_N_CAND_PAD = -(-_N_CAND // SUBLANES) * SUBLANES


def _cparams(sem):
    return pltpu.CompilerParams(dimension_semantics=sem, vmem_limit_bytes=VMEM_LIMIT)


def _inproj_body(x_ref, nw_ref, w_ref, wg_ref, z_ref, zc_ref, g_ref, h_scr, *, n_main, q_blocks, q_scale):
    j = pl.program_id(1)

    @pl.when(j == 0)
    def _():
        x = x_ref[...]
        ms = jnp.mean(x * x, axis=-1, keepdims=True)
        hb = ((x * lax.rsqrt(ms + NORM_EPS)) * nw_ref[...]).astype(BF16)
        h_scr[...] = hb
        g_ref[...] = jnp.dot(hb, wg_ref[...], preferred_element_type=F32)

    z = jnp.dot(h_scr[...], w_ref[...], preferred_element_type=F32)

    @pl.when(j < n_main)
    def _():
        scale = jnp.where(j < q_blocks, q_scale, 1.0).astype(F32)
        z_ref[...] = (z * scale).astype(BF16)

    @pl.when(j >= n_main)
    def _():
        zc_ref[...] = z


def _inproj(x2, norm_w, w_main, w_gate, *, tm=1024, tn=1024):
    T = x2.shape[0]
    n_cols = w_main.shape[1]
    n_conv = 2 * MLSTM_QK_WIDTH
    n_main = (n_cols - n_conv) // tn
    n_blocks = n_cols // tn
    body = functools.partial(_inproj_body, n_main=n_main, q_blocks=FOX_WIDTH // tn,
                             q_scale=FOX_HEAD_DIM ** -0.5 * LOG2E)
    return pl.pallas_call(
        body,
        grid=(T // tm, n_blocks),
        in_specs=[
            pl.BlockSpec((tm, D_MODEL), lambda i, j: (i, 0)),
            pl.BlockSpec((1, D_MODEL), lambda i, j: (0, 0)),
            pl.BlockSpec((D_MODEL, tn), lambda i, j: (0, j)),
            pl.BlockSpec((D_MODEL, GATE_LANES), lambda i, j: (0, 0)),
        ],
        out_specs=[
            pl.BlockSpec((tm, tn), lambda i, j: (i, jnp.minimum(j, n_main - 1))),
            pl.BlockSpec((tm, tn), lambda i, j: (i, jnp.maximum(j - n_main, 0))),
            pl.BlockSpec((tm, GATE_LANES), lambda i, j: (i, 0)),
        ],
        out_shape=[
            jax.ShapeDtypeStruct((T, n_cols - n_conv), BF16),
            jax.ShapeDtypeStruct((T, n_conv), F32),
            jax.ShapeDtypeStruct((T, GATE_LANES), F32),
        ],
        scratch_shapes=[pltpu.VMEM((tm, D_MODEL), BF16)],
        compiler_params=_cparams(("parallel", "arbitrary")),
        name="inproj",
    )(x2, norm_w, w_main, w_gate)


def _split3(v):
    hi = v.astype(BF16)
    r1 = v - hi.astype(F32)
    mid = r1.astype(BF16)
    lo = (r1 - mid.astype(F32)).astype(BF16)
    return hi, mid, lo


def _fox_norms_body(q_ref, k_ref, grp_ref, qn_ref, kn_ref):
    for src, dst in ((q_ref, qn_ref), (k_ref, kn_ref)):
        x = src[...].astype(F32)
        ss = jnp.dot((x * x).astype(BF16), grp_ref[...], preferred_element_type=F32)
        dst[...] = jnp.sqrt(ss) * NORM_MARGIN


def _head_group_matrix():
    g = np.zeros((FOX_WIDTH, GATE_LANES), np.float32)
    for h in range(FOX_HEADS):
        g[h * FOX_HEAD_DIM:(h + 1) * FOX_HEAD_DIM, h] = 1.0
    return g


def _fox_norms(z, grp, *, rows=1024):
    T = z.shape[0]
    out = jax.ShapeDtypeStruct((T, GATE_LANES), F32)
    return pl.pallas_call(
        _fox_norms_body,
        grid=(T // rows,),
        in_specs=[
            pl.BlockSpec((rows, FOX_WIDTH), lambda i: (i, 0)),
            pl.BlockSpec((rows, FOX_WIDTH), lambda i: (i, 1)),
            pl.BlockSpec((FOX_WIDTH, GATE_LANES), lambda i: (0, 0)),
        ],
        out_specs=[pl.BlockSpec((rows, GATE_LANES), lambda i: (i, 0))] * 2,
        out_shape=[out, out],
        compiler_params=_cparams(("parallel",)),
        name="fox_norms",
    )(z, z, grp)


def _gates_body(g_ref, bias_ref, tri_ref, pq_ref, pk_ref, qn_ref, kmax_ref, gc_ref, gr_ref, qx_ref, kx_ref,
                carry_scr, *, rows):
    c = pl.program_id(1)

    @pl.when(c == 0)
    def _():
        carry_scr[...] = jnp.zeros_like(carry_scr)

    lane = lax.broadcasted_iota(jnp.int32, (LANES, GATE_LANES), 1)
    is_glob = lane < FOX_HEADS
    is_ls = is_glob | ((lane >= FOX_HEADS + MLSTM_HEADS) & (lane < FOX_HEADS + 2 * MLSTM_HEADS))
    tri = tri_ref[...]
    for s in range(rows // LANES):
        sl = slice(s * LANES, (s + 1) * LANES)
        g = g_ref[sl, :] + bias_ref[...]
        ls = jnp.minimum(g, 0.0) - jnp.log1p(jnp.exp(-jnp.abs(g)))
        v = jnp.where(is_ls, ls, 0.0)
        hi, mid, lo = _split3(v)
        cs = (jnp.dot(tri, hi, preferred_element_type=F32)
              + jnp.dot(tri, mid, preferred_element_type=F32)
              + jnp.dot(tri, lo, preferred_element_type=F32))
        glob = cs + carry_scr[...]
        carry_scr[...] = glob[LANES - 1:LANES, :]
        out = jnp.where(is_glob, glob, jnp.where(is_ls, cs, g))
        gc_ref[sl, :] = out
        gr_ref[:, sl] = out.T[0:GATE_ROWS, :]
        fhi, fmid, flo = _split3(glob * LOG2E)
        mhi, mmid, mlo = _split3(qn_ref[sl, :] * kmax_ref[0:1, :] + 1.0)
        pieces = jnp.concatenate([fhi, fmid, flo, jnp.ones((LANES, GATE_LANES), BF16), mhi, mmid, mlo], axis=1)
        qx_ref[sl, :] = jnp.dot(pieces, pq_ref[...], preferred_element_type=F32).astype(BF16)
        kx_ref[sl, :] = jnp.dot(pieces, pk_ref[...], preferred_element_type=F32).astype(BF16)


_N_PIECES = 3
_PIECE_GROUPS = 2 * _N_PIECES + 1
_STAB_SLOT = 2 * _N_PIECES


def _placement_matrices():
    n = _N_PIECES
    pq = np.zeros((_PIECE_GROUPS * GATE_LANES, FOX_WIDTH), np.float32)
    pk = np.zeros((_PIECE_GROUPS * GATE_LANES, FOX_WIDTH), np.float32)
    for h in range(FOX_HEADS):
        for p in range(n):
            pq[p * GATE_LANES + h, h * FOX_HEAD_DIM + p] = 1.0
            pk[n * GATE_LANES + h, h * FOX_HEAD_DIM + p] = 1.0
            pq[n * GATE_LANES + h, h * FOX_HEAD_DIM + n + p] = 1.0
            pk[p * GATE_LANES + h, h * FOX_HEAD_DIM + n + p] = -1.0
            pq[(n + 1 + p) * GATE_LANES + h, h * FOX_HEAD_DIM + _STAB_SLOT + p] = -1.0
            pk[n * GATE_LANES + h, h * FOX_HEAD_DIM + _STAB_SLOT + p] = 1.0
    return pq, pk


def _gates(g, bias, tri, pq, pk, qn, kmax, *, batch, seq, rows=1024):
    T = g.shape[0]
    nblk = seq // rows
    return pl.pallas_call(
        functools.partial(_gates_body, rows=rows),
        grid=(batch, nblk),
        in_specs=[
            pl.BlockSpec((rows, GATE_LANES), lambda b, c: (b * nblk + c, 0)),
            pl.BlockSpec((1, GATE_LANES), lambda b, c: (0, 0)),
            pl.BlockSpec((LANES, LANES), lambda b, c: (0, 0)),
            pl.BlockSpec((_PIECE_GROUPS * GATE_LANES, FOX_WIDTH), lambda b, c: (0, 0)),
            pl.BlockSpec((_PIECE_GROUPS * GATE_LANES, FOX_WIDTH), lambda b, c: (0, 0)),
            pl.BlockSpec((rows, GATE_LANES), lambda b, c: (b * nblk + c, 0)),
            pl.BlockSpec((SUBLANES, GATE_LANES), lambda b, c: (b, 0)),
        ],
        out_specs=[
            pl.BlockSpec((rows, GATE_LANES), lambda b, c: (b * nblk + c, 0)),
            pl.BlockSpec((GATE_ROWS, rows), lambda b, c: (0, b * nblk + c)),
            pl.BlockSpec((rows, FOX_WIDTH), lambda b, c: (b * nblk + c, 0)),
            pl.BlockSpec((rows, FOX_WIDTH), lambda b, c: (b * nblk + c, 0)),
        ],
        out_shape=[
            jax.ShapeDtypeStruct((T, GATE_LANES), F32),
            jax.ShapeDtypeStruct((GATE_ROWS, T), F32),
            jax.ShapeDtypeStruct((T, FOX_WIDTH), BF16),
            jax.ShapeDtypeStruct((T, FOX_WIDTH), BF16),
        ],
        scratch_shapes=[pltpu.VMEM((1, GATE_LANES), F32)],
        compiler_params=_cparams(("parallel", "arbitrary")),
        name="gates",
    )(g, bias, tri, pq, pk, qn, kmax)


def _fox_body(lo_ref, fast_ref, q_ref, qx_ref, k_ref, kx_ref, v_ref, nw_ref, o_ref, *, tk, nq):
    b = pl.program_id(0)
    h = pl.program_id(1)
    qi = pl.program_id(2)
    tile = (b * FOX_HEADS + h) * nq + qi
    lo = lo_ref[tile]
    row = lax.broadcasted_iota(jnp.int32, (tk, tk), 0)
    col = lax.broadcasted_iota(jnp.int32, (tk, tk), 1)

    def load_kv(ki):
        start = pl.multiple_of(ki * tk, tk)
        k = jnp.concatenate([k_ref[pl.ds(start, tk), :], kx_ref[pl.ds(start, tk), :]], axis=1)
        return k, v_ref[pl.ds(start, tk), :]

    def logits(q, k, masked):
        s = lax.dot_general(q, k, (((1,), (1,)), ((), ())), preferred_element_type=F32)
        return jnp.where(row >= col, s, -jnp.inf) if masked else s

    def finish(r, num, den):
        out = num / den
        ms = jnp.mean(out * out, axis=-1, keepdims=True)
        o_ref[r * tk:(r + 1) * tk, :] = ((out * lax.rsqrt(ms + NORM_EPS)) * nw_ref[pl.ds(h, 1), :]).astype(BF16)

    @pl.when(fast_ref[tile] == 1)
    def _():
        q_halves = [jnp.concatenate([q_ref[r * tk:(r + 1) * tk, :], qx_ref[r * tk:(r + 1) * tk, :]], axis=1)
                    for r in range(2)]
        ones_col = (lax.broadcasted_iota(jnp.int32, (tk, LANES), 1) == 0).astype(BF16)

        def weights(r, k, masked):
            return jnp.exp2(logits(q_halves[r], k, masked)).astype(BF16)

        def values(ki):
            return jnp.concatenate([load_kv(ki)[1], ones_col], axis=1)

        def add(acc, p, va):
            return acc + jnp.dot(p, va, preferred_element_type=F32)

        def pair(kp, accs):
            blocks = [2 * kp, 2 * kp + 1]
            ps = [[weights(r, load_kv(ki)[0], False) for r in range(2)] for ki in blocks]
            for j, ki in enumerate(blocks):
                va = values(ki)
                accs = tuple(add(accs[r], ps[j][r], va) for r in range(2))
            return accs

        zero = jnp.zeros((tk, 2 * FOX_HEAD_DIM), F32)
        aa, ab = lax.fori_loop(lo // 2, qi, pair, (zero, zero))
        k = load_kv(2 * qi)[0]
        pa, pb = weights(0, k, True), weights(1, k, False)
        pb2 = weights(1, load_kv(2 * qi + 1)[0], True)
        va = values(2 * qi)
        aa, ab = add(aa, pa, va), add(ab, pb, va)
        ab = add(ab, pb2, values(2 * qi + 1))
        for r, acc in enumerate((aa, ab)):
            finish(r, acc[:, 0:FOX_HEAD_DIM], acc[:, FOX_HEAD_DIM:FOX_HEAD_DIM + 1])

    @pl.when(fast_ref[tile] == 0)
    def _():
        _fox_online(q_ref, qx_ref, load_kv, logits, finish, lo, qi, tk)


def _fox_online(q_ref, qx_ref, load_kv, logits, finish, lo, qi, tk):
    lane = lax.broadcasted_iota(jnp.int32, (tk, LANES), 1)
    no_stab = (lane < _STAB_SLOT) | (lane >= _STAB_SLOT + _N_PIECES)
    q_halves = [jnp.concatenate([q_ref[r * tk:(r + 1) * tk, :],
                                 jnp.where(no_stab, qx_ref[r * tk:(r + 1) * tk, :], jnp.zeros((), BF16))], axis=1)
                for r in range(2)]

    def weights(s, carry):
        m, l, acc = carry
        cols = [s[:, c * LANES:(c + 1) * LANES] for c in range(tk // LANES)]
        m_new = jnp.maximum(m, jnp.max(functools.reduce(jnp.maximum, cols), axis=-1, keepdims=True))
        alpha = jnp.exp2(m - m_new)
        ps = [jnp.exp2(c - m_new) for c in cols]
        l = alpha * l + functools.reduce(jnp.add, ps)
        return m_new, l, alpha * acc, jnp.concatenate(ps, axis=1).astype(BF16)

    def update(q, k, v, carry, masked):
        m, l, acc, p = weights(logits(q, k, masked), carry)
        return m, l, acc + jnp.dot(p, v, preferred_element_type=F32)

    def both(ki, carries):
        k, v = load_kv(ki)
        s = [logits(q_halves[r], k, False) for r in range(2)]
        w = [weights(s[r], carries[r]) for r in range(2)]
        return tuple((m, l, acc + jnp.dot(p, v, preferred_element_type=F32)) for m, l, acc, p in w)

    init = (jnp.full((tk, LANES), -jnp.inf, F32), jnp.zeros((tk, LANES), F32), jnp.zeros((tk, FOX_HEAD_DIM), F32))
    ca, cb = lax.fori_loop(lo, 2 * qi, both, (init, init))
    k, v = load_kv(2 * qi)
    ca = update(q_halves[0], k, v, ca, True)
    cb = update(q_halves[1], k, v, cb, False)
    k, v = load_kv(2 * qi + 1)
    cb = update(q_halves[1], k, v, cb, True)
    for r, (_, l, acc) in enumerate((ca, cb)):
        finish(r, acc, jnp.sum(l, axis=-1, keepdims=True))


def _fox_plan(qn, kmax, gc, *, batch, seq, tk):
    tq = 2 * tk
    nq = seq // tq
    nk = seq // tk
    heads = slice(0, FOX_HEADS)
    qmax = jnp.max(qn.reshape(batch, nq, tq, GATE_LANES), axis=2)[..., heads]
    gap = 2.0 * qmax * kmax[:, None, heads] + BOUND_SLACK_LOG2
    f2 = (gc[:, heads] * LOG2E).reshape(batch, nk, tk, FOX_HEADS)
    f_first = f2[:, ::2, 0, :]
    f_last = f2[:, :, tk - 1, :]
    bound = gap[:, :, None, :] + f_first[:, :, None, :] - f_last[:, None, :, :]
    below_diag = jnp.arange(nk)[None, :, None] < 2 * jnp.arange(nq)[:, None, None]
    skip = (bound < -F32_ZERO_LOG2) & below_diag[None]
    lo = jnp.sum(jnp.cumprod(skip.astype(jnp.int32), axis=2), axis=2)
    fast = (gap <= FAST_GAP_LOG2).astype(jnp.int32)
    flat = lambda a: jnp.transpose(a, (0, 2, 1)).reshape(-1)
    return flat(lo), flat(fast)


def _fox(z, qx, kx, nw, lo, fast, *, batch, seq, tk=512):
    T = z.shape[0]
    tq = 2 * tk
    nq = seq // tq
    kcol = FOX_WIDTH // FOX_HEAD_DIM
    grid_spec = pltpu.PrefetchScalarGridSpec(
        num_scalar_prefetch=2,
        grid=(batch, FOX_HEADS, nq),
        in_specs=[
            pl.BlockSpec((tq, FOX_HEAD_DIM), lambda b, h, i, lo, fast: (b * nq + i, h)),
            pl.BlockSpec((tq, FOX_HEAD_DIM), lambda b, h, i, lo, fast: (b * nq + i, h)),
            pl.BlockSpec((seq, FOX_HEAD_DIM), lambda b, h, i, lo, fast: (b, kcol + h)),
            pl.BlockSpec((seq, FOX_HEAD_DIM), lambda b, h, i, lo, fast: (b, h)),
            pl.BlockSpec((seq, FOX_HEAD_DIM), lambda b, h, i, lo, fast: (b, 2 * kcol + h)),
            pl.BlockSpec((FOX_HEADS, FOX_HEAD_DIM), lambda b, h, i, lo, fast: (0, 0)),
        ],
        out_specs=pl.BlockSpec((tq, FOX_HEAD_DIM), lambda b, h, i, lo, fast: (b * nq + i, h)),
    )
    return pl.pallas_call(
        functools.partial(_fox_body, tk=tk, nq=nq),
        grid_spec=grid_spec,
        out_shape=jax.ShapeDtypeStruct((T, FOX_WIDTH), BF16),
        compiler_params=_cparams(("parallel", "parallel", "arbitrary")),
        name="fox",
    )(lo, fast, z, qx, z, kx, z, nw)


def _mlstm_body(zc_ref, zprev_ref, v_ref, o_ref, gc_ref, gr_ref, cw_ref, cb_ref, nw_ref, out_ref,
                full_scr, c_scr, n_scr, m_scr):
    c = pl.program_id(1)
    L = MLSTM_CHUNK
    dk = MLSTM_QK_DIM
    dv = MLSTM_V_DIM

    @pl.when(c == 0)
    def _():
        c_scr[...] = jnp.zeros_like(c_scr)
        n_scr[...] = jnp.zeros_like(n_scr)
        m_scr[...] = jnp.zeros_like(m_scr)

    full_scr[0:SUBLANES, :] = jnp.where(c == 0, 0.0, zprev_ref[...])
    full_scr[SUBLANES:SUBLANES + L, :] = zc_ref[...]
    y = cb_ref[...]
    for j in range(MLSTM_CONV):
        y = y + cw_ref[j:j + 1, :] * full_scr[pl.ds(SUBLANES - (MLSTM_CONV - 1) + j, L), :]
    qk_all = y * jax.nn.sigmoid(y)

    gcb = gc_ref[...]
    grb = gr_ref[...]
    row = lax.broadcasted_iota(jnp.int32, (L, L), 0)
    col = lax.broadcasted_iota(jnp.int32, (L, L), 1)
    causal = row >= col
    for hh in range(MLSTM_HEADS):
        qh = qk_all[:, hh * dk:(hh + 1) * dk]
        kh = qk_all[:, MLSTM_QK_WIDTH + hh * dk:MLSTM_QK_WIDTH + (hh + 1) * dk] * (dk ** -0.5)
        vh = v_ref[:, hh * dv:(hh + 1) * dv]
        oh = o_ref[:, hh * dv:(hh + 1) * dv].astype(F32)
        li = FOX_HEADS + hh
        lf = FOX_HEADS + MLSTM_HEADS + hh
        i_col = gcb[:, li:li + 1]
        b_col = gcb[:, lf:lf + 1]
        i_row = grb[li:li + 1, :]
        b_row = grb[lf:lf + 1, :]
        b_last = b_row[:, L - 1:L]

        c_prev = c_scr[hh]
        n_prev = n_scr[hh]
        m_prev = m_scr[hh][:, 0:1]

        m_loc = jnp.max(b_last - b_row + i_row, axis=-1, keepdims=True)
        a_end = jnp.exp(b_last - b_col + i_col - m_loc)
        kw = kh * a_end
        c_loc = lax.dot_general(kw.astype(BF16), vh, (((0,), (0,)), ((), ())), preferred_element_type=F32)
        n_loc = jnp.sum(kw, axis=0, keepdims=True)

        g_col = b_col + m_prev
        dmat = jnp.where(causal, b_col - b_row + i_row, -jnp.inf)
        m_t = jnp.maximum(g_col, jnp.max(dmat, axis=-1, keepdims=True))
        qb = qh.astype(BF16)
        qk = lax.dot_general(qb, kh.astype(BF16), (((1,), (1,)), ((), ())), preferred_element_type=F32)
        sm = qk * jnp.exp(dmat - m_t)
        inter = jnp.exp(g_col - m_t)
        num = (jnp.dot(sm.astype(BF16), vh, preferred_element_type=F32)
               + inter * jnp.dot(qb, c_prev.astype(BF16), preferred_element_type=F32))
        den = jnp.sum(sm, axis=-1, keepdims=True) + inter * jnp.sum(qh * n_prev, axis=-1, keepdims=True)
        cell = num / jnp.maximum(jnp.abs(den), jnp.exp(-m_t))
        gated = jax.nn.sigmoid(oh) * cell
        ms = jnp.mean(gated * gated, axis=-1, keepdims=True)
        out_ref[:, hh * dv:(hh + 1) * dv] = (
            (gated * lax.rsqrt(ms + NORM_EPS)) * nw_ref[:, hh * dv:(hh + 1) * dv]).astype(BF16)

        m_new = jnp.maximum(b_last + m_prev, m_loc)
        a_prev = jnp.exp(b_last + m_prev - m_new)
        a_loc = jnp.exp(m_loc - m_new)
        c_scr[hh] = a_prev * c_prev + a_loc * c_loc
        n_scr[hh] = a_prev * n_prev + a_loc * n_loc
        m_scr[hh] = jnp.broadcast_to(m_new, (1, LANES))


def _mlstm(zc, z, gc, gr, conv_w, conv_b, nw, *, batch, seq):
    T = zc.shape[0]
    L = MLSTM_CHUNK
    nc = seq // L
    per = L // SUBLANES
    vcol = 3 * FOX_WIDTH // MLSTM_V_WIDTH
    return pl.pallas_call(
        _mlstm_body,
        grid=(batch, nc),
        in_specs=[
            pl.BlockSpec((L, 2 * MLSTM_QK_WIDTH), lambda b, c: (b * nc + c, 0)),
            pl.BlockSpec((SUBLANES, 2 * MLSTM_QK_WIDTH), lambda b, c: (jnp.maximum((b * nc + c) * per - 1, 0), 0)),
            pl.BlockSpec((L, MLSTM_V_WIDTH), lambda b, c: (b * nc + c, vcol)),
            pl.BlockSpec((L, MLSTM_V_WIDTH), lambda b, c: (b * nc + c, vcol + 1)),
            pl.BlockSpec((L, GATE_LANES), lambda b, c: (b * nc + c, 0)),
            pl.BlockSpec((GATE_ROWS, L), lambda b, c: (0, b * nc + c)),
            pl.BlockSpec((MLSTM_CONV, 2 * MLSTM_QK_WIDTH), lambda b, c: (0, 0)),
            pl.BlockSpec((1, 2 * MLSTM_QK_WIDTH), lambda b, c: (0, 0)),
            pl.BlockSpec((1, MLSTM_V_WIDTH), lambda b, c: (0, 0)),
        ],
        out_specs=pl.BlockSpec((L, MLSTM_V_WIDTH), lambda b, c: (b * nc + c, 0)),
        out_shape=jax.ShapeDtypeStruct((T, MLSTM_V_WIDTH), BF16),
        scratch_shapes=[
            pltpu.VMEM((SUBLANES + L, 2 * MLSTM_QK_WIDTH), F32),
            pltpu.VMEM((MLSTM_HEADS, MLSTM_QK_DIM, MLSTM_V_DIM), F32),
            pltpu.VMEM((MLSTM_HEADS, 1, MLSTM_QK_DIM), F32),
            pltpu.VMEM((MLSTM_HEADS, 1, LANES), F32),
        ],
        compiler_params=_cparams(("parallel", "arbitrary")),
        name="mlstm",
    )(zc, zc, z, z, gc, gr, conv_w, conv_b, nw)


def _outproj_body(att_ref, cell_ref, x_ref, wa_ref, wb_ref, n2_ref, x1_ref, h2t_ref):
    y = (jnp.dot(att_ref[...], wa_ref[...], preferred_element_type=F32)
         + jnp.dot(cell_ref[...], wb_ref[...], preferred_element_type=F32))
    x1 = x_ref[...] + y
    x1_ref[...] = x1
    ms = jnp.mean(x1 * x1, axis=-1, keepdims=True)
    h2 = (x1 * lax.rsqrt(ms + NORM_EPS)) * n2_ref[...]
    h2t_ref[...] = h2.T.astype(BF16)


def _outproj(att, cell, x2, wa, wb, n2, *, tm=512):
    T = x2.shape[0]
    return pl.pallas_call(
        _outproj_body,
        grid=(T // tm,),
        in_specs=[
            pl.BlockSpec((tm, FOX_WIDTH), lambda i: (i, 0)),
            pl.BlockSpec((tm, MLSTM_V_WIDTH), lambda i: (i, 0)),
            pl.BlockSpec((tm, D_MODEL), lambda i: (i, 0)),
            pl.BlockSpec((FOX_WIDTH, D_MODEL), lambda i: (0, 0)),
            pl.BlockSpec((MLSTM_V_WIDTH, D_MODEL), lambda i: (0, 0)),
            pl.BlockSpec((1, D_MODEL), lambda i: (0, 0)),
        ],
        out_specs=[
            pl.BlockSpec((tm, D_MODEL), lambda i: (i, 0)),
            pl.BlockSpec((D_MODEL, tm), lambda i: (0, i)),
        ],
        out_shape=[
            jax.ShapeDtypeStruct((T, D_MODEL), F32),
            jax.ShapeDtypeStruct((D_MODEL, T), BF16),
        ],
        compiler_params=_cparams(("parallel",)),
        name="outproj",
    )(att, cell, x2, wa, wb, n2)


def _row_iota(n_rows, tb):
    return lax.broadcasted_iota(jnp.int32, (n_rows, tb), 0).astype(F32)


def _extract16(s, val_ref, idx_ref):
    n_rows, tb = s.shape
    iota = _row_iota(n_rows, tb)

    def body(r, sw):
        mx = jnp.max(sw, axis=0, keepdims=True)
        first = jnp.min(jnp.where(sw == mx, iota, float(n_rows)), axis=0, keepdims=True)
        val_ref[pl.ds(r, 1), :] = mx
        idx_ref[pl.ds(r, 1), :] = first
        return jnp.where(iota == first, -jnp.inf, sw)

    lax.fori_loop(0, PEER_TOPK, body, s)


def _scatter_rounds(idx, values, n_rows, fill):
    tb = idx.shape[1]
    iota = _row_iota(n_rows, tb)
    table = jnp.full((n_rows, tb), fill, F32)
    for r in range(PEER_TOPK):
        v = values[r:r + 1, :] if values is not None else float(r)
        table = jnp.where(iota == idx[r:r + 1, :], v, table)
    return table


def _extract16_distinct(s, val_ref):
    def body(r, sw):
        mx = jnp.max(sw, axis=0, keepdims=True)
        val_ref[pl.ds(r, 1), :] = mx
        return jnp.where(sw == mx, -jnp.inf, sw)

    lax.fori_loop(0, PEER_TOPK, body, s)


def _count_ge(s, thr):
    return jnp.sum((s >= thr).astype(F32), axis=0, keepdims=True)


def _match_rounds(s, vals, values, fill):
    table = jnp.full(s.shape, fill, F32)
    for r in range(PEER_TOPK):
        v = values[r:r + 1, :] if values is not None else float(r)
        table = jnp.where(s == vals[r:r + 1, :], v, table)
    return table


def _peer_sel_body(h2t_ref, wqt_ref, keys_ref, grp_ref, e1_ref, c1_ref, n0_ref, w0_ref,
                   qt_scr, vals_scr, idx_scr, cand_scr, pe_scr, *, tb):
    half = PEER_KEY_DIM // 2
    qt_scr[...] = jnp.dot(wqt_ref[...], h2t_ref[...], preferred_element_type=F32).astype(BF16)

    def head_tables(h, exact):
        s0 = jnp.dot(keys_ref[0], qt_scr[(2 * h) * half:(2 * h + 1) * half, :], preferred_element_type=F32)
        s1 = jnp.dot(keys_ref[1], qt_scr[(2 * h + 1) * half:(2 * h + 2) * half, :], preferred_element_type=F32)
        if exact:
            _extract16(s0, vals_scr.at[0], idx_scr.at[0])
            _extract16(s1, vals_scr.at[1], idx_scr.at[1])
        else:
            _extract16_distinct(s0, vals_scr.at[0])
            _extract16_distinct(s1, vals_scr.at[1])
        a = vals_scr[0]
        b = vals_scr[1]
        ea = jnp.exp(a - a[0:1, :])
        eb = jnp.exp(b - b[0:1, :])
        off = 0
        for r in range(PEER_TOPK):
            ncol = _CAND_COLS[r]
            cand_scr[off:off + ncol, :] = a[r:r + 1, :] + b[0:ncol, :]
            pe_scr[off:off + ncol, :] = ea[r:r + 1, :] * eb[0:ncol, :]
            off += ncol
        cand_scr[_N_CAND:_N_CAND_PAD, :] = jnp.full((_N_CAND_PAD - _N_CAND, tb), -jnp.inf, F32)
        pe_scr[_N_CAND:_N_CAND_PAD, :] = jnp.zeros((_N_CAND_PAD - _N_CAND, tb), F32)
        cand = cand_scr[...]
        last = PEER_TOPK - 1
        if exact:
            _extract16(cand, vals_scr.at[2], idx_scr.at[2])
            sel = _scatter_rounds(idx_scr[2], jnp.ones((PEER_TOPK, tb), F32), _N_CAND_PAD, 0.0)
            ties = None
        else:
            _extract16_distinct(cand, vals_scr.at[2])
            sel = (cand >= vals_scr[2, last:last + 1, :]).astype(F32)
            ties = ((_count_ge(s0, a[last:last + 1, :]) != float(PEER_TOPK)).astype(F32)
                    + (_count_ge(s1, b[last:last + 1, :]) != float(PEER_TOPK)).astype(F32)
                    + (jnp.sum(sel, axis=0, keepdims=True) != float(PEER_TOPK)).astype(F32))
        zsum = jnp.sum(sel * pe_scr[...], axis=0, keepdims=True)
        nr = jnp.dot(grp_ref[...], sel.astype(BF16), preferred_element_type=F32)
        if exact:
            n0 = _scatter_rounds(idx_scr[0], nr, PEER_N_KEYS, 0.0)
            c1 = _scatter_rounds(idx_scr[1], None, PEER_N_KEYS, float(PEER_TOPK))
        else:
            n0 = _match_rounds(s0, a, nr, 0.0)
            c1 = _match_rounds(s1, b, None, float(PEER_TOPK))
        n0_ref[h] = n0
        w0_ref[h] = jnp.exp(s0 - a[0:1, :]) * (0.5 / zsum)
        e1_ref[h] = jnp.exp(s1 - b[0:1, :]).astype(BF16)
        c1_ref[h] = c1.astype(BF16)
        return ties

    ties = functools.reduce(jnp.add, [head_tables(h, False) for h in range(PEER_HEADS)])

    @pl.when(jnp.max(ties) > 0.0)
    def _():
        for h in range(PEER_HEADS):
            head_tables(h, True)


def _peer_sel(h2t, wqt, keys, grp, *, tb=256):
    T = h2t.shape[1]
    tab = jax.ShapeDtypeStruct((PEER_HEADS, PEER_N_KEYS, T), F32)
    tab16 = jax.ShapeDtypeStruct((PEER_HEADS, PEER_N_KEYS, T), BF16)
    tab_spec = pl.BlockSpec((PEER_HEADS, PEER_N_KEYS, tb), lambda i: (0, 0, i))
    return pl.pallas_call(
        functools.partial(_peer_sel_body, tb=tb),
        grid=(T // tb,),
        in_specs=[
            pl.BlockSpec((D_MODEL, tb), lambda i: (0, i)),
            pl.BlockSpec((PEER_HEADS * PEER_KEY_DIM, D_MODEL), lambda i: (0, 0)),
            pl.BlockSpec((2, PEER_N_KEYS, PEER_KEY_DIM // 2), lambda i: (0, 0, 0)),
            pl.BlockSpec((PEER_TOPK, _N_CAND_PAD), lambda i: (0, 0)),
        ],
        out_specs=[tab_spec, tab_spec, tab_spec, tab_spec],
        out_shape=[tab16, tab16, tab, tab],
        scratch_shapes=[
            pltpu.VMEM((PEER_HEADS * PEER_KEY_DIM, tb), BF16),
            pltpu.VMEM((3, PEER_TOPK, tb), F32),
            pltpu.VMEM((3, PEER_TOPK, tb), F32),
            pltpu.VMEM((_N_CAND_PAD, tb), F32),
            pltpu.VMEM((_N_CAND_PAD, tb), F32),
        ],
        compiler_params=_cparams(("parallel",)),
        name="peer_sel",
    )(h2t, wqt, keys, grp)


def _peer_dense_body(u_ref, vt_ref, h2t_ref, e1_ref, c1_ref, n0_ref, w0_ref, yt_ref, st_a, st_b, *, ec, tb, nk):
    k = pl.program_id(1)
    slabs = ec // PEER_N_KEYS

    def pre_activations(st_w):
        st_w[...] = jnp.dot(u_ref[...], h2t_ref[...], preferred_element_type=F32)

    def activate_and_project(st_r):
        parts = []
        for ii in range(slabs):
            g = jnp.zeros((PEER_N_KEYS, tb), BF16)
            for h in range(PEER_HEADS):
                n_row = n0_ref[h, ii:ii + 1, :].astype(BF16)
                w_row = w0_ref[h, ii:ii + 1, :].astype(BF16)
                g = g + jnp.where(c1_ref[h] < n_row, e1_ref[h] * w_row, jnp.zeros((), BF16))
            x = st_r[ii * PEER_N_KEYS:(ii + 1) * PEER_N_KEYS, :]
            act = x * (1.0 + lax.erf(x * math.sqrt(0.5)))
            parts.append(act.astype(BF16) * g)
        at = jnp.concatenate(parts, axis=0)
        yt_ref[...] += jnp.dot(vt_ref[...], at, preferred_element_type=F32)

    @pl.when(k == 0)
    def _():
        yt_ref[...] = jnp.zeros_like(yt_ref)
        pre_activations(st_a)

    @pl.when((k > 0) & (k < nk) & (k % 2 == 1))
    def _():
        pre_activations(st_b)
        activate_and_project(st_a)

    @pl.when((k > 0) & (k < nk) & (k % 2 == 0))
    def _():
        pre_activations(st_a)
        activate_and_project(st_b)

    @pl.when(k == nk)
    def _():
        activate_and_project(st_b if nk % 2 == 0 else st_a)


def _peer_dense(u, vt, h2t, e1, c1, n0, w0, *, tb=512, ec=1024):
    T = h2t.shape[1]
    nk = PEER_N_EXPERTS // ec
    slabs = ec // PEER_N_KEYS
    assert slabs == SUBLANES, "one f32 sublane tile of per-slab gate rows per expert chunk"
    tab_spec = pl.BlockSpec((PEER_HEADS, PEER_N_KEYS, tb), lambda i, k: (0, 0, i))
    row_spec = pl.BlockSpec((PEER_HEADS, slabs, tb), lambda i, k: (0, jnp.maximum(k - 1, 0), i))
    return pl.pallas_call(
        functools.partial(_peer_dense_body, ec=ec, tb=tb, nk=nk),
        grid=(T // tb, nk + 1),
        in_specs=[
            pl.BlockSpec((ec, D_MODEL), lambda i, k: (jnp.minimum(k, nk - 1), 0)),
            pl.BlockSpec((D_MODEL, ec), lambda i, k: (0, jnp.maximum(k - 1, 0))),
            pl.BlockSpec((D_MODEL, tb), lambda i, k: (0, i)),
            tab_spec, tab_spec, row_spec, row_spec,
        ],
        out_specs=pl.BlockSpec((D_MODEL, tb), lambda i, k: (0, i)),
        out_shape=jax.ShapeDtypeStruct((D_MODEL, T), F32),
        scratch_shapes=[pltpu.VMEM((ec, tb), F32), pltpu.VMEM((ec, tb), F32)],
        compiler_params=_cparams(("parallel", "arbitrary")),
        name="peer_dense",
    )(u, vt, h2t, e1, c1, n0, w0)


def _final_body(x1_ref, yt_ref, w_ref, o_ref):
    x2 = x1_ref[...] + yt_ref[...].T
    ms = jnp.mean(x2 * x2, axis=-1, keepdims=True)
    o_ref[...] = (x2 * lax.rsqrt(ms + NORM_EPS)) * w_ref[...]


def _final(x1, yt, w, *, tm=512):
    T = x1.shape[0]
    return pl.pallas_call(
        _final_body,
        grid=(T // tm,),
        in_specs=[
            pl.BlockSpec((tm, D_MODEL), lambda i: (i, 0)),
            pl.BlockSpec((D_MODEL, tm), lambda i: (0, i)),
            pl.BlockSpec((1, D_MODEL), lambda i: (0, 0)),
        ],
        out_specs=pl.BlockSpec((tm, D_MODEL), lambda i: (i, 0)),
        out_shape=jax.ShapeDtypeStruct((T, D_MODEL), F32),
        compiler_params=_cparams(("parallel",)),
        name="final",
    )(x1, yt, w)


def _group_matrix():
    g = np.zeros((PEER_TOPK, _N_CAND_PAD), np.float32)
    off = 0
    for r, ncol in enumerate(_CAND_COLS):
        g[r, off:off + ncol] = 1.0
        off += ncol
    return g


def _layer(x2, norm1_w, w_in, fox_f_bias, conv_w, conv_b, i_bias, f_bias, fox_nw, mlstm_nw, w_out, norm2_w,
           w_q, keys, u, v, *, batch, seq):
    splits = np.cumsum((FOX_WIDTH, FOX_WIDTH, FOX_WIDTH, FOX_HEADS, MLSTM_QK_WIDTH, MLSTM_QK_WIDTH,
                        MLSTM_V_WIDTH, MLSTM_HEADS, MLSTM_HEADS, MLSTM_V_WIDTH))[:-1]
    fq, fk, fv, ff, mq, mk, mv, mi, mf, mo = jnp.split(w_in, [int(p) for p in splits], axis=1)
    w_main = jnp.concatenate([fq, fk, fv, mv, mo, mq, mk], axis=1).astype(BF16)
    gate_pad = GATE_LANES - FOX_HEADS - 2 * MLSTM_HEADS
    w_gate = jnp.pad(jnp.concatenate([ff, mi, mf], axis=1), ((0, 0), (0, gate_pad))).astype(BF16)
    gate_bias = jnp.pad(jnp.concatenate([fox_f_bias, i_bias, f_bias]), (0, gate_pad)).reshape(1, GATE_LANES)
    tri = jnp.asarray(np.tril(np.ones((LANES, LANES), np.float32)), BF16)

    z, zc, g = _inproj(x2, norm1_w.reshape(1, D_MODEL), w_main, w_gate)
    pq, pk = _placement_matrices()
    qn, kn = _fox_norms(z, jnp.asarray(_head_group_matrix(), BF16))
    kmax = jnp.max(kn.reshape(batch, seq, GATE_LANES), axis=1)
    gc, gr, qx, kx = _gates(g, gate_bias, tri, jnp.asarray(pq, BF16), jnp.asarray(pk, BF16), qn,
                            jnp.repeat(kmax, SUBLANES, axis=0), batch=batch, seq=seq)
    lo, fast = _fox_plan(qn, kmax, gc, batch=batch, seq=seq, tk=FOX_KEY_BLOCK)
    att = _fox(z, qx, kx, fox_nw.reshape(FOX_HEADS, FOX_HEAD_DIM), lo, fast, batch=batch, seq=seq, tk=FOX_KEY_BLOCK)
    cell = _mlstm(zc, z, gc, gr, conv_w, conv_b.reshape(1, -1), mlstm_nw.reshape(1, -1), batch=batch, seq=seq)
    x1, h2t = _outproj(att, cell, x2, w_out[:FOX_WIDTH].astype(BF16), w_out[FOX_WIDTH:].astype(BF16),
                       norm2_w.reshape(1, D_MODEL))
    e1, c1, n0, w0 = _peer_sel(h2t, w_q.T.astype(BF16), keys.astype(BF16), jnp.asarray(_group_matrix(), BF16))
    yt = _peer_dense(u.astype(BF16), v.T.astype(BF16), h2t, e1, c1, n0, w0)
    return x1, yt


def kernel(x, norm1_w, w_in, fox_f_bias, mlstm_conv_w, mlstm_conv_b, mlstm_i_bias, mlstm_f_bias, fox_out_norm_w,
           mlstm_out_norm_w, w_out, norm2_w, peer_w_q, peer_keys, peer_u, peer_v, final_norm_w):
    batch, seq, _ = x.shape
    assert w_in.shape[0] == 1, "single-layer block: the final norm is fused with the last residual add"
    x2 = x.reshape(batch * seq, D_MODEL)
    x1, yt = _layer(x2, norm1_w[0], w_in[0], fox_f_bias[0], mlstm_conv_w[0], mlstm_conv_b[0], mlstm_i_bias[0],
                    mlstm_f_bias[0], fox_out_norm_w[0], mlstm_out_norm_w[0], w_out[0], norm2_w[0],
                    peer_w_q[0], peer_keys[0], peer_u[0], peer_v[0], batch=batch, seq=seq)
    out = _final(x1, yt, final_norm_w.reshape(1, D_MODEL))
    return out.reshape(batch, seq, D_MODEL)
```

```python
import functools
import math

import numpy as np
import jax
import jax.numpy as jnp
from jax import lax
from jax.experimental import pallas as pl
from jax.experimental.pallas import tpu as pltpu

F32 = jnp.float32
BF16 = jnp.bfloat16

D_MODEL = 2048
FOX_HEADS = 8
FOX_HEAD_DIM = 128
FOX_WIDTH = FOX_HEADS * FOX_HEAD_DIM
MLSTM_HEADS = 4
MLSTM_QK_DIM = 128
MLSTM_V_DIM = 256
MLSTM_QK_WIDTH = MLSTM_HEADS * MLSTM_QK_DIM
MLSTM_V_WIDTH = MLSTM_HEADS * MLSTM_V_DIM
MLSTM_CONV = 4
MLSTM_CHUNK = 128
PEER_HEADS = 8
PEER_KEY_DIM = 256
PEER_N_KEYS = 128
PEER_TOPK = 16
PEER_N_EXPERTS = PEER_N_KEYS * PEER_N_KEYS
NORM_EPS = 1e-6
LOG2E = math.log2(math.e)
NORM_MARGIN = 1.0 + 2.0 ** -7
F32_ZERO_LOG2 = 150.0
FAST_GAP_LOG2 = 90.0
BOUND_SLACK_LOG2 = 4.0
FOX_KEY_BLOCK = 512

LANES = 128
SUBLANES = 8
GATE_LANES = LANES
GATE_ROWS = 16
VMEM_LIMIT = 56 * 1024 * 1024

_CAND_COLS = tuple(PEER_TOPK // (r + 1) for r in range(PEER_TOPK))
_N_CAND = sum(_CAND_COLS)
_N_CAND_PAD = -(-_N_CAND // SUBLANES) * SUBLANES


def _cparams(sem):
    return pltpu.CompilerParams(dimension_semantics=sem, vmem_limit_bytes=VMEM_LIMIT)


def _inproj_body(x_ref, nw_ref, w_ref, wg_ref, z_ref, zc_ref, g_ref, h_scr, *, n_main, q_blocks, q_scale):
    j = pl.program_id(1)

    @pl.when(j == 0)
    def _():
        x = x_ref[...]
        ms = jnp.mean(x * x, axis=-1, keepdims=True)
        hb = ((x * lax.rsqrt(ms + NORM_EPS)) * nw_ref[...]).astype(BF16)
        h_scr[...] = hb
        g_ref[...] = jnp.dot(hb, wg_ref[...], preferred_element_type=F32)

    z = jnp.dot(h_scr[...], w_ref[...], preferred_element_type=F32)

    @pl.when(j < n_main)
    def _():
        scale = jnp.where(j < q_blocks, q_scale, 1.0).astype(F32)
        z_ref[...] = (z * scale).astype(BF16)

    @pl.when(j >= n_main)
    def _():
        zc_ref[...] = z


def _inproj(x2, norm_w, w_main, w_gate, *, tm=1024, tn=1024):
    T = x2.shape[0]
    n_cols = w_main.shape[1]
    n_conv = 2 * MLSTM_QK_WIDTH
    n_main = (n_cols - n_conv) // tn
    n_blocks = n_cols // tn
    body = functools.partial(_inproj_body, n_main=n_main, q_blocks=FOX_WIDTH // tn,
                             q_scale=FOX_HEAD_DIM ** -0.5 * LOG2E)
    return pl.pallas_call(
        body,
        grid=(T // tm, n_blocks),
        in_specs=[
            pl.BlockSpec((tm, D_MODEL), lambda i, j: (i, 0)),
            pl.BlockSpec((1, D_MODEL), lambda i, j: (0, 0)),
            pl.BlockSpec((D_MODEL, tn), lambda i, j: (0, j)),
            pl.BlockSpec((D_MODEL, GATE_LANES), lambda i, j: (0, 0)),
        ],
        out_specs=[
            pl.BlockSpec((tm, tn), lambda i, j: (i, jnp.minimum(j, n_main - 1))),
            pl.BlockSpec((tm, tn), lambda i, j: (i, jnp.maximum(j - n_main, 0))),
            pl.BlockSpec((tm, GATE_LANES), lambda i, j: (i, 0)),
        ],
        out_shape=[
            jax.ShapeDtypeStruct((T, n_cols - n_conv), BF16),
            jax.ShapeDtypeStruct((T, n_conv), F32),
            jax.ShapeDtypeStruct((T, GATE_LANES), F32),
        ],
        scratch_shapes=[pltpu.VMEM((tm, D_MODEL), BF16)],
        compiler_params=_cparams(("parallel", "arbitrary")),
        name="inproj",
    )(x2, norm_w, w_main, w_gate)


def _split3(v):
    hi = v.astype(BF16)
    r1 = v - hi.astype(F32)
    mid = r1.astype(BF16)
    lo = (r1 - mid.astype(F32)).astype(BF16)
    return hi, mid, lo


def _fox_norms_body(q_ref, k_ref, grp_ref, qn_ref, kn_ref):
    for src, dst in ((q_ref, qn_ref), (k_ref, kn_ref)):
        x = src[...].astype(F32)
        ss = jnp.dot((x * x).astype(BF16), grp_ref[...], preferred_element_type=F32)
        dst[...] = jnp.sqrt(ss) * NORM_MARGIN


def _head_group_matrix():
    g = np.zeros((FOX_WIDTH, GATE_LANES), np.float32)
    for h in range(FOX_HEADS):
        g[h * FOX_HEAD_DIM:(h + 1) * FOX_HEAD_DIM, h] = 1.0
    return g


def _fox_norms(z, grp, *, rows=1024):
    T = z.shape[0]
    out = jax.ShapeDtypeStruct((T, GATE_LANES), F32)
    return pl.pallas_call(
        _fox_norms_body,
        grid=(T // rows,),
        in_specs=[
            pl.BlockSpec((rows, FOX_WIDTH), lambda i: (i, 0)),
            pl.BlockSpec((rows, FOX_WIDTH), lambda i: (i, 1)),
            pl.BlockSpec((FOX_WIDTH, GATE_LANES), lambda i: (0, 0)),
        ],
        out_specs=[pl.BlockSpec((rows, GATE_LANES), lambda i: (i, 0))] * 2,
        out_shape=[out, out],
        compiler_params=_cparams(("parallel",)),
        name="fox_norms",
    )(z, z, grp)


def _gates_body(g_ref, bias_ref, tri_ref, pq_ref, pk_ref, qn_ref, kmax_ref, gc_ref, gr_ref, qx_ref, kx_ref,
                carry_scr, *, rows):
    c = pl.program_id(1)

    @pl.when(c == 0)
    def _():
        carry_scr[...] = jnp.zeros_like(carry_scr)

    lane = lax.broadcasted_iota(jnp.int32, (LANES, GATE_LANES), 1)
    is_glob = lane < FOX_HEADS
    is_ls = is_glob | ((lane >= FOX_HEADS + MLSTM_HEADS) & (lane < FOX_HEADS + 2 * MLSTM_HEADS))
    tri = tri_ref[...]
    for s in range(rows // LANES):
        sl = slice(s * LANES, (s + 1) * LANES)
        g = g_ref[sl, :] + bias_ref[...]
        ls = jnp.minimum(g, 0.0) - jnp.log1p(jnp.exp(-jnp.abs(g)))
        v = jnp.where(is_ls, ls, 0.0)
        hi, mid, lo = _split3(v)
        cs = (jnp.dot(tri, hi, preferred_element_type=F32)
              + jnp.dot(tri, mid, preferred_element_type=F32)
              + jnp.dot(tri, lo, preferred_element_type=F32))
        glob = cs + carry_scr[...]
        carry_scr[...] = glob[LANES - 1:LANES, :]
        out = jnp.where(is_glob, glob, jnp.where(is_ls, cs, g))
        gc_ref[sl, :] = out
        gr_ref[:, sl] = out.T[0:GATE_ROWS, :]
        fhi, fmid, flo = _split3(glob * LOG2E)
        mhi, mmid, mlo = _split3(qn_ref[sl, :] * kmax_ref[0:1, :] + 1.0)
        pieces = jnp.concatenate([fhi, fmid, flo, jnp.ones((LANES, GATE_LANES), BF16), mhi, mmid, mlo], axis=1)
        qx_ref[sl, :] = jnp.dot(pieces, pq_ref[...], preferred_element_type=F32).astype(BF16)
        kx_ref[sl, :] = jnp.dot(pieces, pk_ref[...], preferred_element_type=F32).astype(BF16)


_N_PIECES = 3
_PIECE_GROUPS = 2 * _N_PIECES + 1
_STAB_SLOT = 2 * _N_PIECES


def _placement_matrices():
    n = _N_PIECES
    pq = np.zeros((_PIECE_GROUPS * GATE_LANES, FOX_WIDTH), np.float32)
    pk = np.zeros((_PIECE_GROUPS * GATE_LANES, FOX_WIDTH), np.float32)
    for h in range(FOX_HEADS):
        for p in range(n):
            pq[p * GATE_LANES + h, h * FOX_HEAD_DIM + p] = 1.0
            pk[n * GATE_LANES + h, h * FOX_HEAD_DIM + p] = 1.0
            pq[n * GATE_LANES + h, h * FOX_HEAD_DIM + n + p] = 1.0
            pk[p * GATE_LANES + h, h * FOX_HEAD_DIM + n + p] = -1.0
            pq[(n + 1 + p) * GATE_LANES + h, h * FOX_HEAD_DIM + _STAB_SLOT + p] = -1.0
            pk[n * GATE_LANES + h, h * FOX_HEAD_DIM + _STAB_SLOT + p] = 1.0
    return pq, pk


def _gates(g, bias, tri, pq, pk, qn, kmax, *, batch, seq, rows=1024):
    T = g.shape[0]
    nblk = seq // rows
    return pl.pallas_call(
        functools.partial(_gates_body, rows=rows),
        grid=(batch, nblk),
        in_specs=[
            pl.BlockSpec((rows, GATE_LANES), lambda b, c: (b * nblk + c, 0)),
            pl.BlockSpec((1, GATE_LANES), lambda b, c: (0, 0)),
            pl.BlockSpec((LANES, LANES), lambda b, c: (0, 0)),
            pl.BlockSpec((_PIECE_GROUPS * GATE_LANES, FOX_WIDTH), lambda b, c: (0, 0)),
            pl.BlockSpec((_PIECE_GROUPS * GATE_LANES, FOX_WIDTH), lambda b, c: (0, 0)),
            pl.BlockSpec((rows, GATE_LANES), lambda b, c: (b * nblk + c, 0)),
            pl.BlockSpec((SUBLANES, GATE_LANES), lambda b, c: (b, 0)),
        ],
        out_specs=[
            pl.BlockSpec((rows, GATE_LANES), lambda b, c: (b * nblk + c, 0)),
            pl.BlockSpec((GATE_ROWS, rows), lambda b, c: (0, b * nblk + c)),
            pl.BlockSpec((rows, FOX_WIDTH), lambda b, c: (b * nblk + c, 0)),
            pl.BlockSpec((rows, FOX_WIDTH), lambda b, c: (b * nblk + c, 0)),
        ],
        out_shape=[
            jax.ShapeDtypeStruct((T, GATE_LANES), F32),
            jax.ShapeDtypeStruct((GATE_ROWS, T), F32),
            jax.ShapeDtypeStruct((T, FOX_WIDTH), BF16),
            jax.ShapeDtypeStruct((T, FOX_WIDTH), BF16),
        ],
        scratch_shapes=[pltpu.VMEM((1, GATE_LANES), F32)],
        compiler_params=_cparams(("parallel", "arbitrary")),
        name="gates",
    )(g, bias, tri, pq, pk, qn, kmax)


def _fox_body(lo_ref, fast_ref, q_ref, qx_ref, k_ref, kx_ref, v_ref, nw_ref, o_ref, *, tk, nq):
    b = pl.program_id(0)
    h = pl.program_id(1)
    qi = pl.program_id(2)
    tile = (b * FOX_HEADS + h) * nq + qi
    lo = lo_ref[tile]
    row = lax.broadcasted_iota(jnp.int32, (tk, tk), 0)
    col = lax.broadcasted_iota(jnp.int32, (tk, tk), 1)

    def load_kv(ki):
        start = pl.multiple_of(ki * tk, tk)
        k = jnp.concatenate([k_ref[pl.ds(start, tk), :], kx_ref[pl.ds(start, tk), :]], axis=1)
        return k, v_ref[pl.ds(start, tk), :]

    def logits(q, k, masked):
        s = lax.dot_general(q, k, (((1,), (1,)), ((), ())), preferred_element_type=F32)
        return jnp.where(row >= col, s, -jnp.inf) if masked else s

    def finish(r, num, den):
        out = num / den
        ms = jnp.mean(out * out, axis=-1, keepdims=True)
        o_ref[r * tk:(r + 1) * tk, :] = ((out * lax.rsqrt(ms + NORM_EPS)) * nw_ref[pl.ds(h, 1), :]).astype(BF16)

    @pl.when(fast_ref[tile] == 1)
    def _():
        q_halves = [jnp.concatenate([q_ref[r * tk:(r + 1) * tk, :], qx_ref[r * tk:(r + 1) * tk, :]], axis=1)
                    for r in range(2)]
        ones_col = (lax.broadcasted_iota(jnp.int32, (tk, LANES), 1) == 0).astype(BF16)

        def weights(r, k, masked):
            return jnp.exp2(logits(q_halves[r], k, masked)).astype(BF16)

        def values(ki):
            return jnp.concatenate([load_kv(ki)[1], ones_col], axis=1)

        def add(acc, p, va):
            return acc + jnp.dot(p, va, preferred_element_type=F32)

        def pair(kp, accs):
            blocks = [2 * kp, 2 * kp + 1]
            ps = [[weights(r, load_kv(ki)[0], False) for r in range(2)] for ki in blocks]
            for j, ki in enumerate(blocks):
                va = values(ki)
                accs = tuple(add(accs[r], ps[j][r], va) for r in range(2))
            return accs

        zero = jnp.zeros((tk, 2 * FOX_HEAD_DIM), F32)
        aa, ab = lax.fori_loop(lo // 2, qi, pair, (zero, zero))
        k = load_kv(2 * qi)[0]
        pa, pb = weights(0, k, True), weights(1, k, False)
        pb2 = weights(1, load_kv(2 * qi + 1)[0], True)
        va = values(2 * qi)
        aa, ab = add(aa, pa, va), add(ab, pb, va)
        ab = add(ab, pb2, values(2 * qi + 1))
        for r, acc in enumerate((aa, ab)):
            finish(r, acc[:, 0:FOX_HEAD_DIM], acc[:, FOX_HEAD_DIM:FOX_HEAD_DIM + 1])

    @pl.when(fast_ref[tile] == 0)
    def _():
        _fox_online(q_ref, qx_ref, load_kv, logits, finish, lo, qi, tk)


def _fox_online(q_ref, qx_ref, load_kv, logits, finish, lo, qi, tk):
    lane = lax.broadcasted_iota(jnp.int32, (tk, LANES), 1)
    no_stab = (lane < _STAB_SLOT) | (lane >= _STAB_SLOT + _N_PIECES)
    q_halves = [jnp.concatenate([q_ref[r * tk:(r + 1) * tk, :],
                                 jnp.where(no_stab, qx_ref[r * tk:(r + 1) * tk, :], jnp.zeros((), BF16))], axis=1)
                for r in range(2)]

    def weights(s, carry):
        m, l, acc = carry
        cols = [s[:, c * LANES:(c + 1) * LANES] for c in range(tk // LANES)]
        m_new = jnp.maximum(m, jnp.max(functools.reduce(jnp.maximum, cols), axis=-1, keepdims=True))
        alpha = jnp.exp2(m - m_new)
        ps = [jnp.exp2(c - m_new) for c in cols]
        l = alpha * l + functools.reduce(jnp.add, ps)
        return m_new, l, alpha * acc, jnp.concatenate(ps, axis=1).astype(BF16)

    def update(q, k, v, carry, masked):
        m, l, acc, p = weights(logits(q, k, masked), carry)
        return m, l, acc + jnp.dot(p, v, preferred_element_type=F32)

    def both(ki, carries):
        k, v = load_kv(ki)
        s = [logits(q_halves[r], k, False) for r in range(2)]
        w = [weights(s[r], carries[r]) for r in range(2)]
        return tuple((m, l, acc + jnp.dot(p, v, preferred_element_type=F32)) for m, l, acc, p in w)

    init = (jnp.full((tk, LANES), -jnp.inf, F32), jnp.zeros((tk, LANES), F32), jnp.zeros((tk, FOX_HEAD_DIM), F32))
    ca, cb = lax.fori_loop(lo, 2 * qi, both, (init, init))
    k, v = load_kv(2 * qi)
    ca = update(q_halves[0], k, v, ca, True)
    cb = update(q_halves[1], k, v, cb, False)
    k, v = load_kv(2 * qi + 1)
    cb = update(q_halves[1], k, v, cb, True)
    for r, (_, l, acc) in enumerate((ca, cb)):
        finish(r, acc, jnp.sum(l, axis=-1, keepdims=True))


def _fox_plan(qn, kmax, gc, *, batch, seq, tk):
    tq = 2 * tk
    nq = seq // tq
    nk = seq // tk
    heads = slice(0, FOX_HEADS)
    qmax = jnp.max(qn.reshape(batch, nq, tq, GATE_LANES), axis=2)[..., heads]
    gap = 2.0 * qmax * kmax[:, None, heads] + BOUND_SLACK_LOG2
    f2 = (gc[:, heads] * LOG2E).reshape(batch, nk, tk, FOX_HEADS)
    f_first = f2[:, ::2, 0, :]
    f_last = f2[:, :, tk - 1, :]
    bound = gap[:, :, None, :] + f_first[:, :, None, :] - f_last[:, None, :, :]
    below_diag = jnp.arange(nk)[None, :, None] < 2 * jnp.arange(nq)[:, None, None]
    skip = (bound < -F32_ZERO_LOG2) & below_diag[None]
    lo = jnp.sum(jnp.cumprod(skip.astype(jnp.int32), axis=2), axis=2)
    fast = (gap <= FAST_GAP_LOG2).astype(jnp.int32)
    flat = lambda a: jnp.transpose(a, (0, 2, 1)).reshape(-1)
    return flat(lo), flat(fast)


def _fox(z, qx, kx, nw, lo, fast, *, batch, seq, tk=512):
    T = z.shape[0]
    tq = 2 * tk
    nq = seq // tq
    kcol = FOX_WIDTH // FOX_HEAD_DIM
    grid_spec = pltpu.PrefetchScalarGridSpec(
        num_scalar_prefetch=2,
        grid=(batch, FOX_HEADS, nq),
        in_specs=[
            pl.BlockSpec((tq, FOX_HEAD_DIM), lambda b, h, i, lo, fast: (b * nq + i, h)),
            pl.BlockSpec((tq, FOX_HEAD_DIM), lambda b, h, i, lo, fast: (b * nq + i, h)),
            pl.BlockSpec((seq, FOX_HEAD_DIM), lambda b, h, i, lo, fast: (b, kcol + h)),
            pl.BlockSpec((seq, FOX_HEAD_DIM), lambda b, h, i, lo, fast: (b, h)),
            pl.BlockSpec((seq, FOX_HEAD_DIM), lambda b, h, i, lo, fast: (b, 2 * kcol + h)),
            pl.BlockSpec((FOX_HEADS, FOX_HEAD_DIM), lambda b, h, i, lo, fast: (0, 0)),
        ],
        out_specs=pl.BlockSpec((tq, FOX_HEAD_DIM), lambda b, h, i, lo, fast: (b * nq + i, h)),
    )
    return pl.pallas_call(
        functools.partial(_fox_body, tk=tk, nq=nq),
        grid_spec=grid_spec,
        out_shape=jax.ShapeDtypeStruct((T, FOX_WIDTH), BF16),
        compiler_params=_cparams(("parallel", "parallel", "arbitrary")),
        name="fox",
    )(lo, fast, z, qx, z, kx, z, nw)


def _mlstm_body(zc_ref, zprev_ref, v_ref, o_ref, gc_ref, gr_ref, cw_ref, cb_ref, nw_ref, out_ref,
                full_scr, c_scr, n_scr, m_scr, *, cps):
    c = pl.program_id(1)
    L = MLSTM_CHUNK
    dk = MLSTM_QK_DIM
    dv = MLSTM_V_DIM
    rows = cps * L

    @pl.when(c == 0)
    def _():
        c_scr[...] = jnp.zeros_like(c_scr)
        n_scr[...] = jnp.zeros_like(n_scr)
        m_scr[...] = jnp.zeros_like(m_scr)

    full_scr[0:SUBLANES, :] = jnp.where(c == 0, 0.0, zprev_ref[...])
    full_scr[SUBLANES:SUBLANES + rows, :] = zc_ref[...]
    y = cb_ref[...]
    for j in range(MLSTM_CONV):
        y = y + cw_ref[j:j + 1, :] * full_scr[pl.ds(SUBLANES - (MLSTM_CONV - 1) + j, rows), :]
    qk_all = y * jax.nn.sigmoid(y)

    row = lax.broadcasted_iota(jnp.int32, (L, L), 0)
    col = lax.broadcasted_iota(jnp.int32, (L, L), 1)
    causal = row >= col
    state = [(c_scr[hh], n_scr[hh], m_scr[hh][:, 0:1]) for hh in range(MLSTM_HEADS)]
    for sc, hh in [(sc, hh) for sc in range(cps) for hh in range(MLSTM_HEADS)]:
        rs = slice(sc * L, (sc + 1) * L)
        gcb = gc_ref[rs, :]
        grb = gr_ref[:, rs]
        qh = qk_all[rs, hh * dk:(hh + 1) * dk]
        kh = qk_all[rs, MLSTM_QK_WIDTH + hh * dk:MLSTM_QK_WIDTH + (hh + 1) * dk] * (dk ** -0.5)
        vh = v_ref[rs, hh * dv:(hh + 1) * dv]
        oh = o_ref[rs, hh * dv:(hh + 1) * dv].astype(F32)
        li = FOX_HEADS + hh
        lf = FOX_HEADS + MLSTM_HEADS + hh
        i_col = gcb[:, li:li + 1]
        b_col = gcb[:, lf:lf + 1]
        i_row = grb[li:li + 1, :]
        b_row = grb[lf:lf + 1, :]
        b_last = b_row[:, L - 1:L]

        c_prev, n_prev, m_prev = state[hh]

        m_loc = jnp.max(b_last - b_row + i_row, axis=-1, keepdims=True)
        a_end = jnp.exp(b_last - b_col + i_col - m_loc)
        kw = kh * a_end
        c_loc = lax.dot_general(kw.astype(BF16), vh, (((0,), (0,)), ((), ())), preferred_element_type=F32)
        n_loc = jnp.sum(kw, axis=0, keepdims=True)

        g_col = b_col + m_prev
        dmat = jnp.where(causal, b_col - b_row + i_row, -jnp.inf)
        m_t = jnp.maximum(g_col, jnp.max(dmat, axis=-1, keepdims=True))
        qb = qh.astype(BF16)
        qk = lax.dot_general(qb, kh.astype(BF16), (((1,), (1,)), ((), ())), preferred_element_type=F32)
        sm = qk * jnp.exp(dmat - m_t)
        inter = jnp.exp(g_col - m_t)
        num = (jnp.dot(sm.astype(BF16), vh, preferred_element_type=F32)
               + inter * jnp.dot(qb, c_prev.astype(BF16), preferred_element_type=F32))
        den = jnp.sum(sm, axis=-1, keepdims=True) + inter * jnp.sum(qh * n_prev, axis=-1, keepdims=True)
        cell = num / jnp.maximum(jnp.abs(den), jnp.exp(-m_t))
        gated = jax.nn.sigmoid(oh) * cell
        ms = jnp.mean(gated * gated, axis=-1, keepdims=True)
        out_ref[rs, hh * dv:(hh + 1) * dv] = (
            (gated * lax.rsqrt(ms + NORM_EPS)) * nw_ref[:, hh * dv:(hh + 1) * dv]).astype(BF16)

        m_new = jnp.maximum(b_last + m_prev, m_loc)
        a_prev = jnp.exp(b_last + m_prev - m_new)
        a_loc = jnp.exp(m_loc - m_new)
        state[hh] = (a_prev * c_prev + a_loc * c_loc, a_prev * n_prev + a_loc * n_loc, m_new)

    for hh, (c_new, n_new, m_new) in enumerate(state):
        c_scr[hh] = c_new
        n_scr[hh] = n_new
        m_scr[hh] = jnp.broadcast_to(m_new, (1, LANES))


def _mlstm(zc, z, gc, gr, conv_w, conv_b, nw, *, batch, seq, cps=1):
    T = zc.shape[0]
    L = cps * MLSTM_CHUNK
    nc = seq // L
    per = L // SUBLANES
    vcol = 3 * FOX_WIDTH // MLSTM_V_WIDTH
    return pl.pallas_call(
        functools.partial(_mlstm_body, cps=cps),
        grid=(batch, nc),
        in_specs=[
            pl.BlockSpec((L, 2 * MLSTM_QK_WIDTH), lambda b, c: (b * nc + c, 0)),
            pl.BlockSpec((SUBLANES, 2 * MLSTM_QK_WIDTH), lambda b, c: (jnp.maximum((b * nc + c) * per - 1, 0), 0)),
            pl.BlockSpec((L, MLSTM_V_WIDTH), lambda b, c: (b * nc + c, vcol)),
            pl.BlockSpec((L, MLSTM_V_WIDTH), lambda b, c: (b * nc + c, vcol + 1)),
            pl.BlockSpec((L, GATE_LANES), lambda b, c: (b * nc + c, 0)),
            pl.BlockSpec((GATE_ROWS, L), lambda b, c: (0, b * nc + c)),
            pl.BlockSpec((MLSTM_CONV, 2 * MLSTM_QK_WIDTH), lambda b, c: (0, 0)),
            pl.BlockSpec((1, 2 * MLSTM_QK_WIDTH), lambda b, c: (0, 0)),
            pl.BlockSpec((1, MLSTM_V_WIDTH), lambda b, c: (0, 0)),
        ],
        out_specs=pl.BlockSpec((L, MLSTM_V_WIDTH), lambda b, c: (b * nc + c, 0)),
        out_shape=jax.ShapeDtypeStruct((T, MLSTM_V_WIDTH), BF16),
        scratch_shapes=[
            pltpu.VMEM((SUBLANES + L, 2 * MLSTM_QK_WIDTH), F32),
            pltpu.VMEM((MLSTM_HEADS, MLSTM_QK_DIM, MLSTM_V_DIM), F32),
            pltpu.VMEM((MLSTM_HEADS, 1, MLSTM_QK_DIM), F32),
            pltpu.VMEM((MLSTM_HEADS, 1, LANES), F32),
        ],
        compiler_params=_cparams(("parallel", "arbitrary")),
        name="mlstm",
    )(zc, zc, z, z, gc, gr, conv_w, conv_b, nw)


def _outproj_body(att_ref, cell_ref, x_ref, wa_ref, wb_ref, n2_ref, x1_ref, h2t_ref):
    y = (jnp.dot(att_ref[...], wa_ref[...], preferred_element_type=F32)
         + jnp.dot(cell_ref[...], wb_ref[...], preferred_element_type=F32))
    x1 = x_ref[...] + y
    x1_ref[...] = x1
    ms = jnp.mean(x1 * x1, axis=-1, keepdims=True)
    h2 = (x1 * lax.rsqrt(ms + NORM_EPS)) * n2_ref[...]
    h2t_ref[...] = h2.T.astype(BF16)


def _outproj(att, cell, x2, wa, wb, n2, *, tm=512):
    T = x2.shape[0]
    return pl.pallas_call(
        _outproj_body,
        grid=(T // tm,),
        in_specs=[
            pl.BlockSpec((tm, FOX_WIDTH), lambda i: (i, 0)),
            pl.BlockSpec((tm, MLSTM_V_WIDTH), lambda i: (i, 0)),
            pl.BlockSpec((tm, D_MODEL), lambda i: (i, 0)),
            pl.BlockSpec((FOX_WIDTH, D_MODEL), lambda i: (0, 0)),
            pl.BlockSpec((MLSTM_V_WIDTH, D_MODEL), lambda i: (0, 0)),
            pl.BlockSpec((1, D_MODEL), lambda i: (0, 0)),
        ],
        out_specs=[
            pl.BlockSpec((tm, D_MODEL), lambda i: (i, 0)),
            pl.BlockSpec((D_MODEL, tm), lambda i: (0, i)),
        ],
        out_shape=[
            jax.ShapeDtypeStruct((T, D_MODEL), F32),
            jax.ShapeDtypeStruct((D_MODEL, T), BF16),
        ],
        compiler_params=_cparams(("parallel",)),
        name="outproj",
    )(att, cell, x2, wa, wb, n2)


def _row_iota(n_rows, tb):
    return lax.broadcasted_iota(jnp.int32, (n_rows, tb), 0).astype(F32)


def _extract16(s, val_ref, idx_ref):
    n_rows, tb = s.shape
    iota = _row_iota(n_rows, tb)

    def body(r, sw):
        mx = jnp.max(sw, axis=0, keepdims=True)
        first = jnp.min(jnp.where(sw == mx, iota, float(n_rows)), axis=0, keepdims=True)
        val_ref[pl.ds(r, 1), :] = mx
        idx_ref[pl.ds(r, 1), :] = first
        return jnp.where(iota == first, -jnp.inf, sw)

    lax.fori_loop(0, PEER_TOPK, body, s)


def _scatter_rounds(idx, values, n_rows, fill):
    tb = idx.shape[1]
    iota = _row_iota(n_rows, tb)
    table = jnp.full((n_rows, tb), fill, F32)
    for r in range(PEER_TOPK):
        v = values[r:r + 1, :] if values is not None else float(r)
        table = jnp.where(iota == idx[r:r + 1, :], v, table)
    return table


def _extract16_distinct(s, val_ref):
    def body(r, sw):
        mx = jnp.max(sw, axis=0, keepdims=True)
        val_ref[pl.ds(r, 1), :] = mx
        return jnp.where(sw == mx, -jnp.inf, sw)

    lax.fori_loop(0, PEER_TOPK, body, s)


def _count_ge(s, thr):
    return jnp.sum((s >= thr).astype(F32), axis=0, keepdims=True)


def _match_rounds(s, vals, values, fill):
    table = jnp.full(s.shape, fill, F32)
    for r in range(PEER_TOPK):
        v = values[r:r + 1, :] if values is not None else float(r)
        table = jnp.where(s == vals[r:r + 1, :], v, table)
    return table


def _peer_sel_body(h2t_ref, wqt_ref, keys_ref, grp_ref, e1_ref, c1_ref, n0_ref, w0_ref,
                   qt_scr, vals_scr, idx_scr, cand_scr, pe_scr, *, tb):
    half = PEER_KEY_DIM // 2
    qt_scr[...] = jnp.dot(wqt_ref[...], h2t_ref[...], preferred_element_type=F32).astype(BF16)

    def head_tables(h, exact):
        s0 = jnp.dot(keys_ref[0], qt_scr[(2 * h) * half:(2 * h + 1) * half, :], preferred_element_type=F32)
        s1 = jnp.dot(keys_ref[1], qt_scr[(2 * h + 1) * half:(2 * h + 2) * half, :], preferred_element_type=F32)
        if exact:
            _extract16(s0, vals_scr.at[0], idx_scr.at[0])
            _extract16(s1, vals_scr.at[1], idx_scr.at[1])
        else:
            _extract16_distinct(s0, vals_scr.at[0])
            _extract16_distinct(s1, vals_scr.at[1])
        a = vals_scr[0]
        b = vals_scr[1]
        ea = jnp.exp(a - a[0:1, :])
        eb = jnp.exp(b - b[0:1, :])
        off = 0
        for r in range(PEER_TOPK):
            ncol = _CAND_COLS[r]
            cand_scr[off:off + ncol, :] = a[r:r + 1, :] + b[0:ncol, :]
            pe_scr[off:off + ncol, :] = ea[r:r + 1, :] * eb[0:ncol, :]
            off += ncol
        cand_scr[_N_CAND:_N_CAND_PAD, :] = jnp.full((_N_CAND_PAD - _N_CAND, tb), -jnp.inf, F32)
        pe_scr[_N_CAND:_N_CAND_PAD, :] = jnp.zeros((_N_CAND_PAD - _N_CAND, tb), F32)
        cand = cand_scr[...]
        last = PEER_TOPK - 1
        if exact:
            _extract16(cand, vals_scr.at[2], idx_scr.at[2])
            sel = _scatter_rounds(idx_scr[2], jnp.ones((PEER_TOPK, tb), F32), _N_CAND_PAD, 0.0)
            ties = None
        else:
            _extract16_distinct(cand, vals_scr.at[2])
            sel = (cand >= vals_scr[2, last:last + 1, :]).astype(F32)
            ties = ((_count_ge(s0, a[last:last + 1, :]) != float(PEER_TOPK)).astype(F32)
                    + (_count_ge(s1, b[last:last + 1, :]) != float(PEER_TOPK)).astype(F32)
                    + (jnp.sum(sel, axis=0, keepdims=True) != float(PEER_TOPK)).astype(F32))
        zsum = jnp.sum(sel * pe_scr[...], axis=0, keepdims=True)
        nr = jnp.dot(grp_ref[...], sel.astype(BF16), preferred_element_type=F32)
        if exact:
            n0 = _scatter_rounds(idx_scr[0], nr, PEER_N_KEYS, 0.0)
            c1 = _scatter_rounds(idx_scr[1], None, PEER_N_KEYS, float(PEER_TOPK))
        else:
            n0 = _match_rounds(s0, a, nr, 0.0)
            c1 = _match_rounds(s1, b, None, float(PEER_TOPK))
        n0_ref[h] = n0
        w0_ref[h] = jnp.exp(s0 - a[0:1, :]) * (0.5 / zsum)
        e1_ref[h] = jnp.exp(s1 - b[0:1, :]).astype(BF16)
        c1_ref[h] = c1.astype(BF16)
        return ties

    for h in range(PEER_HEADS):
        ties = head_tables(h, False)

        @pl.when(jnp.max(ties) > 0.0)
        def _():
            head_tables(h, True)


def _peer_sel(h2t, wqt, keys, grp, *, tb=256):
    T = h2t.shape[1]
    tab = jax.ShapeDtypeStruct((PEER_HEADS, PEER_N_KEYS, T), F32)
    tab16 = jax.ShapeDtypeStruct((PEER_HEADS, PEER_N_KEYS, T), BF16)
    tab_spec = pl.BlockSpec((PEER_HEADS, PEER_N_KEYS, tb), lambda i: (0, 0, i))
    return pl.pallas_call(
        functools.partial(_peer_sel_body, tb=tb),
        grid=(T // tb,),
        in_specs=[
            pl.BlockSpec((D_MODEL, tb), lambda i: (0, i)),
            pl.BlockSpec((PEER_HEADS * PEER_KEY_DIM, D_MODEL), lambda i: (0, 0)),
            pl.BlockSpec((2, PEER_N_KEYS, PEER_KEY_DIM // 2), lambda i: (0, 0, 0)),
            pl.BlockSpec((PEER_TOPK, _N_CAND_PAD), lambda i: (0, 0)),
        ],
        out_specs=[tab_spec, tab_spec, tab_spec, tab_spec],
        out_shape=[tab16, tab16, tab, tab],
        scratch_shapes=[
            pltpu.VMEM((PEER_HEADS * PEER_KEY_DIM, tb), BF16),
            pltpu.VMEM((3, PEER_TOPK, tb), F32),
            pltpu.VMEM((3, PEER_TOPK, tb), F32),
            pltpu.VMEM((_N_CAND_PAD, tb), F32),
            pltpu.VMEM((_N_CAND_PAD, tb), F32),
        ],
        compiler_params=_cparams(("parallel",)),
        name="peer_sel",
    )(h2t, wqt, keys, grp)


def _peer_dense_body(u_ref, vt_ref, h2t_ref, e1_ref, c1_ref, n0_ref, w0_ref, yt_ref, st_a, st_b, *, ec, tb, nk):
    k = pl.program_id(1)
    slabs = ec // PEER_N_KEYS

    def pre_activations(st_w):
        st_w[...] = jnp.dot(u_ref[...], h2t_ref[...], preferred_element_type=F32)

    def activate_and_project(st_r):
        parts = []
        for ii in range(slabs):
            g = jnp.zeros((PEER_N_KEYS, tb), BF16)
            for h in range(PEER_HEADS):
                n_row = n0_ref[h, ii:ii + 1, :].astype(BF16)
                w_row = w0_ref[h, ii:ii + 1, :].astype(BF16)
                g = g + jnp.where(c1_ref[h] < n_row, e1_ref[h] * w_row, jnp.zeros((), BF16))
            x = st_r[ii * PEER_N_KEYS:(ii + 1) * PEER_N_KEYS, :]
            act = x * (1.0 + lax.erf(x * math.sqrt(0.5)))
            parts.append(act.astype(BF16) * g)
        at = jnp.concatenate(parts, axis=0)
        yt_ref[...] += jnp.dot(vt_ref[...], at, preferred_element_type=F32)

    @pl.when(k == 0)
    def _():
        yt_ref[...] = jnp.zeros_like(yt_ref)
        pre_activations(st_a)

    @pl.when((k > 0) & (k < nk) & (k % 2 == 1))
    def _():
        pre_activations(st_b)
        activate_and_project(st_a)

    @pl.when((k > 0) & (k < nk) & (k % 2 == 0))
    def _():
        pre_activations(st_a)
        activate_and_project(st_b)

    @pl.when(k == nk)
    def _():
        activate_and_project(st_b if nk % 2 == 0 else st_a)


def _peer_dense(u, vt, h2t, e1, c1, n0, w0, *, tb=512, ec=1024):
    T = h2t.shape[1]
    nk = PEER_N_EXPERTS // ec
    slabs = ec // PEER_N_KEYS
    assert slabs == SUBLANES, "one f32 sublane tile of per-slab gate rows per expert chunk"
    tab_spec = pl.BlockSpec((PEER_HEADS, PEER_N_KEYS, tb), lambda i, k: (0, 0, i))
    row_spec = pl.BlockSpec((PEER_HEADS, slabs, tb), lambda i, k: (0, jnp.maximum(k - 1, 0), i))
    return pl.pallas_call(
        functools.partial(_peer_dense_body, ec=ec, tb=tb, nk=nk),
        grid=(T // tb, nk + 1),
        in_specs=[
            pl.BlockSpec((ec, D_MODEL), lambda i, k: (jnp.minimum(k, nk - 1), 0)),
            pl.BlockSpec((D_MODEL, ec), lambda i, k: (0, jnp.maximum(k - 1, 0))),
            pl.BlockSpec((D_MODEL, tb), lambda i, k: (0, i)),
            tab_spec, tab_spec, row_spec, row_spec,
        ],
        out_specs=pl.BlockSpec((D_MODEL, tb), lambda i, k: (0, i)),
        out_shape=jax.ShapeDtypeStruct((D_MODEL, T), F32),
        scratch_shapes=[pltpu.VMEM((ec, tb), F32), pltpu.VMEM((ec, tb), F32)],
        compiler_params=_cparams(("parallel", "arbitrary")),
        name="peer_dense",
    )(u, vt, h2t, e1, c1, n0, w0)


def _final_body(x1_ref, yt_ref, w_ref, o_ref):
    x2 = x1_ref[...] + yt_ref[...].T
    ms = jnp.mean(x2 * x2, axis=-1, keepdims=True)
    o_ref[...] = (x2 * lax.rsqrt(ms + NORM_EPS)) * w_ref[...]


def _final(x1, yt, w, *, tm=512):
    T = x1.shape[0]
    return pl.pallas_call(
        _final_body,
        grid=(T // tm,),
        in_specs=[
            pl.BlockSpec((tm, D_MODEL), lambda i: (i, 0)),
            pl.BlockSpec((D_MODEL, tm), lambda i: (0, i)),
            pl.BlockSpec((1, D_MODEL), lambda i: (0, 0)),
        ],
        out_specs=pl.BlockSpec((tm, D_MODEL), lambda i: (i, 0)),
        out_shape=jax.ShapeDtypeStruct((T, D_MODEL), F32),
        compiler_params=_cparams(("parallel",)),
        name="final",
    )(x1, yt, w)


def _group_matrix():
    g = np.zeros((PEER_TOPK, _N_CAND_PAD), np.float32)
    off = 0
    for r, ncol in enumerate(_CAND_COLS):
        g[r, off:off + ncol] = 1.0
        off += ncol
    return g


def _layer(x2, norm1_w, w_in, fox_f_bias, conv_w, conv_b, i_bias, f_bias, fox_nw, mlstm_nw, w_out, norm2_w,
           w_q, keys, u, v, *, batch, seq):
    splits = np.cumsum((FOX_WIDTH, FOX_WIDTH, FOX_WIDTH, FOX_HEADS, MLSTM_QK_WIDTH, MLSTM_QK_WIDTH,
                        MLSTM_V_WIDTH, MLSTM_HEADS, MLSTM_HEADS, MLSTM_V_WIDTH))[:-1]
    fq, fk, fv, ff, mq, mk, mv, mi, mf, mo = jnp.split(w_in, [int(p) for p in splits], axis=1)
    w_main = jnp.concatenate([fq, fk, fv, mv, mo, mq, mk], axis=1).astype(BF16)
    gate_pad = GATE_LANES - FOX_HEADS - 2 * MLSTM_HEADS
    w_gate = jnp.pad(jnp.concatenate([ff, mi, mf], axis=1), ((0, 0), (0, gate_pad))).astype(BF16)
    gate_bias = jnp.pad(jnp.concatenate([fox_f_bias, i_bias, f_bias]), (0, gate_pad)).reshape(1, GATE_LANES)
    tri = jnp.asarray(np.tril(np.ones((LANES, LANES), np.float32)), BF16)

    z, zc, g = _inproj(x2, norm1_w.reshape(1, D_MODEL), w_main, w_gate)
    pq, pk = _placement_matrices()
    qn, kn = _fox_norms(z, jnp.asarray(_head_group_matrix(), BF16))
    kmax = jnp.max(kn.reshape(batch, seq, GATE_LANES), axis=1)
    gc, gr, qx, kx = _gates(g, gate_bias, tri, jnp.asarray(pq, BF16), jnp.asarray(pk, BF16), qn,
                            jnp.repeat(kmax, SUBLANES, axis=0), batch=batch, seq=seq)
    lo, fast = _fox_plan(qn, kmax, gc, batch=batch, seq=seq, tk=FOX_KEY_BLOCK)
    att = _fox(z, qx, kx, fox_nw.reshape(FOX_HEADS, FOX_HEAD_DIM), lo, fast, batch=batch, seq=seq, tk=FOX_KEY_BLOCK)
    cell = _mlstm(zc, z, gc, gr, conv_w, conv_b.reshape(1, -1), mlstm_nw.reshape(1, -1), batch=batch, seq=seq)
    x1, h2t = _outproj(att, cell, x2, w_out[:FOX_WIDTH].astype(BF16), w_out[FOX_WIDTH:].astype(BF16),
                       norm2_w.reshape(1, D_MODEL))
    e1, c1, n0, w0 = _peer_sel(h2t, w_q.T.astype(BF16), keys.astype(BF16), jnp.asarray(_group_matrix(), BF16))
    yt = _peer_dense(u.astype(BF16), v.T.astype(BF16), h2t, e1, c1, n0, w0)
    return x1, yt


def kernel(x, norm1_w, w_in, fox_f_bias, mlstm_conv_w, mlstm_conv_b, mlstm_i_bias, mlstm_f_bias, fox_out_norm_w,
           mlstm_out_norm_w, w_out, norm2_w, peer_w_q, peer_keys, peer_u, peer_v, final_norm_w):
    batch, seq, _ = x.shape
    assert w_in.shape[0] == 1, "single-layer block: the final norm is fused with the last residual add"
    x2 = x.reshape(batch * seq, D_MODEL)
    x1, yt = _layer(x2, norm1_w[0], w_in[0], fox_f_bias[0], mlstm_conv_w[0], mlstm_conv_b[0], mlstm_i_bias[0],
                    mlstm_f_bias[0], fox_out_norm_w[0], mlstm_out_norm_w[0], w_out[0], norm2_w[0],
                    peer_w_q[0], peer_keys[0], peer_u[0], peer_v[0], batch=batch, seq=seq)
    out = _final(x1, yt, final_norm_w.reshape(1, D_MODEL))
    return out.reshape(batch, seq, D_MODEL)
```

```python
import functools
import math

import numpy as np
import jax
import jax.numpy as jnp
from jax import lax
from jax.experimental import pallas as pl
from jax.experimental.pallas import tpu as pltpu

F32 = jnp.float32
BF16 = jnp.bfloat16

D_MODEL = 2048
FOX_HEADS = 8
FOX_HEAD_DIM = 128
FOX_WIDTH = FOX_HEADS * FOX_HEAD_DIM
MLSTM_HEADS = 4
MLSTM_QK_DIM = 128
MLSTM_V_DIM = 256
MLSTM_QK_WIDTH = MLSTM_HEADS * MLSTM_QK_DIM
MLSTM_V_WIDTH = MLSTM_HEADS * MLSTM_V_DIM
MLSTM_CONV = 4
MLSTM_CHUNK = 128
PEER_HEADS = 8
PEER_KEY_DIM = 256
PEER_N_KEYS = 128
PEER_TOPK = 16
PEER_N_EXPERTS = PEER_N_KEYS * PEER_N_KEYS
NORM_EPS = 1e-6
LOG2E = math.log2(math.e)
NORM_MARGIN = 1.0 + 2.0 ** -7
F32_ZERO_LOG2 = 150.0
FAST_GAP_LOG2 = 90.0
BOUND_SLACK_LOG2 = 4.0
FOX_KEY_BLOCK = 512

LANES = 128
SUBLANES = 8
GATE_LANES = LANES
GATE_ROWS = 16
VMEM_LIMIT = 56 * 1024 * 1024

_CAND_COLS = tuple(PEER_TOPK // (r + 1) for r in range(PEER_TOPK))
_N_CAND = sum(_CAND_COLS)
_N_CAND_PAD = -(-_N_CAND // SUBLANES) * SUBLANES


def _cparams(sem):
    return pltpu.CompilerParams(dimension_semantics=sem, vmem_limit_bytes=VMEM_LIMIT)


def _inproj_body(x_ref, nw_ref, w_ref, wg_ref, z_ref, zc_ref, g_ref, h_scr, *, n_main, q_blocks, q_scale):
    j = pl.program_id(1)

    @pl.when(j == 0)
    def _():
        x = x_ref[...]
        ms = jnp.mean(x * x, axis=-1, keepdims=True)
        hb = ((x * lax.rsqrt(ms + NORM_EPS)) * nw_ref[...]).astype(BF16)
        h_scr[...] = hb
        g_ref[...] = jnp.dot(hb, wg_ref[...], preferred_element_type=F32)

    z = jnp.dot(h_scr[...], w_ref[...], preferred_element_type=F32)

    @pl.when(j < n_main)
    def _():
        scale = jnp.where(j < q_blocks, q_scale, 1.0).astype(F32)
        z_ref[...] = (z * scale).astype(BF16)

    @pl.when(j >= n_main)
    def _():
        zc_ref[...] = z


def _inproj(x2, norm_w, w_main, w_gate, *, tm=1024, tn=1024):
    T = x2.shape[0]
    n_cols = w_main.shape[1]
    n_conv = 2 * MLSTM_QK_WIDTH
    n_main = (n_cols - n_conv) // tn
    n_blocks = n_cols // tn
    body = functools.partial(_inproj_body, n_main=n_main, q_blocks=FOX_WIDTH // tn,
                             q_scale=FOX_HEAD_DIM ** -0.5 * LOG2E)
    return pl.pallas_call(
        body,
        grid=(T // tm, n_blocks),
        in_specs=[
            pl.BlockSpec((tm, D_MODEL), lambda i, j: (i, 0)),
            pl.BlockSpec((1, D_MODEL), lambda i, j: (0, 0)),
            pl.BlockSpec((D_MODEL, tn), lambda i, j: (0, j)),
            pl.BlockSpec((D_MODEL, GATE_LANES), lambda i, j: (0, 0)),
        ],
        out_specs=[
            pl.BlockSpec((tm, tn), lambda i, j: (i, jnp.minimum(j, n_main - 1))),
            pl.BlockSpec((tm, tn), lambda i, j: (i, jnp.maximum(j - n_main, 0))),
            pl.BlockSpec((tm, GATE_LANES), lambda i, j: (i, 0)),
        ],
        out_shape=[
            jax.ShapeDtypeStruct((T, n_cols - n_conv), BF16),
            jax.ShapeDtypeStruct((T, n_conv), F32),
            jax.ShapeDtypeStruct((T, GATE_LANES), F32),
        ],
        scratch_shapes=[pltpu.VMEM((tm, D_MODEL), BF16)],
        compiler_params=_cparams(("parallel", "arbitrary")),
        name="inproj",
    )(x2, norm_w, w_main, w_gate)


def _split3(v):
    hi = v.astype(BF16)
    r1 = v - hi.astype(F32)
    mid = r1.astype(BF16)
    lo = (r1 - mid.astype(F32)).astype(BF16)
    return hi, mid, lo


def _fox_norms_body(q_ref, k_ref, grp_ref, qn_ref, kn_ref):
    for src, dst in ((q_ref, qn_ref), (k_ref, kn_ref)):
        x = src[...].astype(F32)
        ss = jnp.dot((x * x).astype(BF16), grp_ref[...], preferred_element_type=F32)
        dst[...] = jnp.sqrt(ss) * NORM_MARGIN


def _head_group_matrix():
    g = np.zeros((FOX_WIDTH, GATE_LANES), np.float32)
    for h in range(FOX_HEADS):
        g[h * FOX_HEAD_DIM:(h + 1) * FOX_HEAD_DIM, h] = 1.0
    return g


def _fox_norms(z, grp, *, rows=1024):
    T = z.shape[0]
    out = jax.ShapeDtypeStruct((T, GATE_LANES), F32)
    return pl.pallas_call(
        _fox_norms_body,
        grid=(T // rows,),
        in_specs=[
            pl.BlockSpec((rows, FOX_WIDTH), lambda i: (i, 0)),
            pl.BlockSpec((rows, FOX_WIDTH), lambda i: (i, 1)),
            pl.BlockSpec((FOX_WIDTH, GATE_LANES), lambda i: (0, 0)),
        ],
        out_specs=[pl.BlockSpec((rows, GATE_LANES), lambda i: (i, 0))] * 2,
        out_shape=[out, out],
        compiler_params=_cparams(("parallel",)),
        name="fox_norms",
    )(z, z, grp)


def _gates_body(g_ref, bias_ref, tri_ref, pq_ref, pk_ref, qn_ref, kmax_ref, gc_ref, gr_ref, qx_ref, kx_ref,
                carry_scr, *, rows):
    c = pl.program_id(1)

    @pl.when(c == 0)
    def _():
        carry_scr[...] = jnp.zeros_like(carry_scr)

    lane = lax.broadcasted_iota(jnp.int32, (LANES, GATE_LANES), 1)
    is_glob = lane < FOX_HEADS
    is_ls = is_glob | ((lane >= FOX_HEADS + MLSTM_HEADS) & (lane < FOX_HEADS + 2 * MLSTM_HEADS))
    tri = tri_ref[...]
    for s in range(rows // LANES):
        sl = slice(s * LANES, (s + 1) * LANES)
        g = g_ref[sl, :] + bias_ref[...]
        ls = jnp.minimum(g, 0.0) - jnp.log1p(jnp.exp(-jnp.abs(g)))
        v = jnp.where(is_ls, ls, 0.0)
        hi, mid, lo = _split3(v)
        cs = (jnp.dot(tri, hi, preferred_element_type=F32)
              + jnp.dot(tri, mid, preferred_element_type=F32)
              + jnp.dot(tri, lo, preferred_element_type=F32))
        glob = cs + carry_scr[...]
        carry_scr[...] = glob[LANES - 1:LANES, :]
        out = jnp.where(is_glob, glob, jnp.where(is_ls, cs, g))
        gc_ref[sl, :] = out
        gr_ref[:, sl] = out.T[0:GATE_ROWS, :]
        fhi, fmid, flo = _split3(glob * LOG2E)
        mhi, mmid, mlo = _split3(qn_ref[sl, :] * kmax_ref[0:1, :] + 1.0)
        pieces = jnp.concatenate([fhi, fmid, flo, jnp.ones((LANES, GATE_LANES), BF16), mhi, mmid, mlo], axis=1)
        qx_ref[sl, :] = jnp.dot(pieces, pq_ref[...], preferred_element_type=F32).astype(BF16)
        kx_ref[sl, :] = jnp.dot(pieces, pk_ref[...], preferred_element_type=F32).astype(BF16)


_N_PIECES = 3
_PIECE_GROUPS = 2 * _N_PIECES + 1
_STAB_SLOT = 2 * _N_PIECES


def _placement_matrices():
    n = _N_PIECES
    pq = np.zeros((_PIECE_GROUPS * GATE_LANES, FOX_WIDTH), np.float32)
    pk = np.zeros((_PIECE_GROUPS * GATE_LANES, FOX_WIDTH), np.float32)
    for h in range(FOX_HEADS):
        for p in range(n):
            pq[p * GATE_LANES + h, h * FOX_HEAD_DIM + p] = 1.0
            pk[n * GATE_LANES + h, h * FOX_HEAD_DIM + p] = 1.0
            pq[n * GATE_LANES + h, h * FOX_HEAD_DIM + n + p] = 1.0
            pk[p * GATE_LANES + h, h * FOX_HEAD_DIM + n + p] = -1.0
            pq[(n + 1 + p) * GATE_LANES + h, h * FOX_HEAD_DIM + _STAB_SLOT + p] = -1.0
            pk[n * GATE_LANES + h, h * FOX_HEAD_DIM + _STAB_SLOT + p] = 1.0
    return pq, pk


def _gates(g, bias, tri, pq, pk, qn, kmax, *, batch, seq, rows=1024):
    T = g.shape[0]
    nblk = seq // rows
    return pl.pallas_call(
        functools.partial(_gates_body, rows=rows),
        grid=(batch, nblk),
        in_specs=[
            pl.BlockSpec((rows, GATE_LANES), lambda b, c: (b * nblk + c, 0)),
            pl.BlockSpec((1, GATE_LANES), lambda b, c: (0, 0)),
            pl.BlockSpec((LANES, LANES), lambda b, c: (0, 0)),
            pl.BlockSpec((_PIECE_GROUPS * GATE_LANES, FOX_WIDTH), lambda b, c: (0, 0)),
            pl.BlockSpec((_PIECE_GROUPS * GATE_LANES, FOX_WIDTH), lambda b, c: (0, 0)),
            pl.BlockSpec((rows, GATE_LANES), lambda b, c: (b * nblk + c, 0)),
            pl.BlockSpec((SUBLANES, GATE_LANES), lambda b, c: (b, 0)),
        ],
        out_specs=[
            pl.BlockSpec((rows, GATE_LANES), lambda b, c: (b * nblk + c, 0)),
            pl.BlockSpec((GATE_ROWS, rows), lambda b, c: (0, b * nblk + c)),
            pl.BlockSpec((rows, FOX_WIDTH), lambda b, c: (b * nblk + c, 0)),
            pl.BlockSpec((rows, FOX_WIDTH), lambda b, c: (b * nblk + c, 0)),
        ],
        out_shape=[
            jax.ShapeDtypeStruct((T, GATE_LANES), F32),
            jax.ShapeDtypeStruct((GATE_ROWS, T), F32),
            jax.ShapeDtypeStruct((T, FOX_WIDTH), BF16),
            jax.ShapeDtypeStruct((T, FOX_WIDTH), BF16),
        ],
        scratch_shapes=[pltpu.VMEM((1, GATE_LANES), F32)],
        compiler_params=_cparams(("parallel", "arbitrary")),
        name="gates",
    )(g, bias, tri, pq, pk, qn, kmax)


def _fox_body(lo_ref, fast_ref, q_ref, qx_ref, k_ref, kx_ref, v_ref, nw_ref, o_ref, *, tk, nq):
    b = pl.program_id(0)
    h = pl.program_id(1)
    qi = pl.program_id(2)
    tile = (b * FOX_HEADS + h) * nq + qi
    lo = lo_ref[tile]
    row = lax.broadcasted_iota(jnp.int32, (tk, tk), 0)
    col = lax.broadcasted_iota(jnp.int32, (tk, tk), 1)

    def load_kv(ki):
        start = pl.multiple_of(ki * tk, tk)
        k = jnp.concatenate([k_ref[pl.ds(start, tk), :], kx_ref[pl.ds(start, tk), :]], axis=1)
        return k, v_ref[pl.ds(start, tk), :]

    def logits(q, k, masked):
        s = lax.dot_general(q, k, (((1,), (1,)), ((), ())), preferred_element_type=F32)
        return jnp.where(row >= col, s, -jnp.inf) if masked else s

    def finish(r, num, den):
        out = num / den
        ms = jnp.mean(out * out, axis=-1, keepdims=True)
        o_ref[r * tk:(r + 1) * tk, :] = ((out * lax.rsqrt(ms + NORM_EPS)) * nw_ref[pl.ds(h, 1), :]).astype(BF16)

    @pl.when(fast_ref[tile] == 1)
    def _():
        q_halves = [jnp.concatenate([q_ref[r * tk:(r + 1) * tk, :], qx_ref[r * tk:(r + 1) * tk, :]], axis=1)
                    for r in range(2)]
        ones_col = (lax.broadcasted_iota(jnp.int32, (tk, LANES), 1) == 0).astype(BF16)

        def weights(r, k, masked):
            return jnp.exp2(logits(q_halves[r], k, masked)).astype(BF16)

        def values(ki):
            return jnp.concatenate([load_kv(ki)[1], ones_col], axis=1)

        def add(acc, p, va):
            return acc + jnp.dot(p, va, preferred_element_type=F32)

        def pair(kp, accs):
            blocks = [2 * kp, 2 * kp + 1]
            ps = [[weights(r, load_kv(ki)[0], False) for r in range(2)] for ki in blocks]
            for j, ki in enumerate(blocks):
                va = values(ki)
                accs = tuple(add(accs[r], ps[j][r], va) for r in range(2))
            return accs

        zero = jnp.zeros((tk, 2 * FOX_HEAD_DIM), F32)
        aa, ab = lax.fori_loop(lo // 2, qi, pair, (zero, zero))
        k = load_kv(2 * qi)[0]
        pa, pb = weights(0, k, True), weights(1, k, False)
        pb2 = weights(1, load_kv(2 * qi + 1)[0], True)
        va = values(2 * qi)
        aa, ab = add(aa, pa, va), add(ab, pb, va)
        ab = add(ab, pb2, values(2 * qi + 1))
        for r, acc in enumerate((aa, ab)):
            finish(r, acc[:, 0:FOX_HEAD_DIM], acc[:, FOX_HEAD_DIM:FOX_HEAD_DIM + 1])

    @pl.when(fast_ref[tile] == 0)
    def _():
        _fox_online(q_ref, qx_ref, load_kv, logits, finish, lo, qi, tk)


def _fox_online(q_ref, qx_ref, load_kv, logits, finish, lo, qi, tk):
    lane = lax.broadcasted_iota(jnp.int32, (tk, LANES), 1)
    no_stab = (lane < _STAB_SLOT) | (lane >= _STAB_SLOT + _N_PIECES)
    q_halves = [jnp.concatenate([q_ref[r * tk:(r + 1) * tk, :],
                                 jnp.where(no_stab, qx_ref[r * tk:(r + 1) * tk, :], jnp.zeros((), BF16))], axis=1)
                for r in range(2)]

    def weights(s, carry):
        m, l, acc = carry
        cols = [s[:, c * LANES:(c + 1) * LANES] for c in range(tk // LANES)]
        m_new = jnp.maximum(m, jnp.max(functools.reduce(jnp.maximum, cols), axis=-1, keepdims=True))
        alpha = jnp.exp2(m - m_new)
        ps = [jnp.exp2(c - m_new) for c in cols]
        l = alpha * l + functools.reduce(jnp.add, ps)
        return m_new, l, alpha * acc, jnp.concatenate(ps, axis=1).astype(BF16)

    def update(q, k, v, carry, masked):
        m, l, acc, p = weights(logits(q, k, masked), carry)
        return m, l, acc + jnp.dot(p, v, preferred_element_type=F32)

    def both(ki, carries):
        k, v = load_kv(ki)
        s = [logits(q_halves[r], k, False) for r in range(2)]
        w = [weights(s[r], carries[r]) for r in range(2)]
        return tuple((m, l, acc + jnp.dot(p, v, preferred_element_type=F32)) for m, l, acc, p in w)

    init = (jnp.full((tk, LANES), -jnp.inf, F32), jnp.zeros((tk, LANES), F32), jnp.zeros((tk, FOX_HEAD_DIM), F32))
    ca, cb = lax.fori_loop(lo, 2 * qi, both, (init, init))
    k, v = load_kv(2 * qi)
    ca = update(q_halves[0], k, v, ca, True)
    cb = update(q_halves[1], k, v, cb, False)
    k, v = load_kv(2 * qi + 1)
    cb = update(q_halves[1], k, v, cb, True)
    for r, (_, l, acc) in enumerate((ca, cb)):
        finish(r, acc, jnp.sum(l, axis=-1, keepdims=True))


def _fox_plan(qn, kmax, gc, *, batch, seq, tk):
    tq = 2 * tk
    nq = seq // tq
    nk = seq // tk
    heads = slice(0, FOX_HEADS)
    qmax = jnp.max(qn.reshape(batch, nq, tq, GATE_LANES), axis=2)[..., heads]
    gap = 2.0 * qmax * kmax[:, None, heads] + BOUND_SLACK_LOG2
    f2 = (gc[:, heads] * LOG2E).reshape(batch, nk, tk, FOX_HEADS)
    f_first = f2[:, ::2, 0, :]
    f_last = f2[:, :, tk - 1, :]
    bound = gap[:, :, None, :] + f_first[:, :, None, :] - f_last[:, None, :, :]
    below_diag = jnp.arange(nk)[None, :, None] < 2 * jnp.arange(nq)[:, None, None]
    skip = (bound < -F32_ZERO_LOG2) & below_diag[None]
    lo = jnp.sum(jnp.cumprod(skip.astype(jnp.int32), axis=2), axis=2)
    fast = (gap <= FAST_GAP_LOG2).astype(jnp.int32)
    flat = lambda a: jnp.transpose(a, (0, 2, 1)).reshape(-1)
    return flat(lo), flat(fast)


def _fox(z, qx, kx, nw, lo, fast, *, batch, seq, tk=512):
    T = z.shape[0]
    tq = 2 * tk
    nq = seq // tq
    kcol = FOX_WIDTH // FOX_HEAD_DIM
    grid_spec = pltpu.PrefetchScalarGridSpec(
        num_scalar_prefetch=2,
        grid=(batch, FOX_HEADS, nq),
        in_specs=[
            pl.BlockSpec((tq, FOX_HEAD_DIM), lambda b, h, i, lo, fast: (b * nq + i, h)),
            pl.BlockSpec((tq, FOX_HEAD_DIM), lambda b, h, i, lo, fast: (b * nq + i, h)),
            pl.BlockSpec((seq, FOX_HEAD_DIM), lambda b, h, i, lo, fast: (b, kcol + h)),
            pl.BlockSpec((seq, FOX_HEAD_DIM), lambda b, h, i, lo, fast: (b, h)),
            pl.BlockSpec((seq, FOX_HEAD_DIM), lambda b, h, i, lo, fast: (b, 2 * kcol + h)),
            pl.BlockSpec((FOX_HEADS, FOX_HEAD_DIM), lambda b, h, i, lo, fast: (0, 0)),
        ],
        out_specs=pl.BlockSpec((tq, FOX_HEAD_DIM), lambda b, h, i, lo, fast: (b * nq + i, h)),
    )
    return pl.pallas_call(
        functools.partial(_fox_body, tk=tk, nq=nq),
        grid_spec=grid_spec,
        out_shape=jax.ShapeDtypeStruct((T, FOX_WIDTH), BF16),
        compiler_params=_cparams(("parallel", "parallel", "arbitrary")),
        name="fox",
    )(lo, fast, z, qx, z, kx, z, nw)


def _mlstm_body(zc_ref, zprev_ref, v_ref, o_ref, gc_ref, gr_ref, cw_ref, cb_ref, nw_ref, out_ref,
                full_scr, c_scr, n_scr, m_scr, *, cps):
    c = pl.program_id(1)
    L = MLSTM_CHUNK
    dk = MLSTM_QK_DIM
    dv = MLSTM_V_DIM
    rows = cps * L

    @pl.when(c == 0)
    def _():
        c_scr[...] = jnp.zeros_like(c_scr)
        n_scr[...] = jnp.zeros_like(n_scr)
        m_scr[...] = jnp.zeros_like(m_scr)

    full_scr[0:SUBLANES, :] = jnp.where(c == 0, 0.0, zprev_ref[...])
    full_scr[SUBLANES:SUBLANES + rows, :] = zc_ref[...]
    y = cb_ref[...]
    for j in range(MLSTM_CONV):
        y = y + cw_ref[j:j + 1, :] * full_scr[pl.ds(SUBLANES - (MLSTM_CONV - 1) + j, rows), :]
    qk_all = y * jax.nn.sigmoid(y)

    row = lax.broadcasted_iota(jnp.int32, (L, L), 0)
    col = lax.broadcasted_iota(jnp.int32, (L, L), 1)
    causal = row >= col
    assert cps == 1
    heads = range(MLSTM_HEADS)
    gcb = gc_ref[...]
    grb = gr_ref[...]
    qh = [qk_all[:, hh * dk:(hh + 1) * dk] for hh in heads]
    kh = [qk_all[:, MLSTM_QK_WIDTH + hh * dk:MLSTM_QK_WIDTH + (hh + 1) * dk] * (dk ** -0.5) for hh in heads]
    vh = [v_ref[:, hh * dv:(hh + 1) * dv] for hh in heads]
    i_col = [gcb[:, FOX_HEADS + hh:FOX_HEADS + hh + 1] for hh in heads]
    b_col = [gcb[:, FOX_HEADS + MLSTM_HEADS + hh:FOX_HEADS + MLSTM_HEADS + hh + 1] for hh in heads]
    i_row = [grb[FOX_HEADS + hh:FOX_HEADS + hh + 1, :] for hh in heads]
    b_row = [grb[FOX_HEADS + MLSTM_HEADS + hh:FOX_HEADS + MLSTM_HEADS + hh + 1, :] for hh in heads]
    b_last = [b_row[hh][:, L - 1:L] for hh in heads]
    c_prev = [c_scr[hh] for hh in heads]
    n_prev = [n_scr[hh] for hh in heads]
    m_prev = [m_scr[hh][:, 0:1] for hh in heads]

    m_loc = [jnp.max(b_last[hh] - b_row[hh] + i_row[hh], axis=-1, keepdims=True) for hh in heads]
    kw = [kh[hh] * jnp.exp(b_last[hh] - b_col[hh] + i_col[hh] - m_loc[hh]) for hh in heads]
    g_col = [b_col[hh] + m_prev[hh] for hh in heads]
    dmat = [jnp.where(causal, b_col[hh] - b_row[hh] + i_row[hh], -jnp.inf) for hh in heads]
    m_t = [jnp.maximum(g_col[hh], jnp.max(dmat[hh], axis=-1, keepdims=True)) for hh in heads]
    decay = [jnp.exp(dmat[hh] - m_t[hh]) for hh in heads]
    inter = [jnp.exp(g_col[hh] - m_t[hh]) for hh in heads]
    qb = [qh[hh].astype(BF16) for hh in heads]
    qk = [lax.dot_general(qb[hh], kh[hh].astype(BF16), (((1,), (1,)), ((), ())), preferred_element_type=F32)
          for hh in heads]
    carried = [jnp.dot(qb[hh], c_prev[hh].astype(BF16), preferred_element_type=F32) for hh in heads]
    c_loc = [lax.dot_general(kw[hh].astype(BF16), vh[hh], (((0,), (0,)), ((), ())), preferred_element_type=F32)
             for hh in heads]
    sm = [qk[hh] * decay[hh] for hh in heads]
    intra = [jnp.dot(sm[hh].astype(BF16), vh[hh], preferred_element_type=F32) for hh in heads]
    den = [jnp.sum(sm[hh], axis=-1, keepdims=True)
           + inter[hh] * jnp.sum(qh[hh] * n_prev[hh], axis=-1, keepdims=True) for hh in heads]
    for hh in heads:
        cell = (intra[hh] + inter[hh] * carried[hh]) / jnp.maximum(jnp.abs(den[hh]), jnp.exp(-m_t[hh]))
        gated = jax.nn.sigmoid(o_ref[:, hh * dv:(hh + 1) * dv].astype(F32)) * cell
        ms = jnp.mean(gated * gated, axis=-1, keepdims=True)
        out_ref[:, hh * dv:(hh + 1) * dv] = (
            (gated * lax.rsqrt(ms + NORM_EPS)) * nw_ref[:, hh * dv:(hh + 1) * dv]).astype(BF16)
    for hh in heads:
        m_new = jnp.maximum(b_last[hh] + m_prev[hh], m_loc[hh])
        a_prev = jnp.exp(b_last[hh] + m_prev[hh] - m_new)
        a_loc = jnp.exp(m_loc[hh] - m_new)
        c_scr[hh] = a_prev * c_prev[hh] + a_loc * c_loc[hh]
        n_scr[hh] = a_prev * n_prev[hh] + a_loc * jnp.sum(kw[hh], axis=0, keepdims=True)
        m_scr[hh] = jnp.broadcast_to(m_new, (1, LANES))


def _mlstm(zc, z, gc, gr, conv_w, conv_b, nw, *, batch, seq, cps=1):
    T = zc.shape[0]
    L = cps * MLSTM_CHUNK
    nc = seq // L
    per = L // SUBLANES
    vcol = 3 * FOX_WIDTH // MLSTM_V_WIDTH
    return pl.pallas_call(
        functools.partial(_mlstm_body, cps=cps),
        grid=(batch, nc),
        in_specs=[
            pl.BlockSpec((L, 2 * MLSTM_QK_WIDTH), lambda b, c: (b * nc + c, 0)),
            pl.BlockSpec((SUBLANES, 2 * MLSTM_QK_WIDTH), lambda b, c: (jnp.maximum((b * nc + c) * per - 1, 0), 0)),
            pl.BlockSpec((L, MLSTM_V_WIDTH), lambda b, c: (b * nc + c, vcol)),
            pl.BlockSpec((L, MLSTM_V_WIDTH), lambda b, c: (b * nc + c, vcol + 1)),
            pl.BlockSpec((L, GATE_LANES), lambda b, c: (b * nc + c, 0)),
            pl.BlockSpec((GATE_ROWS, L), lambda b, c: (0, b * nc + c)),
            pl.BlockSpec((MLSTM_CONV, 2 * MLSTM_QK_WIDTH), lambda b, c: (0, 0)),
            pl.BlockSpec((1, 2 * MLSTM_QK_WIDTH), lambda b, c: (0, 0)),
            pl.BlockSpec((1, MLSTM_V_WIDTH), lambda b, c: (0, 0)),
        ],
        out_specs=pl.BlockSpec((L, MLSTM_V_WIDTH), lambda b, c: (b * nc + c, 0)),
        out_shape=jax.ShapeDtypeStruct((T, MLSTM_V_WIDTH), BF16),
        scratch_shapes=[
            pltpu.VMEM((SUBLANES + L, 2 * MLSTM_QK_WIDTH), F32),
            pltpu.VMEM((MLSTM_HEADS, MLSTM_QK_DIM, MLSTM_V_DIM), F32),
            pltpu.VMEM((MLSTM_HEADS, 1, MLSTM_QK_DIM), F32),
            pltpu.VMEM((MLSTM_HEADS, 1, LANES), F32),
        ],
        compiler_params=_cparams(("parallel", "arbitrary")),
        name="mlstm",
    )(zc, zc, z, z, gc, gr, conv_w, conv_b, nw)


def _outproj_body(att_ref, cell_ref, x_ref, wa_ref, wb_ref, n2_ref, x1_ref, h2t_ref):
    y = (jnp.dot(att_ref[...], wa_ref[...], preferred_element_type=F32)
         + jnp.dot(cell_ref[...], wb_ref[...], preferred_element_type=F32))
    x1 = x_ref[...] + y
    x1_ref[...] = x1
    ms = jnp.mean(x1 * x1, axis=-1, keepdims=True)
    h2 = (x1 * lax.rsqrt(ms + NORM_EPS)) * n2_ref[...]
    h2t_ref[...] = h2.T.astype(BF16)


def _outproj(att, cell, x2, wa, wb, n2, *, tm=512):
    T = x2.shape[0]
    return pl.pallas_call(
        _outproj_body,
        grid=(T // tm,),
        in_specs=[
            pl.BlockSpec((tm, FOX_WIDTH), lambda i: (i, 0)),
            pl.BlockSpec((tm, MLSTM_V_WIDTH), lambda i: (i, 0)),
            pl.BlockSpec((tm, D_MODEL), lambda i: (i, 0)),
            pl.BlockSpec((FOX_WIDTH, D_MODEL), lambda i: (0, 0)),
            pl.BlockSpec((MLSTM_V_WIDTH, D_MODEL), lambda i: (0, 0)),
            pl.BlockSpec((1, D_MODEL), lambda i: (0, 0)),
        ],
        out_specs=[
            pl.BlockSpec((tm, D_MODEL), lambda i: (i, 0)),
            pl.BlockSpec((D_MODEL, tm), lambda i: (0, i)),
        ],
        out_shape=[
            jax.ShapeDtypeStruct((T, D_MODEL), F32),
            jax.ShapeDtypeStruct((D_MODEL, T), BF16),
        ],
        compiler_params=_cparams(("parallel",)),
        name="outproj",
    )(att, cell, x2, wa, wb, n2)


def _row_iota(n_rows, tb):
    return lax.broadcasted_iota(jnp.int32, (n_rows, tb), 0).astype(F32)


def _extract16(s, val_ref, idx_ref):
    n_rows, tb = s.shape
    iota = _row_iota(n_rows, tb)

    def body(r, sw):
        mx = jnp.max(sw, axis=0, keepdims=True)
        first = jnp.min(jnp.where(sw == mx, iota, float(n_rows)), axis=0, keepdims=True)
        val_ref[pl.ds(r, 1), :] = mx
        idx_ref[pl.ds(r, 1), :] = first
        return jnp.where(iota == first, -jnp.inf, sw)

    lax.fori_loop(0, PEER_TOPK, body, s)


def _scatter_rounds(idx, values, n_rows, fill):
    tb = idx.shape[1]
    iota = _row_iota(n_rows, tb)
    table = jnp.full((n_rows, tb), fill, F32)
    for r in range(PEER_TOPK):
        v = values[r:r + 1, :] if values is not None else float(r)
        table = jnp.where(iota == idx[r:r + 1, :], v, table)
    return table


def _extract16_distinct(s, val_ref):
    def body(r, sw):
        mx = jnp.max(sw, axis=0, keepdims=True)
        val_ref[pl.ds(r, 1), :] = mx
        return jnp.where(sw == mx, -jnp.inf, sw)

    lax.fori_loop(0, PEER_TOPK, body, s)


def _count_ge(s, thr):
    return jnp.sum((s >= thr).astype(F32), axis=0, keepdims=True)


def _match_rounds(s, vals, values, fill):
    table = jnp.full(s.shape, fill, F32)
    for r in range(PEER_TOPK):
        v = values[r:r + 1, :] if values is not None else float(r)
        table = jnp.where(s == vals[r:r + 1, :], v, table)
    return table


def _peer_sel_body(h2t_ref, h2t_next_ref, wqt_ref, keys_ref, grp_ref, e1_ref, c1_ref, n0_ref, w0_ref,
                   qt_scr, sc_scr, vals_scr, idx_scr, cand_scr, pe_scr, *, tb):
    half = PEER_KEY_DIM // 2

    def queries(src_ref):
        qt_scr[...] = jnp.dot(wqt_ref[...], src_ref[...], preferred_element_type=F32).astype(BF16)

    @pl.when(pl.program_id(0) == 0)
    def _():
        queries(h2t_ref)

    for hp in range(2 * PEER_HEADS):
        sc_scr[hp] = jnp.dot(keys_ref[hp % 2], qt_scr[hp * half:(hp + 1) * half, :], preferred_element_type=F32)
    queries(h2t_next_ref)

    def head_tables(h, exact):
        s0 = sc_scr[2 * h]
        s1 = sc_scr[2 * h + 1]
        if exact:
            _extract16(s0, vals_scr.at[0], idx_scr.at[0])
            _extract16(s1, vals_scr.at[1], idx_scr.at[1])
        else:
            _extract16_distinct(s0, vals_scr.at[0])
            _extract16_distinct(s1, vals_scr.at[1])
        a = vals_scr[0]
        b = vals_scr[1]
        ea = jnp.exp(a - a[0:1, :])
        eb = jnp.exp(b - b[0:1, :])
        off = 0
        for r in range(PEER_TOPK):
            ncol = _CAND_COLS[r]
            cand_scr[off:off + ncol, :] = a[r:r + 1, :] + b[0:ncol, :]
            pe_scr[off:off + ncol, :] = ea[r:r + 1, :] * eb[0:ncol, :]
            off += ncol
        cand_scr[_N_CAND:_N_CAND_PAD, :] = jnp.full((_N_CAND_PAD - _N_CAND, tb), -jnp.inf, F32)
        pe_scr[_N_CAND:_N_CAND_PAD, :] = jnp.zeros((_N_CAND_PAD - _N_CAND, tb), F32)
        cand = cand_scr[...]
        last = PEER_TOPK - 1
        if exact:
            _extract16(cand, vals_scr.at[2], idx_scr.at[2])
            sel = _scatter_rounds(idx_scr[2], jnp.ones((PEER_TOPK, tb), F32), _N_CAND_PAD, 0.0)
            ties = None
        else:
            _extract16_distinct(cand, vals_scr.at[2])
            sel = (cand >= vals_scr[2, last:last + 1, :]).astype(F32)
            ties = ((_count_ge(s0, a[last:last + 1, :]) != float(PEER_TOPK)).astype(F32)
                    + (_count_ge(s1, b[last:last + 1, :]) != float(PEER_TOPK)).astype(F32)
                    + (jnp.sum(sel, axis=0, keepdims=True) != float(PEER_TOPK)).astype(F32))
        zsum = jnp.sum(sel * pe_scr[...], axis=0, keepdims=True)
        nr = jnp.dot(grp_ref[...], sel.astype(BF16), preferred_element_type=F32)
        if exact:
            n0 = _scatter_rounds(idx_scr[0], nr, PEER_N_KEYS, 0.0)
            c1 = _scatter_rounds(idx_scr[1], None, PEER_N_KEYS, float(PEER_TOPK))
        else:
            n0 = _match_rounds(s0, a, nr, 0.0)
            c1 = _match_rounds(s1, b, None, float(PEER_TOPK))
        n0_ref[h] = n0
        w0_ref[h] = jnp.exp(s0 - a[0:1, :]) * (0.5 / zsum)
        e1_ref[h] = jnp.exp(s1 - b[0:1, :]).astype(BF16)
        c1_ref[h] = c1.astype(BF16)
        return ties

    for h in range(PEER_HEADS):
        ties = head_tables(h, False)

        @pl.when(jnp.max(ties) > 0.0)
        def _():
            head_tables(h, True)


def _peer_sel(h2t, wqt, keys, grp, *, tb=256):
    T = h2t.shape[1]
    tab = jax.ShapeDtypeStruct((PEER_HEADS, PEER_N_KEYS, T), F32)
    tab16 = jax.ShapeDtypeStruct((PEER_HEADS, PEER_N_KEYS, T), BF16)
    tab_spec = pl.BlockSpec((PEER_HEADS, PEER_N_KEYS, tb), lambda i: (0, 0, i))
    n_blocks = T // tb
    return pl.pallas_call(
        functools.partial(_peer_sel_body, tb=tb),
        grid=(n_blocks,),
        in_specs=[
            pl.BlockSpec((D_MODEL, tb), lambda i: (0, 0)),
            pl.BlockSpec((D_MODEL, tb), lambda i: (0, jnp.minimum(i + 1, n_blocks - 1))),
            pl.BlockSpec((PEER_HEADS * PEER_KEY_DIM, D_MODEL), lambda i: (0, 0)),
            pl.BlockSpec((2, PEER_N_KEYS, PEER_KEY_DIM // 2), lambda i: (0, 0, 0)),
            pl.BlockSpec((PEER_TOPK, _N_CAND_PAD), lambda i: (0, 0)),
        ],
        out_specs=[tab_spec, tab_spec, tab_spec, tab_spec],
        out_shape=[tab16, tab16, tab, tab],
        scratch_shapes=[
            pltpu.VMEM((PEER_HEADS * PEER_KEY_DIM, tb), BF16),
            pltpu.VMEM((2 * PEER_HEADS, PEER_N_KEYS, tb), F32),
            pltpu.VMEM((3, PEER_TOPK, tb), F32),
            pltpu.VMEM((3, PEER_TOPK, tb), F32),
            pltpu.VMEM((_N_CAND_PAD, tb), F32),
            pltpu.VMEM((_N_CAND_PAD, tb), F32),
        ],
        compiler_params=_cparams(("arbitrary",)),
        name="peer_sel",
    )(h2t, h2t, wqt, keys, grp)


def _peer_dense_body(u_ref, vt_ref, h2t_ref, e1_ref, c1_ref, n0_ref, w0_ref, yt_ref, st_a, st_b, *, ec, tb, nk):
    k = pl.program_id(1)
    slabs = ec // PEER_N_KEYS

    def pre_activations(st_w):
        st_w[...] = jnp.dot(u_ref[...], h2t_ref[...], preferred_element_type=F32)

    def activate_and_project(st_r):
        parts = []
        for ii in range(slabs):
            g = jnp.zeros((PEER_N_KEYS, tb), BF16)
            for h in range(PEER_HEADS):
                n_row = n0_ref[h, ii:ii + 1, :].astype(BF16)
                w_row = w0_ref[h, ii:ii + 1, :].astype(BF16)
                g = g + jnp.where(c1_ref[h] < n_row, e1_ref[h] * w_row, jnp.zeros((), BF16))
            x = st_r[ii * PEER_N_KEYS:(ii + 1) * PEER_N_KEYS, :]
            act = x * (1.0 + lax.erf(x * math.sqrt(0.5)))
            parts.append(act.astype(BF16) * g)
        at = jnp.concatenate(parts, axis=0)
        yt_ref[...] += jnp.dot(vt_ref[...], at, preferred_element_type=F32)

    @pl.when(k == 0)
    def _():
        yt_ref[...] = jnp.zeros_like(yt_ref)
        pre_activations(st_a)

    @pl.when((k > 0) & (k < nk) & (k % 2 == 1))
    def _():
        pre_activations(st_b)
        activate_and_project(st_a)

    @pl.when((k > 0) & (k < nk) & (k % 2 == 0))
    def _():
        pre_activations(st_a)
        activate_and_project(st_b)

    @pl.when(k == nk)
    def _():
        activate_and_project(st_b if nk % 2 == 0 else st_a)


def _peer_dense(u, vt, h2t, e1, c1, n0, w0, *, tb=512, ec=1024):
    T = h2t.shape[1]
    nk = PEER_N_EXPERTS // ec
    slabs = ec // PEER_N_KEYS
    assert slabs == SUBLANES, "one f32 sublane tile of per-slab gate rows per expert chunk"
    tab_spec = pl.BlockSpec((PEER_HEADS, PEER_N_KEYS, tb), lambda i, k: (0, 0, i))
    row_spec = pl.BlockSpec((PEER_HEADS, slabs, tb), lambda i, k: (0, jnp.maximum(k - 1, 0), i))
    return pl.pallas_call(
        functools.partial(_peer_dense_body, ec=ec, tb=tb, nk=nk),
        grid=(T // tb, nk + 1),
        in_specs=[
            pl.BlockSpec((ec, D_MODEL), lambda i, k: (jnp.minimum(k, nk - 1), 0)),
            pl.BlockSpec((D_MODEL, ec), lambda i, k: (0, jnp.maximum(k - 1, 0))),
            pl.BlockSpec((D_MODEL, tb), lambda i, k: (0, i)),
            tab_spec, tab_spec, row_spec, row_spec,
        ],
        out_specs=pl.BlockSpec((D_MODEL, tb), lambda i, k: (0, i)),
        out_shape=jax.ShapeDtypeStruct((D_MODEL, T), F32),
        scratch_shapes=[pltpu.VMEM((ec, tb), F32), pltpu.VMEM((ec, tb), F32)],
        compiler_params=_cparams(("parallel", "arbitrary")),
        name="peer_dense",
    )(u, vt, h2t, e1, c1, n0, w0)


def _final_body(x1_ref, yt_ref, w_ref, o_ref):
    x2 = x1_ref[...] + yt_ref[...].T
    ms = jnp.mean(x2 * x2, axis=-1, keepdims=True)
    o_ref[...] = (x2 * lax.rsqrt(ms + NORM_EPS)) * w_ref[...]


def _final(x1, yt, w, *, tm=512):
    T = x1.shape[0]
    return pl.pallas_call(
        _final_body,
        grid=(T // tm,),
        in_specs=[
            pl.BlockSpec((tm, D_MODEL), lambda i: (i, 0)),
            pl.BlockSpec((D_MODEL, tm), lambda i: (0, i)),
            pl.BlockSpec((1, D_MODEL), lambda i: (0, 0)),
        ],
        out_specs=pl.BlockSpec((tm, D_MODEL), lambda i: (i, 0)),
        out_shape=jax.ShapeDtypeStruct((T, D_MODEL), F32),
        compiler_params=_cparams(("parallel",)),
        name="final",
    )(x1, yt, w)


def _group_matrix():
    g = np.zeros((PEER_TOPK, _N_CAND_PAD), np.float32)
    off = 0
    for r, ncol in enumerate(_CAND_COLS):
        g[r, off:off + ncol] = 1.0
        off += ncol
    return g


def _layer(x2, norm1_w, w_in, fox_f_bias, conv_w, conv_b, i_bias, f_bias, fox_nw, mlstm_nw, w_out, norm2_w,
           w_q, keys, u, v, *, batch, seq):
    splits = np.cumsum((FOX_WIDTH, FOX_WIDTH, FOX_WIDTH, FOX_HEADS, MLSTM_QK_WIDTH, MLSTM_QK_WIDTH,
                        MLSTM_V_WIDTH, MLSTM_HEADS, MLSTM_HEADS, MLSTM_V_WIDTH))[:-1]
    fq, fk, fv, ff, mq, mk, mv, mi, mf, mo = jnp.split(w_in, [int(p) for p in splits], axis=1)
    w_main = jnp.concatenate([fq, fk, fv, mv, mo, mq, mk], axis=1).astype(BF16)
    gate_pad = GATE_LANES - FOX_HEADS - 2 * MLSTM_HEADS
    w_gate = jnp.pad(jnp.concatenate([ff, mi, mf], axis=1), ((0, 0), (0, gate_pad))).astype(BF16)
    gate_bias = jnp.pad(jnp.concatenate([fox_f_bias, i_bias, f_bias]), (0, gate_pad)).reshape(1, GATE_LANES)
    tri = jnp.asarray(np.tril(np.ones((LANES, LANES), np.float32)), BF16)

    z, zc, g = _inproj(x2, norm1_w.reshape(1, D_MODEL), w_main, w_gate)
    pq, pk = _placement_matrices()
    qn, kn = _fox_norms(z, jnp.asarray(_head_group_matrix(), BF16))
    kmax = jnp.max(kn.reshape(batch, seq, GATE_LANES), axis=1)
    gc, gr, qx, kx = _gates(g, gate_bias, tri, jnp.asarray(pq, BF16), jnp.asarray(pk, BF16), qn,
                            jnp.repeat(kmax, SUBLANES, axis=0), batch=batch, seq=seq)
    lo, fast = _fox_plan(qn, kmax, gc, batch=batch, seq=seq, tk=FOX_KEY_BLOCK)
    att = _fox(z, qx, kx, fox_nw.reshape(FOX_HEADS, FOX_HEAD_DIM), lo, fast, batch=batch, seq=seq, tk=FOX_KEY_BLOCK)
    cell = _mlstm(zc, z, gc, gr, conv_w, conv_b.reshape(1, -1), mlstm_nw.reshape(1, -1), batch=batch, seq=seq)
    x1, h2t = _outproj(att, cell, x2, w_out[:FOX_WIDTH].astype(BF16), w_out[FOX_WIDTH:].astype(BF16),
                       norm2_w.reshape(1, D_MODEL))
    e1, c1, n0, w0 = _peer_sel(h2t, w_q.T.astype(BF16), keys.astype(BF16), jnp.asarray(_group_matrix(), BF16))
    yt = _peer_dense(u.astype(BF16), v.T.astype(BF16), h2t, e1, c1, n0, w0)
    return x1, yt


def kernel(x, norm1_w, w_in, fox_f_bias, mlstm_conv_w, mlstm_conv_b, mlstm_i_bias, mlstm_f_bias, fox_out_norm_w,
           mlstm_out_norm_w, w_out, norm2_w, peer_w_q, peer_keys, peer_u, peer_v, final_norm_w):
    batch, seq, _ = x.shape
    assert w_in.shape[0] == 1, "single-layer block: the final norm is fused with the last residual add"
    x2 = x.reshape(batch * seq, D_MODEL)
    x1, yt = _layer(x2, norm1_w[0], w_in[0], fox_f_bias[0], mlstm_conv_w[0], mlstm_conv_b[0], mlstm_i_bias[0],
                    mlstm_f_bias[0], fox_out_norm_w[0], mlstm_out_norm_w[0], w_out[0], norm2_w[0],
                    peer_w_q[0], peer_keys[0], peer_u[0], peer_v[0], batch=batch, seq=seq)
    out = _final(x1, yt, final_norm_w.reshape(1, D_MODEL))
    return out.reshape(batch, seq, D_MODEL)
```

```python
import functools
import math

import numpy as np
import jax
import jax.numpy as jnp
from jax import lax
from jax.experimental import pallas as pl
from jax.experimental.pallas import tpu as pltpu

F32 = jnp.float32
BF16 = jnp.bfloat16

D_MODEL = 2048
FOX_HEADS = 8
FOX_HEAD_DIM = 128
FOX_WIDTH = FOX_HEADS * FOX_HEAD_DIM
MLSTM_HEADS = 4
MLSTM_QK_DIM = 128
MLSTM_V_DIM = 256
MLSTM_QK_WIDTH = MLSTM_HEADS * MLSTM_QK_DIM
MLSTM_V_WIDTH = MLSTM_HEADS * MLSTM_V_DIM
MLSTM_CONV = 4
MLSTM_CHUNK = 128
PEER_HEADS = 8
PEER_KEY_DIM = 256
PEER_N_KEYS = 128
PEER_TOPK = 16
PEER_N_EXPERTS = PEER_N_KEYS * PEER_N_KEYS
NORM_EPS = 1e-6
LOG2E = math.log2(math.e)
NORM_MARGIN = 1.0 + 2.0 ** -7
F32_ZERO_LOG2 = 150.0
FAST_GAP_LOG2 = 90.0
BOUND_SLACK_LOG2 = 4.0
FOX_KEY_BLOCK = 512

LANES = 128
SUBLANES = 8
GATE_LANES = LANES
GATE_ROWS = 16
VMEM_LIMIT = 56 * 1024 * 1024

_CAND_COLS = tuple(PEER_TOPK // (r + 1) for r in range(PEER_TOPK))
_N_CAND = sum(_CAND_COLS)
_N_CAND_PAD = -(-_N_CAND // SUBLANES) * SUBLANES


def _cparams(sem, flags=None):
    return pltpu.CompilerParams(dimension_semantics=sem, vmem_limit_bytes=VMEM_LIMIT, flags=flags)


def _inproj_body(x_ref, nw_ref, w_ref, wg_ref, z_ref, zc_ref, g_ref, h_scr, *, n_main, q_blocks, q_scale):
    j = pl.program_id(1)

    @pl.when(j == 0)
    def _():
        x = x_ref[...]
        ms = jnp.mean(x * x, axis=-1, keepdims=True)
        hb = ((x * lax.rsqrt(ms + NORM_EPS)) * nw_ref[...]).astype(BF16)
        h_scr[...] = hb
        g_ref[...] = jnp.dot(hb, wg_ref[...], preferred_element_type=F32)

    z = jnp.dot(h_scr[...], w_ref[...], preferred_element_type=F32)

    @pl.when(j < n_main)
    def _():
        scale = jnp.where(j < q_blocks, q_scale, 1.0).astype(F32)
        z_ref[...] = (z * scale).astype(BF16)

    @pl.when(j >= n_main)
    def _():
        zc_ref[...] = z


def _inproj(x2, norm_w, w_main, w_gate, *, tm=1024, tn=1024):
    T = x2.shape[0]
    n_cols = w_main.shape[1]
    n_conv = 2 * MLSTM_QK_WIDTH
    n_main = (n_cols - n_conv) // tn
    n_blocks = n_cols // tn
    body = functools.partial(_inproj_body, n_main=n_main, q_blocks=FOX_WIDTH // tn,
                             q_scale=FOX_HEAD_DIM ** -0.5 * LOG2E)
    return pl.pallas_call(
        body,
        grid=(T // tm, n_blocks),
        in_specs=[
            pl.BlockSpec((tm, D_MODEL), lambda i, j: (i, 0)),
            pl.BlockSpec((1, D_MODEL), lambda i, j: (0, 0)),
            pl.BlockSpec((D_MODEL, tn), lambda i, j: (0, j)),
            pl.BlockSpec((D_MODEL, GATE_LANES), lambda i, j: (0, 0)),
        ],
        out_specs=[
            pl.BlockSpec((tm, tn), lambda i, j: (i, jnp.minimum(j, n_main - 1))),
            pl.BlockSpec((tm, tn), lambda i, j: (i, jnp.maximum(j - n_main, 0))),
            pl.BlockSpec((tm, GATE_LANES), lambda i, j: (i, 0)),
        ],
        out_shape=[
            jax.ShapeDtypeStruct((T, n_cols - n_conv), BF16),
            jax.ShapeDtypeStruct((T, n_conv), F32),
            jax.ShapeDtypeStruct((T, GATE_LANES), F32),
        ],
        scratch_shapes=[pltpu.VMEM((tm, D_MODEL), BF16)],
        compiler_params=_cparams(("parallel", "arbitrary")),
        name="inproj",
    )(x2, norm_w, w_main, w_gate)


def _split3(v):
    hi = v.astype(BF16)
    r1 = v - hi.astype(F32)
    mid = r1.astype(BF16)
    lo = (r1 - mid.astype(F32)).astype(BF16)
    return hi, mid, lo


def _fox_norms_body(q_ref, k_ref, grp_ref, qn_ref, kn_ref):
    for src, dst in ((q_ref, qn_ref), (k_ref, kn_ref)):
        x = src[...].astype(F32)
        ss = jnp.dot((x * x).astype(BF16), grp_ref[...], preferred_element_type=F32)
        dst[...] = jnp.sqrt(ss) * NORM_MARGIN


def _head_group_matrix():
    g = np.zeros((FOX_WIDTH, GATE_LANES), np.float32)
    for h in range(FOX_HEADS):
        g[h * FOX_HEAD_DIM:(h + 1) * FOX_HEAD_DIM, h] = 1.0
    return g


def _fox_norms(z, grp, *, rows=1024):
    T = z.shape[0]
    out = jax.ShapeDtypeStruct((T, GATE_LANES), F32)
    return pl.pallas_call(
        _fox_norms_body,
        grid=(T // rows,),
        in_specs=[
            pl.BlockSpec((rows, FOX_WIDTH), lambda i: (i, 0)),
            pl.BlockSpec((rows, FOX_WIDTH), lambda i: (i, 1)),
            pl.BlockSpec((FOX_WIDTH, GATE_LANES), lambda i: (0, 0)),
        ],
        out_specs=[pl.BlockSpec((rows, GATE_LANES), lambda i: (i, 0))] * 2,
        out_shape=[out, out],
        compiler_params=_cparams(("parallel",)),
        name="fox_norms",
    )(z, z, grp)


def _gates_body(g_ref, bias_ref, tri_ref, pq_ref, pk_ref, qn_ref, kmax_ref, gc_ref, gr_ref, qx_ref, kx_ref,
                carry_scr, *, rows):
    c = pl.program_id(1)

    @pl.when(c == 0)
    def _():
        carry_scr[...] = jnp.zeros_like(carry_scr)

    lane = lax.broadcasted_iota(jnp.int32, (LANES, GATE_LANES), 1)
    is_glob = lane < FOX_HEADS
    is_ls = is_glob | ((lane >= FOX_HEADS + MLSTM_HEADS) & (lane < FOX_HEADS + 2 * MLSTM_HEADS))
    tri = tri_ref[...]
    for s in range(rows // LANES):
        sl = slice(s * LANES, (s + 1) * LANES)
        g = g_ref[sl, :] + bias_ref[...]
        ls = jnp.minimum(g, 0.0) - jnp.log1p(jnp.exp(-jnp.abs(g)))
        v = jnp.where(is_ls, ls, 0.0)
        hi, mid, lo = _split3(v)
        cs = (jnp.dot(tri, hi, preferred_element_type=F32)
              + jnp.dot(tri, mid, preferred_element_type=F32)
              + jnp.dot(tri, lo, preferred_element_type=F32))
        glob = cs + carry_scr[...]
        carry_scr[...] = glob[LANES - 1:LANES, :]
        out = jnp.where(is_glob, glob, jnp.where(is_ls, cs, g))
        gc_ref[sl, :] = out
        gr_ref[:, sl] = out.T[0:GATE_ROWS, :]
        groups = (*_split3(glob * LOG2E), jnp.ones((LANES, GATE_LANES), BF16),
                  *_split3(qn_ref[sl, :] * kmax_ref[0:1, :] + 1.0))
        pieces = jnp.zeros((LANES, GATE_LANES), F32)
        for gi, grp in enumerate(groups):
            moved = pltpu.roll(grp.astype(F32), gi * FOX_HEADS, axis=1) if gi else grp.astype(F32)
            pieces = jnp.where((lane >= gi * FOX_HEADS) & (lane < (gi + 1) * FOX_HEADS), moved, pieces)
        pieces = pieces.astype(BF16)
        qx_ref[sl, :] = jnp.dot(pieces, pq_ref[...], preferred_element_type=F32).astype(BF16)
        kx_ref[sl, :] = jnp.dot(pieces, pk_ref[...], preferred_element_type=F32).astype(BF16)


_N_PIECES = 3
_PIECE_GROUPS = 2 * _N_PIECES + 1
_STAB_SLOT = 2 * _N_PIECES


def _placement_matrices():
    n = _N_PIECES
    assert _PIECE_GROUPS * FOX_HEADS <= GATE_LANES
    pq = np.zeros((GATE_LANES, FOX_WIDTH), np.float32)
    pk = np.zeros((GATE_LANES, FOX_WIDTH), np.float32)
    for h in range(FOX_HEADS):
        for p in range(n):
            pq[p * FOX_HEADS + h, h * FOX_HEAD_DIM + p] = 1.0
            pk[n * FOX_HEADS + h, h * FOX_HEAD_DIM + p] = 1.0
            pq[n * FOX_HEADS + h, h * FOX_HEAD_DIM + n + p] = 1.0
            pk[p * FOX_HEADS + h, h * FOX_HEAD_DIM + n + p] = -1.0
            pq[(n + 1 + p) * FOX_HEADS + h, h * FOX_HEAD_DIM + _STAB_SLOT + p] = -1.0
            pk[n * FOX_HEADS + h, h * FOX_HEAD_DIM + _STAB_SLOT + p] = 1.0
    return pq, pk


def _gates(g, bias, tri, pq, pk, qn, kmax, *, batch, seq, rows=1024):
    T = g.shape[0]
    nblk = seq // rows
    return pl.pallas_call(
        functools.partial(_gates_body, rows=rows),
        grid=(batch, nblk),
        in_specs=[
            pl.BlockSpec((rows, GATE_LANES), lambda b, c: (b * nblk + c, 0)),
            pl.BlockSpec((1, GATE_LANES), lambda b, c: (0, 0)),
            pl.BlockSpec((LANES, LANES), lambda b, c: (0, 0)),
            pl.BlockSpec((GATE_LANES, FOX_WIDTH), lambda b, c: (0, 0)),
            pl.BlockSpec((GATE_LANES, FOX_WIDTH), lambda b, c: (0, 0)),
            pl.BlockSpec((rows, GATE_LANES), lambda b, c: (b * nblk + c, 0)),
            pl.BlockSpec((SUBLANES, GATE_LANES), lambda b, c: (b, 0)),
        ],
        out_specs=[
            pl.BlockSpec((rows, GATE_LANES), lambda b, c: (b * nblk + c, 0)),
            pl.BlockSpec((GATE_ROWS, rows), lambda b, c: (0, b * nblk + c)),
            pl.BlockSpec((rows, FOX_WIDTH), lambda b, c: (b * nblk + c, 0)),
            pl.BlockSpec((rows, FOX_WIDTH), lambda b, c: (b * nblk + c, 0)),
        ],
        out_shape=[
            jax.ShapeDtypeStruct((T, GATE_LANES), F32),
            jax.ShapeDtypeStruct((GATE_ROWS, T), F32),
            jax.ShapeDtypeStruct((T, FOX_WIDTH), BF16),
            jax.ShapeDtypeStruct((T, FOX_WIDTH), BF16),
        ],
        scratch_shapes=[pltpu.VMEM((1, GATE_LANES), F32)],
        compiler_params=_cparams(("parallel", "arbitrary")),
        name="gates",
    )(g, bias, tri, pq, pk, qn, kmax)


def _fox_body(lo_ref, fast_ref, q_ref, qx_ref, k_ref, kx_ref, v_ref, nw_ref, o_ref, *, tk, nq):
    b = pl.program_id(0)
    h = pl.program_id(1)
    qi = pl.program_id(2)
    tile = (b * FOX_HEADS + h) * nq + qi
    lo = lo_ref[tile]
    row = lax.broadcasted_iota(jnp.int32, (tk, tk), 0)
    col = lax.broadcasted_iota(jnp.int32, (tk, tk), 1)

    def load_kv(ki):
        start = pl.multiple_of(ki * tk, tk)
        k = jnp.concatenate([k_ref[pl.ds(start, tk), :], kx_ref[pl.ds(start, tk), :]], axis=1)
        return k, v_ref[pl.ds(start, tk), :]

    def logits(q, k, masked):
        s = lax.dot_general(q, k, (((1,), (1,)), ((), ())), preferred_element_type=F32)
        return jnp.where(row >= col, s, -jnp.inf) if masked else s

    def finish(r, num, den):
        out = num / den
        ms = jnp.mean(out * out, axis=-1, keepdims=True)
        o_ref[r * tk:(r + 1) * tk, :] = ((out * lax.rsqrt(ms + NORM_EPS)) * nw_ref[pl.ds(h, 1), :]).astype(BF16)

    @pl.when(fast_ref[tile] == 1)
    def _():
        q_halves = [jnp.concatenate([q_ref[r * tk:(r + 1) * tk, :], qx_ref[r * tk:(r + 1) * tk, :]], axis=1)
                    for r in range(2)]
        ones_col = (lax.broadcasted_iota(jnp.int32, (tk, LANES), 1) == 0).astype(BF16)

        def weights(r, k, masked):
            return jnp.exp2(logits(q_halves[r], k, masked)).astype(BF16)

        def values(ki):
            return jnp.concatenate([load_kv(ki)[1], ones_col], axis=1)

        def add(acc, p, va):
            return acc + jnp.dot(p, va, preferred_element_type=F32)

        def pair(kp, accs):
            blocks = [2 * kp, 2 * kp + 1]
            ps = [[weights(r, load_kv(ki)[0], False) for r in range(2)] for ki in blocks]
            for j, ki in enumerate(blocks):
                va = values(ki)
                accs = tuple(add(accs[r], ps[j][r], va) for r in range(2))
            return accs

        zero = jnp.zeros((tk, 2 * FOX_HEAD_DIM), F32)
        aa, ab = lax.fori_loop(lo // 2, qi, pair, (zero, zero))
        k = load_kv(2 * qi)[0]
        pa, pb = weights(0, k, True), weights(1, k, False)
        pb2 = weights(1, load_kv(2 * qi + 1)[0], True)
        va = values(2 * qi)
        aa, ab = add(aa, pa, va), add(ab, pb, va)
        ab = add(ab, pb2, values(2 * qi + 1))
        for r, acc in enumerate((aa, ab)):
            finish(r, acc[:, 0:FOX_HEAD_DIM], acc[:, FOX_HEAD_DIM:FOX_HEAD_DIM + 1])

    @pl.when(fast_ref[tile] == 0)
    def _():
        _fox_online(q_ref, qx_ref, load_kv, logits, finish, lo, qi, tk)


def _fox_online(q_ref, qx_ref, load_kv, logits, finish, lo, qi, tk):
    lane = lax.broadcasted_iota(jnp.int32, (tk, LANES), 1)
    no_stab = (lane < _STAB_SLOT) | (lane >= _STAB_SLOT + _N_PIECES)
    q_halves = [jnp.concatenate([q_ref[r * tk:(r + 1) * tk, :],
                                 jnp.where(no_stab, qx_ref[r * tk:(r + 1) * tk, :], jnp.zeros((), BF16))], axis=1)
                for r in range(2)]

    def weights(s, carry):
        m, l, acc = carry
        cols = [s[:, c * LANES:(c + 1) * LANES] for c in range(tk // LANES)]
        m_new = jnp.maximum(m, jnp.max(functools.reduce(jnp.maximum, cols), axis=-1, keepdims=True))
        alpha = jnp.exp2(m - m_new)
        ps = [jnp.exp2(c - m_new) for c in cols]
        l = alpha * l + functools.reduce(jnp.add, ps)
        return m_new, l, alpha * acc, jnp.concatenate(ps, axis=1).astype(BF16)

    def update(q, k, v, carry, masked):
        m, l, acc, p = weights(logits(q, k, masked), carry)
        return m, l, acc + jnp.dot(p, v, preferred_element_type=F32)

    def both(ki, carries):
        k, v = load_kv(ki)
        s = [logits(q_halves[r], k, False) for r in range(2)]
        w = [weights(s[r], carries[r]) for r in range(2)]
        return tuple((m, l, acc + jnp.dot(p, v, preferred_element_type=F32)) for m, l, acc, p in w)

    init = (jnp.full((tk, LANES), -jnp.inf, F32), jnp.zeros((tk, LANES), F32), jnp.zeros((tk, FOX_HEAD_DIM), F32))
    ca, cb = lax.fori_loop(lo, 2 * qi, both, (init, init))
    k, v = load_kv(2 * qi)
    ca = update(q_halves[0], k, v, ca, True)
    cb = update(q_halves[1], k, v, cb, False)
    k, v = load_kv(2 * qi + 1)
    cb = update(q_halves[1], k, v, cb, True)
    for r, (_, l, acc) in enumerate((ca, cb)):
        finish(r, acc, jnp.sum(l, axis=-1, keepdims=True))


def _fox_plan(qn, kmax, gc, *, batch, seq, tk):
    tq = 2 * tk
    nq = seq // tq
    nk = seq // tk
    heads = slice(0, FOX_HEADS)
    qmax = jnp.max(qn.reshape(batch, nq, tq, GATE_LANES), axis=2)[..., heads]
    gap = 2.0 * qmax * kmax[:, None, heads] + BOUND_SLACK_LOG2
    f2 = (gc[:, heads] * LOG2E).reshape(batch, nk, tk, FOX_HEADS)
    f_first = f2[:, ::2, 0, :]
    f_last = f2[:, :, tk - 1, :]
    bound = gap[:, :, None, :] + f_first[:, :, None, :] - f_last[:, None, :, :]
    below_diag = jnp.arange(nk)[None, :, None] < 2 * jnp.arange(nq)[:, None, None]
    skip = (bound < -F32_ZERO_LOG2) & below_diag[None]
    lo = jnp.sum(jnp.cumprod(skip.astype(jnp.int32), axis=2), axis=2)
    fast = (gap <= FAST_GAP_LOG2).astype(jnp.int32)
    flat = lambda a: jnp.transpose(a, (0, 2, 1)).reshape(-1)
    return flat(lo), flat(fast)


def _fox(z, qx, kx, nw, lo, fast, *, batch, seq, tk=512):
    T = z.shape[0]
    tq = 2 * tk
    nq = seq // tq
    kcol = FOX_WIDTH // FOX_HEAD_DIM
    grid_spec = pltpu.PrefetchScalarGridSpec(
        num_scalar_prefetch=2,
        grid=(batch, FOX_HEADS, nq),
        in_specs=[
            pl.BlockSpec((tq, FOX_HEAD_DIM), lambda b, h, i, lo, fast: (b * nq + i, h)),
            pl.BlockSpec((tq, FOX_HEAD_DIM), lambda b, h, i, lo, fast: (b * nq + i, h)),
            pl.BlockSpec((seq, FOX_HEAD_DIM), lambda b, h, i, lo, fast: (b, kcol + h)),
            pl.BlockSpec((seq, FOX_HEAD_DIM), lambda b, h, i, lo, fast: (b, h)),
            pl.BlockSpec((seq, FOX_HEAD_DIM), lambda b, h, i, lo, fast: (b, 2 * kcol + h)),
            pl.BlockSpec((FOX_HEADS, FOX_HEAD_DIM), lambda b, h, i, lo, fast: (0, 0)),
        ],
        out_specs=pl.BlockSpec((tq, FOX_HEAD_DIM), lambda b, h, i, lo, fast: (b * nq + i, h)),
    )
    return pl.pallas_call(
        functools.partial(_fox_body, tk=tk, nq=nq),
        grid_spec=grid_spec,
        out_shape=jax.ShapeDtypeStruct((T, FOX_WIDTH), BF16),
        compiler_params=_cparams(("parallel", "parallel", "arbitrary")),
        name="fox",
    )(lo, fast, z, qx, z, kx, z, nw)


def _mlstm_body(zc_ref, zprev_ref, v_ref, o_ref, gc_ref, gr_ref, cw_ref, cb_ref, nw_ref, out_ref,
                full_scr, c_scr, n_scr, m_scr, *, cps):
    c = pl.program_id(1)
    L = MLSTM_CHUNK
    dk = MLSTM_QK_DIM
    dv = MLSTM_V_DIM
    rows = cps * L

    @pl.when(c == 0)
    def _():
        c_scr[...] = jnp.zeros_like(c_scr)
        n_scr[...] = jnp.zeros_like(n_scr)
        m_scr[...] = jnp.zeros_like(m_scr)

    full_scr[0:SUBLANES, :] = jnp.where(c == 0, 0.0, zprev_ref[...])
    full_scr[SUBLANES:SUBLANES + rows, :] = zc_ref[...]
    y = cb_ref[...]
    for j in range(MLSTM_CONV):
        y = y + cw_ref[j:j + 1, :] * full_scr[pl.ds(SUBLANES - (MLSTM_CONV - 1) + j, rows), :]
    qk_all = y * jax.nn.sigmoid(y)

    row = lax.broadcasted_iota(jnp.int32, (L, L), 0)
    col = lax.broadcasted_iota(jnp.int32, (L, L), 1)
    causal = row >= col
    assert cps == 1
    heads = range(MLSTM_HEADS)
    gcb = gc_ref[...]
    grb = gr_ref[...]
    qh = [qk_all[:, hh * dk:(hh + 1) * dk] for hh in heads]
    kh = [qk_all[:, MLSTM_QK_WIDTH + hh * dk:MLSTM_QK_WIDTH + (hh + 1) * dk] * (dk ** -0.5) for hh in heads]
    vh = [v_ref[:, hh * dv:(hh + 1) * dv] for hh in heads]
    i_col = [gcb[:, FOX_HEADS + hh:FOX_HEADS + hh + 1] for hh in heads]
    b_col = [gcb[:, FOX_HEADS + MLSTM_HEADS + hh:FOX_HEADS + MLSTM_HEADS + hh + 1] for hh in heads]
    i_row = [grb[FOX_HEADS + hh:FOX_HEADS + hh + 1, :] for hh in heads]
    b_row = [grb[FOX_HEADS + MLSTM_HEADS + hh:FOX_HEADS + MLSTM_HEADS + hh + 1, :] for hh in heads]
    b_last = [b_row[hh][:, L - 1:L] for hh in heads]
    c_prev = [c_scr[hh] for hh in heads]
    n_prev = [n_scr[hh] for hh in heads]
    m_prev = [m_scr[hh][:, 0:1] for hh in heads]

    m_loc = [jnp.max(b_last[hh] - b_row[hh] + i_row[hh], axis=-1, keepdims=True) for hh in heads]
    kw = [kh[hh] * jnp.exp(b_last[hh] - b_col[hh] + i_col[hh] - m_loc[hh]) for hh in heads]
    g_col = [b_col[hh] + m_prev[hh] for hh in heads]
    dmat = [jnp.where(causal, b_col[hh] - b_row[hh] + i_row[hh], -jnp.inf) for hh in heads]
    m_t = [jnp.maximum(g_col[hh], jnp.max(dmat[hh], axis=-1, keepdims=True)) for hh in heads]
    decay = [jnp.exp(dmat[hh] - m_t[hh]) for hh in heads]
    inter = [jnp.exp(g_col[hh] - m_t[hh]) for hh in heads]
    qb = [qh[hh].astype(BF16) for hh in heads]
    qk = [lax.dot_general(qb[hh], kh[hh].astype(BF16), (((1,), (1,)), ((), ())), preferred_element_type=F32)
          for hh in heads]
    carried = [jnp.dot(qb[hh], c_prev[hh].astype(BF16), preferred_element_type=F32) for hh in heads]
    c_loc = [lax.dot_general(kw[hh].astype(BF16), vh[hh], (((0,), (0,)), ((), ())), preferred_element_type=F32)
             for hh in heads]
    sm = [qk[hh] * decay[hh] for hh in heads]
    intra = [jnp.dot(sm[hh].astype(BF16), vh[hh], preferred_element_type=F32) for hh in heads]
    den = [jnp.sum(sm[hh], axis=-1, keepdims=True)
           + inter[hh] * jnp.sum(qh[hh] * n_prev[hh], axis=-1, keepdims=True) for hh in heads]
    for hh in heads:
        cell = (intra[hh] + inter[hh] * carried[hh]) / jnp.maximum(jnp.abs(den[hh]), jnp.exp(-m_t[hh]))
        gated = jax.nn.sigmoid(o_ref[:, hh * dv:(hh + 1) * dv].astype(F32)) * cell
        ms = jnp.mean(gated * gated, axis=-1, keepdims=True)
        out_ref[:, hh * dv:(hh + 1) * dv] = (
            (gated * lax.rsqrt(ms + NORM_EPS)) * nw_ref[:, hh * dv:(hh + 1) * dv]).astype(BF16)
    for hh in heads:
        m_new = jnp.maximum(b_last[hh] + m_prev[hh], m_loc[hh])
        a_prev = jnp.exp(b_last[hh] + m_prev[hh] - m_new)
        a_loc = jnp.exp(m_loc[hh] - m_new)
        c_scr[hh] = a_prev * c_prev[hh] + a_loc * c_loc[hh]
        n_scr[hh] = a_prev * n_prev[hh] + a_loc * jnp.sum(kw[hh], axis=0, keepdims=True)
        m_scr[hh] = jnp.broadcast_to(m_new, (1, LANES))


def _mlstm(zc, z, gc, gr, conv_w, conv_b, nw, *, batch, seq, cps=1):
    T = zc.shape[0]
    L = cps * MLSTM_CHUNK
    nc = seq // L
    per = L // SUBLANES
    vcol = 3 * FOX_WIDTH // MLSTM_V_WIDTH
    return pl.pallas_call(
        functools.partial(_mlstm_body, cps=cps),
        grid=(batch, nc),
        in_specs=[
            pl.BlockSpec((L, 2 * MLSTM_QK_WIDTH), lambda b, c: (b * nc + c, 0)),
            pl.BlockSpec((SUBLANES, 2 * MLSTM_QK_WIDTH), lambda b, c: (jnp.maximum((b * nc + c) * per - 1, 0), 0)),
            pl.BlockSpec((L, MLSTM_V_WIDTH), lambda b, c: (b * nc + c, vcol)),
            pl.BlockSpec((L, MLSTM_V_WIDTH), lambda b, c: (b * nc + c, vcol + 1)),
            pl.BlockSpec((L, GATE_LANES), lambda b, c: (b * nc + c, 0)),
            pl.BlockSpec((GATE_ROWS, L), lambda b, c: (0, b * nc + c)),
            pl.BlockSpec((MLSTM_CONV, 2 * MLSTM_QK_WIDTH), lambda b, c: (0, 0)),
            pl.BlockSpec((1, 2 * MLSTM_QK_WIDTH), lambda b, c: (0, 0)),
            pl.BlockSpec((1, MLSTM_V_WIDTH), lambda b, c: (0, 0)),
        ],
        out_specs=pl.BlockSpec((L, MLSTM_V_WIDTH), lambda b, c: (b * nc + c, 0)),
        out_shape=jax.ShapeDtypeStruct((T, MLSTM_V_WIDTH), BF16),
        scratch_shapes=[
            pltpu.VMEM((SUBLANES + L, 2 * MLSTM_QK_WIDTH), F32),
            pltpu.VMEM((MLSTM_HEADS, MLSTM_QK_DIM, MLSTM_V_DIM), F32),
            pltpu.VMEM((MLSTM_HEADS, 1, MLSTM_QK_DIM), F32),
            pltpu.VMEM((MLSTM_HEADS, 1, LANES), F32),
        ],
        compiler_params=_cparams(("parallel", "arbitrary")),
        name="mlstm",
    )(zc, zc, z, z, gc, gr, conv_w, conv_b, nw)


def _outproj_body(att_ref, cell_ref, x_ref, wa_ref, wb_ref, n2_ref, x1_ref, h2t_ref):
    y = (jnp.dot(att_ref[...], wa_ref[...], preferred_element_type=F32)
         + jnp.dot(cell_ref[...], wb_ref[...], preferred_element_type=F32))
    x1 = x_ref[...] + y
    x1_ref[...] = x1
    ms = jnp.mean(x1 * x1, axis=-1, keepdims=True)
    h2 = (x1 * lax.rsqrt(ms + NORM_EPS)) * n2_ref[...]
    h2t_ref[...] = h2.T.astype(BF16)


def _outproj(att, cell, x2, wa, wb, n2, *, tm=512):
    T = x2.shape[0]
    return pl.pallas_call(
        _outproj_body,
        grid=(T // tm,),
        in_specs=[
            pl.BlockSpec((tm, FOX_WIDTH), lambda i: (i, 0)),
            pl.BlockSpec((tm, MLSTM_V_WIDTH), lambda i: (i, 0)),
            pl.BlockSpec((tm, D_MODEL), lambda i: (i, 0)),
            pl.BlockSpec((FOX_WIDTH, D_MODEL), lambda i: (0, 0)),
            pl.BlockSpec((MLSTM_V_WIDTH, D_MODEL), lambda i: (0, 0)),
            pl.BlockSpec((1, D_MODEL), lambda i: (0, 0)),
        ],
        out_specs=[
            pl.BlockSpec((tm, D_MODEL), lambda i: (i, 0)),
            pl.BlockSpec((D_MODEL, tm), lambda i: (0, i)),
        ],
        out_shape=[
            jax.ShapeDtypeStruct((T, D_MODEL), F32),
            jax.ShapeDtypeStruct((D_MODEL, T), BF16),
        ],
        compiler_params=_cparams(("parallel",)),
        name="outproj",
    )(att, cell, x2, wa, wb, n2)


def _row_iota(n_rows, tb):
    return lax.broadcasted_iota(jnp.int32, (n_rows, tb), 0).astype(F32)


def _extract16(s, val_ref, idx_ref):
    n_rows, tb = s.shape
    iota = _row_iota(n_rows, tb)

    def body(r, sw):
        mx = jnp.max(sw, axis=0, keepdims=True)
        first = jnp.min(jnp.where(sw == mx, iota, float(n_rows)), axis=0, keepdims=True)
        val_ref[pl.ds(r, 1), :] = mx
        idx_ref[pl.ds(r, 1), :] = first
        return jnp.where(iota == first, -jnp.inf, sw)

    lax.fori_loop(0, PEER_TOPK, body, s)


def _scatter_rounds(idx, values, n_rows, fill):
    tb = idx.shape[1]
    iota = _row_iota(n_rows, tb)
    table = jnp.full((n_rows, tb), fill, F32)
    for r in range(PEER_TOPK):
        v = values[r:r + 1, :] if values is not None else float(r)
        table = jnp.where(iota == idx[r:r + 1, :], v, table)
    return table


def _extract16_distinct(arrays, val_refs):
    def body(r, carry):
        out = []
        for sw, val_ref in zip(carry, val_refs):
            mx = jnp.max(sw, axis=0, keepdims=True)
            val_ref[pl.ds(r, 1), :] = mx
            out.append(jnp.where(sw == mx, -jnp.inf, sw))
        return tuple(out)

    lax.fori_loop(0, PEER_TOPK, body, tuple(arrays))


def _count_ge(s, thr):
    return jnp.sum((s >= thr).astype(F32), axis=0, keepdims=True)


def _match_rounds(s, vals, values, fill):
    table = jnp.full(s.shape, fill, F32)
    for r in range(PEER_TOPK):
        v = values[r:r + 1, :] if values is not None else float(r)
        table = jnp.where(s == vals[r:r + 1, :], v, table)
    return table


def _peer_sel_body(h2t_ref, h2t_next_ref, wqt_ref, keys_ref, grp_ref, e1_ref, c1_ref, n0_ref, w0_ref,
                   qt_scr, sc_scr, vals_scr, idx_scr, cand_scr, pe_scr, *, tb):
    half = PEER_KEY_DIM // 2

    def queries(src_ref):
        qt_scr[...] = jnp.dot(wqt_ref[...], src_ref[...], preferred_element_type=F32).astype(BF16)

    @pl.when(pl.program_id(0) == 0)
    def _():
        queries(h2t_ref)

    for hp in range(2 * PEER_HEADS):
        sc_scr[hp] = jnp.dot(keys_ref[hp % 2], qt_scr[hp * half:(hp + 1) * half, :], preferred_element_type=F32)
    queries(h2t_next_ref)

    def head_tables(h, exact):
        s0 = sc_scr[2 * h]
        s1 = sc_scr[2 * h + 1]
        if exact:
            _extract16(s0, vals_scr.at[0], idx_scr.at[0])
            _extract16(s1, vals_scr.at[1], idx_scr.at[1])
        else:
            _extract16_distinct([s0], [vals_scr.at[0]])
            _extract16_distinct([s1], [vals_scr.at[1]])
        a = vals_scr[0]
        b = vals_scr[1]
        ea = jnp.exp(a - a[0:1, :])
        eb = jnp.exp(b - b[0:1, :])
        off = 0
        for r in range(PEER_TOPK):
            ncol = _CAND_COLS[r]
            cand_scr[off:off + ncol, :] = a[r:r + 1, :] + b[0:ncol, :]
            pe_scr[off:off + ncol, :] = ea[r:r + 1, :] * eb[0:ncol, :]
            off += ncol
        cand_scr[_N_CAND:_N_CAND_PAD, :] = jnp.full((_N_CAND_PAD - _N_CAND, tb), -jnp.inf, F32)
        pe_scr[_N_CAND:_N_CAND_PAD, :] = jnp.zeros((_N_CAND_PAD - _N_CAND, tb), F32)
        cand = cand_scr[...]
        last = PEER_TOPK - 1
        if exact:
            _extract16(cand, vals_scr.at[2], idx_scr.at[2])
            sel = _scatter_rounds(idx_scr[2], jnp.ones((PEER_TOPK, tb), F32), _N_CAND_PAD, 0.0)
            ties = None
        else:
            _extract16_distinct([cand], [vals_scr.at[2]])
            sel = (cand >= vals_scr[2, last:last + 1, :]).astype(F32)
            ties = ((_count_ge(s0, a[last:last + 1, :]) != float(PEER_TOPK)).astype(F32)
                    + (_count_ge(s1, b[last:last + 1, :]) != float(PEER_TOPK)).astype(F32)
                    + (jnp.sum(sel, axis=0, keepdims=True) != float(PEER_TOPK)).astype(F32))
        zsum = jnp.sum(sel * pe_scr[...], axis=0, keepdims=True)
        nr = jnp.dot(grp_ref[...], sel.astype(BF16), preferred_element_type=F32)
        if exact:
            n0 = _scatter_rounds(idx_scr[0], nr, PEER_N_KEYS, 0.0)
            c1 = _scatter_rounds(idx_scr[1], None, PEER_N_KEYS, float(PEER_TOPK))
        else:
            n0 = _match_rounds(s0, a, nr, 0.0)
            c1 = _match_rounds(s1, b, None, float(PEER_TOPK))
        n0_ref[h] = n0
        w0_ref[h] = jnp.exp(s0 - a[0:1, :]) * (0.5 / zsum)
        e1_ref[h] = jnp.exp(s1 - b[0:1, :]).astype(BF16)
        c1_ref[h] = c1.astype(BF16)
        return ties

    for h in range(PEER_HEADS):
        ties = head_tables(h, False)

        @pl.when(jnp.max(ties) > 0.0)
        def _():
            head_tables(h, True)


def _peer_sel(h2t, wqt, keys, grp, *, tb=256):
    T = h2t.shape[1]
    tab = jax.ShapeDtypeStruct((PEER_HEADS, PEER_N_KEYS, T), F32)
    tab16 = jax.ShapeDtypeStruct((PEER_HEADS, PEER_N_KEYS, T), BF16)
    tab_spec = pl.BlockSpec((PEER_HEADS, PEER_N_KEYS, tb), lambda i: (0, 0, i))
    n_blocks = T // tb
    return pl.pallas_call(
        functools.partial(_peer_sel_body, tb=tb),
        grid=(n_blocks,),
        in_specs=[
            pl.BlockSpec((D_MODEL, tb), lambda i: (0, 0)),
            pl.BlockSpec((D_MODEL, tb), lambda i: (0, jnp.minimum(i + 1, n_blocks - 1))),
            pl.BlockSpec((PEER_HEADS * PEER_KEY_DIM, D_MODEL), lambda i: (0, 0)),
            pl.BlockSpec((2, PEER_N_KEYS, PEER_KEY_DIM // 2), lambda i: (0, 0, 0)),
            pl.BlockSpec((PEER_TOPK, _N_CAND_PAD), lambda i: (0, 0)),
        ],
        out_specs=[tab_spec, tab_spec, tab_spec, tab_spec],
        out_shape=[tab16, tab16, tab, tab],
        scratch_shapes=[
            pltpu.VMEM((PEER_HEADS * PEER_KEY_DIM, tb), BF16),
            pltpu.VMEM((2 * PEER_HEADS, PEER_N_KEYS, tb), F32),
            pltpu.VMEM((3, PEER_TOPK, tb), F32),
            pltpu.VMEM((3, PEER_TOPK, tb), F32),
            pltpu.VMEM((_N_CAND_PAD, tb), F32),
            pltpu.VMEM((_N_CAND_PAD, tb), F32),
        ],
        compiler_params=_cparams(("arbitrary",)),
        name="peer_sel",
    )(h2t, h2t, wqt, keys, grp)


def _peer_dense_body(u_ref, vt_ref, h2t_ref, e1_ref, c1_ref, n0_ref, w0_ref, x1_ref, fw_ref, out_ref,
                     yt_ref, st_a, st_b, *, ec, tb, nk):
    k = pl.program_id(1)
    slabs = ec // PEER_N_KEYS

    def pre_activations(st_w):
        st_w[...] = jnp.dot(u_ref[...], h2t_ref[...], preferred_element_type=F32)

    def activation_slab(st_r, ii):
        g = jnp.zeros((PEER_N_KEYS, tb), BF16)
        for h in range(PEER_HEADS):
            n_row = n0_ref[h, ii:ii + 1, :].astype(BF16)
            w_row = w0_ref[h, ii:ii + 1, :].astype(BF16)
            g = g + jnp.where(c1_ref[h] < n_row, e1_ref[h] * w_row, jnp.zeros((), BF16))
        x = st_r[ii * PEER_N_KEYS:(ii + 1) * PEER_N_KEYS, :]
        act = x * (1.0 + lax.erf(x * math.sqrt(0.5)))
        return act.astype(BF16) * g

    def activate_and_project(st_r):
        at = jnp.concatenate([activation_slab(st_r, ii) for ii in range(slabs)], axis=0)
        yt_ref[...] += jnp.dot(vt_ref[...], at, preferred_element_type=F32)

    @pl.when(k == 0)
    def _():
        yt_ref[...] = jnp.zeros_like(yt_ref)
        pre_activations(st_a)

    @pl.when((k > 0) & (k < nk) & (k % 2 == 1))
    def _():
        pre_activations(st_b)
        activate_and_project(st_a)

    @pl.when((k > 0) & (k < nk) & (k % 2 == 0))
    def _():
        pre_activations(st_a)
        activate_and_project(st_b)

    @pl.when(k == nk)
    def _():
        activate_and_project(st_b if nk % 2 == 0 else st_a)
        x2 = x1_ref[...] + yt_ref[...].T
        ms = jnp.mean(x2 * x2, axis=-1, keepdims=True)
        out_ref[...] = (x2 * lax.rsqrt(ms + NORM_EPS)) * fw_ref[...]


def _peer_dense(u, vt, h2t, e1, c1, n0, w0, x1, final_w, *, tb=512, ec=1024):
    T = h2t.shape[1]
    nk = PEER_N_EXPERTS // ec
    slabs = ec // PEER_N_KEYS
    assert slabs == SUBLANES, "one f32 sublane tile of per-slab gate rows per expert chunk"
    tab_spec = pl.BlockSpec((PEER_HEADS, PEER_N_KEYS, tb), lambda i, k: (0, 0, i))
    row_spec = pl.BlockSpec((PEER_HEADS, slabs, tb), lambda i, k: (0, jnp.maximum(k - 1, 0), i))
    return pl.pallas_call(
        functools.partial(_peer_dense_body, ec=ec, tb=tb, nk=nk),
        grid=(T // tb, nk + 1),
        in_specs=[
            pl.BlockSpec((ec, D_MODEL), lambda i, k: (jnp.minimum(k, nk - 1), 0)),
            pl.BlockSpec((D_MODEL, ec), lambda i, k: (0, jnp.maximum(k - 1, 0))),
            pl.BlockSpec((D_MODEL, tb), lambda i, k: (0, i)),
            tab_spec, tab_spec, row_spec, row_spec,
            pl.BlockSpec((tb, D_MODEL), lambda i, k: (i, 0)),
            pl.BlockSpec((1, D_MODEL), lambda i, k: (0, 0)),
        ],
        out_specs=pl.BlockSpec((tb, D_MODEL), lambda i, k: (i, 0)),
        out_shape=jax.ShapeDtypeStruct((T, D_MODEL), F32),
        scratch_shapes=[pltpu.VMEM((D_MODEL, tb), F32),
                        pltpu.VMEM((ec, tb), F32), pltpu.VMEM((ec, tb), F32)],
        compiler_params=_cparams(("parallel", "arbitrary")),
        name="peer_dense",
    )(u, vt, h2t, e1, c1, n0, w0, x1, final_w)


def _group_matrix():
    g = np.zeros((PEER_TOPK, _N_CAND_PAD), np.float32)
    off = 0
    for r, ncol in enumerate(_CAND_COLS):
        g[r, off:off + ncol] = 1.0
        off += ncol
    return g


def _layer(x2, norm1_w, w_in, fox_f_bias, conv_w, conv_b, i_bias, f_bias, fox_nw, mlstm_nw, w_out, norm2_w,
           w_q, keys, u, v, final_w, *, batch, seq):
    splits = np.cumsum((FOX_WIDTH, FOX_WIDTH, FOX_WIDTH, FOX_HEADS, MLSTM_QK_WIDTH, MLSTM_QK_WIDTH,
                        MLSTM_V_WIDTH, MLSTM_HEADS, MLSTM_HEADS, MLSTM_V_WIDTH))[:-1]
    fq, fk, fv, ff, mq, mk, mv, mi, mf, mo = jnp.split(w_in, [int(p) for p in splits], axis=1)
    w_main = jnp.concatenate([fq, fk, fv, mv, mo, mq, mk], axis=1).astype(BF16)
    gate_pad = GATE_LANES - FOX_HEADS - 2 * MLSTM_HEADS
    w_gate = jnp.pad(jnp.concatenate([ff, mi, mf], axis=1), ((0, 0), (0, gate_pad))).astype(BF16)
    gate_bias = jnp.pad(jnp.concatenate([fox_f_bias, i_bias, f_bias]), (0, gate_pad)).reshape(1, GATE_LANES)
    tri = jnp.asarray(np.tril(np.ones((LANES, LANES), np.float32)), BF16)

    z, zc, g = _inproj(x2, norm1_w.reshape(1, D_MODEL), w_main, w_gate)
    pq, pk = _placement_matrices()
    qn, kn = _fox_norms(z, jnp.asarray(_head_group_matrix(), BF16))
    kmax = jnp.max(kn.reshape(batch, seq, GATE_LANES), axis=1)
    gc, gr, qx, kx = _gates(g, gate_bias, tri, jnp.asarray(pq, BF16), jnp.asarray(pk, BF16), qn,
                            jnp.repeat(kmax, SUBLANES, axis=0), batch=batch, seq=seq)
    lo, fast = _fox_plan(qn, kmax, gc, batch=batch, seq=seq, tk=FOX_KEY_BLOCK)
    att = _fox(z, qx, kx, fox_nw.reshape(FOX_HEADS, FOX_HEAD_DIM), lo, fast, batch=batch, seq=seq, tk=FOX_KEY_BLOCK)
    cell = _mlstm(zc, z, gc, gr, conv_w, conv_b.reshape(1, -1), mlstm_nw.reshape(1, -1), batch=batch, seq=seq)
    x1, h2t = _outproj(att, cell, x2, w_out[:FOX_WIDTH].astype(BF16), w_out[FOX_WIDTH:].astype(BF16),
                       norm2_w.reshape(1, D_MODEL))
    e1, c1, n0, w0 = _peer_sel(h2t, w_q.T.astype(BF16), keys.astype(BF16), jnp.asarray(_group_matrix(), BF16))
    return _peer_dense(u.astype(BF16), v.T.astype(BF16), h2t, e1, c1, n0, w0, x1, final_w)


def kernel(x, norm1_w, w_in, fox_f_bias, mlstm_conv_w, mlstm_conv_b, mlstm_i_bias, mlstm_f_bias, fox_out_norm_w,
           mlstm_out_norm_w, w_out, norm2_w, peer_w_q, peer_keys, peer_u, peer_v, final_norm_w):
    batch, seq, _ = x.shape
    assert w_in.shape[0] == 1, "single-layer block: the final norm is fused with the last residual add"
    x2 = x.reshape(batch * seq, D_MODEL)
    out = _layer(x2, norm1_w[0], w_in[0], fox_f_bias[0], mlstm_conv_w[0], mlstm_conv_b[0], mlstm_i_bias[0],
                 mlstm_f_bias[0], fox_out_norm_w[0], mlstm_out_norm_w[0], w_out[0], norm2_w[0],
                 peer_w_q[0], peer_keys[0], peer_u[0], peer_v[0], final_norm_w.reshape(1, D_MODEL),
                 batch=batch, seq=seq)
    return out.reshape(batch, seq, D_MODEL)
```

```python
import functools
import math

import numpy as np
import jax
import jax.numpy as jnp
from jax import lax
from jax.experimental import pallas as pl
from jax.experimental.pallas import tpu as pltpu

F32 = jnp.float32
BF16 = jnp.bfloat16

D_MODEL = 2048
FOX_HEADS = 8
FOX_HEAD_DIM = 128
FOX_WIDTH = FOX_HEADS * FOX_HEAD_DIM
MLSTM_HEADS = 4
MLSTM_QK_DIM = 128
MLSTM_V_DIM = 256
MLSTM_QK_WIDTH = MLSTM_HEADS * MLSTM_QK_DIM
MLSTM_V_WIDTH = MLSTM_HEADS * MLSTM_V_DIM
MLSTM_CONV = 4
MLSTM_CHUNK = 128
PEER_HEADS = 8
PEER_KEY_DIM = 256
PEER_N_KEYS = 128
PEER_TOPK = 16
PEER_N_EXPERTS = PEER_N_KEYS * PEER_N_KEYS
NORM_EPS = 1e-6
LOG2E = math.log2(math.e)
NORM_MARGIN = 1.0 + 2.0 ** -7
F32_ZERO_LOG2 = 150.0
FAST_GAP_LOG2 = 90.0
BOUND_SLACK_LOG2 = 4.0
FOX_KEY_BLOCK = 512

LANES = 128
SUBLANES = 8
GATE_LANES = LANES
GATE_ROWS = 16
VMEM_LIMIT = 56 * 1024 * 1024

_CAND_COLS = tuple(PEER_TOPK // (r + 1) for r in range(PEER_TOPK))
_N_CAND = sum(_CAND_COLS)
_N_CAND_PAD = -(-_N_CAND // SUBLANES) * SUBLANES


def _cparams(sem, flags=None):
    return pltpu.CompilerParams(dimension_semantics=sem, vmem_limit_bytes=VMEM_LIMIT, flags=flags)


def _inproj_body(x_ref, nw_ref, w_ref, wg_ref, z_ref, zc_ref, g_ref, h_scr, *, n_main, q_blocks, q_scale):
    j = pl.program_id(1)

    @pl.when(j == 0)
    def _():
        x = x_ref[...]
        ms = jnp.mean(x * x, axis=-1, keepdims=True)
        hb = ((x * lax.rsqrt(ms + NORM_EPS)) * nw_ref[...]).astype(BF16)
        h_scr[...] = hb
        g_ref[...] = jnp.dot(hb, wg_ref[...], preferred_element_type=F32)

    z = jnp.dot(h_scr[...], w_ref[...], preferred_element_type=F32)

    @pl.when(j < n_main)
    def _():
        scale = jnp.where(j < q_blocks, q_scale, 1.0).astype(F32)
        z_ref[...] = (z * scale).astype(BF16)

    @pl.when(j >= n_main)
    def _():
        zc_ref[...] = z


def _inproj(x2, norm_w, w_main, w_gate, *, tm=1024, tn=1024):
    T = x2.shape[0]
    n_cols = w_main.shape[1]
    n_conv = 2 * MLSTM_QK_WIDTH
    n_main = (n_cols - n_conv) // tn
    n_blocks = n_cols // tn
    body = functools.partial(_inproj_body, n_main=n_main, q_blocks=FOX_WIDTH // tn,
                             q_scale=FOX_HEAD_DIM ** -0.5 * LOG2E)
    return pl.pallas_call(
        body,
        grid=(T // tm, n_blocks),
        in_specs=[
            pl.BlockSpec((tm, D_MODEL), lambda i, j: (i, 0)),
            pl.BlockSpec((1, D_MODEL), lambda i, j: (0, 0)),
            pl.BlockSpec((D_MODEL, tn), lambda i, j: (0, j)),
            pl.BlockSpec((D_MODEL, GATE_LANES), lambda i, j: (0, 0)),
        ],
        out_specs=[
            pl.BlockSpec((tm, tn), lambda i, j: (i, jnp.minimum(j, n_main - 1))),
            pl.BlockSpec((tm, tn), lambda i, j: (i, jnp.maximum(j - n_main, 0))),
            pl.BlockSpec((tm, GATE_LANES), lambda i, j: (i, 0)),
        ],
        out_shape=[
            jax.ShapeDtypeStruct((T, n_cols - n_conv), BF16),
            jax.ShapeDtypeStruct((T, n_conv), F32),
            jax.ShapeDtypeStruct((T, GATE_LANES), F32),
        ],
        scratch_shapes=[pltpu.VMEM((tm, D_MODEL), BF16)],
        compiler_params=_cparams(("parallel", "arbitrary")),
        name="inproj",
    )(x2, norm_w, w_main, w_gate)


def _split3(v):
    hi = v.astype(BF16)
    r1 = v - hi.astype(F32)
    mid = r1.astype(BF16)
    lo = (r1 - mid.astype(F32)).astype(BF16)
    return hi, mid, lo


def _fox_norms_body(q_ref, k_ref, grp_ref, qn_ref, kn_ref):
    for src, dst in ((q_ref, qn_ref), (k_ref, kn_ref)):
        x = src[...].astype(F32)
        ss = jnp.dot((x * x).astype(BF16), grp_ref[...], preferred_element_type=F32)
        dst[...] = jnp.sqrt(ss) * NORM_MARGIN


def _head_group_matrix():
    g = np.zeros((FOX_WIDTH, GATE_LANES), np.float32)
    for h in range(FOX_HEADS):
        g[h * FOX_HEAD_DIM:(h + 1) * FOX_HEAD_DIM, h] = 1.0
    return g


def _fox_norms(z, grp, *, rows=1024):
    T = z.shape[0]
    out = jax.ShapeDtypeStruct((T, GATE_LANES), F32)
    return pl.pallas_call(
        _fox_norms_body,
        grid=(T // rows,),
        in_specs=[
            pl.BlockSpec((rows, FOX_WIDTH), lambda i: (i, 0)),
            pl.BlockSpec((rows, FOX_WIDTH), lambda i: (i, 1)),
            pl.BlockSpec((FOX_WIDTH, GATE_LANES), lambda i: (0, 0)),
        ],
        out_specs=[pl.BlockSpec((rows, GATE_LANES), lambda i: (i, 0))] * 2,
        out_shape=[out, out],
        compiler_params=_cparams(("parallel",)),
        name="fox_norms",
    )(z, z, grp)


def _gates_body(g_ref, bias_ref, tri_ref, pq_ref, pk_ref, qn_ref, kmax_ref, gc_ref, gr_ref, qx_ref, kx_ref,
                carry_scr, *, rows):
    c = pl.program_id(1)

    @pl.when(c == 0)
    def _():
        carry_scr[...] = jnp.zeros_like(carry_scr)

    lane = lax.broadcasted_iota(jnp.int32, (LANES, GATE_LANES), 1)
    is_glob = lane < FOX_HEADS
    is_ls = is_glob | ((lane >= FOX_HEADS + MLSTM_HEADS) & (lane < FOX_HEADS + 2 * MLSTM_HEADS))
    tri = tri_ref[...]
    for s in range(rows // LANES):
        sl = slice(s * LANES, (s + 1) * LANES)
        g = g_ref[sl, :] + bias_ref[...]
        ls = jnp.minimum(g, 0.0) - jnp.log1p(jnp.exp(-jnp.abs(g)))
        v = jnp.where(is_ls, ls, 0.0)
        hi, mid, lo = _split3(v)
        cs = (jnp.dot(tri, hi, preferred_element_type=F32)
              + jnp.dot(tri, mid, preferred_element_type=F32)
              + jnp.dot(tri, lo, preferred_element_type=F32))
        glob = cs + carry_scr[...]
        carry_scr[...] = glob[LANES - 1:LANES, :]
        out = jnp.where(is_glob, glob, jnp.where(is_ls, cs, g))
        gc_ref[sl, :] = out
        gr_ref[:, sl] = out.T[0:GATE_ROWS, :]
        groups = (*_split3(glob * LOG2E), jnp.ones((LANES, GATE_LANES), BF16),
                  *_split3(qn_ref[sl, :] * kmax_ref[0:1, :] + 1.0))
        pieces = jnp.zeros((LANES, GATE_LANES), F32)
        for gi, grp in enumerate(groups):
            moved = pltpu.roll(grp.astype(F32), gi * FOX_HEADS, axis=1) if gi else grp.astype(F32)
            pieces = jnp.where((lane >= gi * FOX_HEADS) & (lane < (gi + 1) * FOX_HEADS), moved, pieces)
        pieces = pieces.astype(BF16)
        qx_ref[sl, :] = jnp.dot(pieces, pq_ref[...], preferred_element_type=F32).astype(BF16)
        kx_ref[sl, :] = jnp.dot(pieces, pk_ref[...], preferred_element_type=F32).astype(BF16)


_N_PIECES = 3
_PIECE_GROUPS = 2 * _N_PIECES + 1
_STAB_SLOT = 2 * _N_PIECES


def _placement_matrices():
    n = _N_PIECES
    assert _PIECE_GROUPS * FOX_HEADS <= GATE_LANES
    pq = np.zeros((GATE_LANES, FOX_WIDTH), np.float32)
    pk = np.zeros((GATE_LANES, FOX_WIDTH), np.float32)
    for h in range(FOX_HEADS):
        for p in range(n):
            pq[p * FOX_HEADS + h, h * FOX_HEAD_DIM + p] = 1.0
            pk[n * FOX_HEADS + h, h * FOX_HEAD_DIM + p] = 1.0
            pq[n * FOX_HEADS + h, h * FOX_HEAD_DIM + n + p] = 1.0
            pk[p * FOX_HEADS + h, h * FOX_HEAD_DIM + n + p] = -1.0
            pq[(n + 1 + p) * FOX_HEADS + h, h * FOX_HEAD_DIM + _STAB_SLOT + p] = -1.0
            pk[n * FOX_HEADS + h, h * FOX_HEAD_DIM + _STAB_SLOT + p] = 1.0
    return pq, pk


def _gates(g, bias, tri, pq, pk, qn, kmax, *, batch, seq, rows=1024):
    T = g.shape[0]
    nblk = seq // rows
    return pl.pallas_call(
        functools.partial(_gates_body, rows=rows),
        grid=(batch, nblk),
        in_specs=[
            pl.BlockSpec((rows, GATE_LANES), lambda b, c: (b * nblk + c, 0)),
            pl.BlockSpec((1, GATE_LANES), lambda b, c: (0, 0)),
            pl.BlockSpec((LANES, LANES), lambda b, c: (0, 0)),
            pl.BlockSpec((GATE_LANES, FOX_WIDTH), lambda b, c: (0, 0)),
            pl.BlockSpec((GATE_LANES, FOX_WIDTH), lambda b, c: (0, 0)),
            pl.BlockSpec((rows, GATE_LANES), lambda b, c: (b * nblk + c, 0)),
            pl.BlockSpec((SUBLANES, GATE_LANES), lambda b, c: (b, 0)),
        ],
        out_specs=[
            pl.BlockSpec((rows, GATE_LANES), lambda b, c: (b * nblk + c, 0)),
            pl.BlockSpec((GATE_ROWS, rows), lambda b, c: (0, b * nblk + c)),
            pl.BlockSpec((rows, FOX_WIDTH), lambda b, c: (b * nblk + c, 0)),
            pl.BlockSpec((rows, FOX_WIDTH), lambda b, c: (b * nblk + c, 0)),
        ],
        out_shape=[
            jax.ShapeDtypeStruct((T, GATE_LANES), F32),
            jax.ShapeDtypeStruct((GATE_ROWS, T), F32),
            jax.ShapeDtypeStruct((T, FOX_WIDTH), BF16),
            jax.ShapeDtypeStruct((T, FOX_WIDTH), BF16),
        ],
        scratch_shapes=[pltpu.VMEM((1, GATE_LANES), F32)],
        compiler_params=_cparams(("parallel", "arbitrary")),
        name="gates",
    )(g, bias, tri, pq, pk, qn, kmax)


def _fox_body(lo_ref, fast_ref, q_ref, qx_ref, k_ref, kx_ref, v_ref, nw_ref, o_ref, *, tk, nq):
    b = pl.program_id(0)
    h = pl.program_id(1)
    qi = pl.program_id(2)
    tile = (b * FOX_HEADS + h) * nq + qi
    lo = lo_ref[tile]
    row = lax.broadcasted_iota(jnp.int32, (tk, tk), 0)
    col = lax.broadcasted_iota(jnp.int32, (tk, tk), 1)

    def load_kv(ki):
        start = pl.multiple_of(ki * tk, tk)
        k = jnp.concatenate([k_ref[pl.ds(start, tk), :], kx_ref[pl.ds(start, tk), :]], axis=1)
        return k, v_ref[pl.ds(start, tk), :]

    def logits(q, k, masked):
        s = lax.dot_general(q, k, (((1,), (1,)), ((), ())), preferred_element_type=F32)
        return jnp.where(row >= col, s, -jnp.inf) if masked else s

    def finish(r, num, den):
        out = num / den
        ms = jnp.mean(out * out, axis=-1, keepdims=True)
        o_ref[r * tk:(r + 1) * tk, :] = ((out * lax.rsqrt(ms + NORM_EPS)) * nw_ref[pl.ds(h, 1), :]).astype(BF16)

    @pl.when(fast_ref[tile] == 1)
    def _():
        q_halves = [jnp.concatenate([q_ref[r * tk:(r + 1) * tk, :], qx_ref[r * tk:(r + 1) * tk, :]], axis=1)
                    for r in range(2)]
        ones_col = (lax.broadcasted_iota(jnp.int32, (tk, LANES), 1) == 0).astype(BF16)

        def weights(r, k, masked):
            return jnp.exp2(logits(q_halves[r], k, masked)).astype(BF16)

        def values(ki):
            return jnp.concatenate([load_kv(ki)[1], ones_col], axis=1)

        def add(acc, p, va):
            return acc + jnp.dot(p, va, preferred_element_type=F32)

        def pair(kp, accs):
            blocks = [2 * kp, 2 * kp + 1]
            ps = [[weights(r, load_kv(ki)[0], False) for r in range(2)] for ki in blocks]
            for j, ki in enumerate(blocks):
                va = values(ki)
                accs = tuple(add(accs[r], ps[j][r], va) for r in range(2))
            return accs

        zero = jnp.zeros((tk, 2 * FOX_HEAD_DIM), F32)
        aa, ab = lax.fori_loop(lo // 2, qi, pair, (zero, zero))
        k = load_kv(2 * qi)[0]
        pa, pb = weights(0, k, True), weights(1, k, False)
        pb2 = weights(1, load_kv(2 * qi + 1)[0], True)
        va = values(2 * qi)
        aa, ab = add(aa, pa, va), add(ab, pb, va)
        ab = add(ab, pb2, values(2 * qi + 1))
        for r, acc in enumerate((aa, ab)):
            finish(r, acc[:, 0:FOX_HEAD_DIM], acc[:, FOX_HEAD_DIM:FOX_HEAD_DIM + 1])

    @pl.when(fast_ref[tile] == 0)
    def _():
        _fox_online(q_ref, qx_ref, load_kv, logits, finish, lo, qi, tk)


def _fox_online(q_ref, qx_ref, load_kv, logits, finish, lo, qi, tk):
    lane = lax.broadcasted_iota(jnp.int32, (tk, LANES), 1)
    no_stab = (lane < _STAB_SLOT) | (lane >= _STAB_SLOT + _N_PIECES)
    q_halves = [jnp.concatenate([q_ref[r * tk:(r + 1) * tk, :],
                                 jnp.where(no_stab, qx_ref[r * tk:(r + 1) * tk, :], jnp.zeros((), BF16))], axis=1)
                for r in range(2)]

    def weights(s, carry):
        m, l, acc = carry
        cols = [s[:, c * LANES:(c + 1) * LANES] for c in range(tk // LANES)]
        m_new = jnp.maximum(m, jnp.max(functools.reduce(jnp.maximum, cols), axis=-1, keepdims=True))
        alpha = jnp.exp2(m - m_new)
        ps = [jnp.exp2(c - m_new) for c in cols]
        l = alpha * l + functools.reduce(jnp.add, ps)
        return m_new, l, alpha * acc, jnp.concatenate(ps, axis=1).astype(BF16)

    def update(q, k, v, carry, masked):
        m, l, acc, p = weights(logits(q, k, masked), carry)
        return m, l, acc + jnp.dot(p, v, preferred_element_type=F32)

    def both(ki, carries):
        k, v = load_kv(ki)
        s = [logits(q_halves[r], k, False) for r in range(2)]
        w = [weights(s[r], carries[r]) for r in range(2)]
        return tuple((m, l, acc + jnp.dot(p, v, preferred_element_type=F32)) for m, l, acc, p in w)

    init = (jnp.full((tk, LANES), -jnp.inf, F32), jnp.zeros((tk, LANES), F32), jnp.zeros((tk, FOX_HEAD_DIM), F32))
    ca, cb = lax.fori_loop(lo, 2 * qi, both, (init, init))
    k, v = load_kv(2 * qi)
    ca = update(q_halves[0], k, v, ca, True)
    cb = update(q_halves[1], k, v, cb, False)
    k, v = load_kv(2 * qi + 1)
    cb = update(q_halves[1], k, v, cb, True)
    for r, (_, l, acc) in enumerate((ca, cb)):
        finish(r, acc, jnp.sum(l, axis=-1, keepdims=True))


def _fox_plan(qn, kmax, gc, *, batch, seq, tk):
    tq = 2 * tk
    nq = seq // tq
    nk = seq // tk
    heads = slice(0, FOX_HEADS)
    qmax = jnp.max(qn.reshape(batch, nq, tq, GATE_LANES), axis=2)[..., heads]
    gap = 2.0 * qmax * kmax[:, None, heads] + BOUND_SLACK_LOG2
    f2 = (gc[:, heads] * LOG2E).reshape(batch, nk, tk, FOX_HEADS)
    f_first = f2[:, ::2, 0, :]
    f_last = f2[:, :, tk - 1, :]
    bound = gap[:, :, None, :] + f_first[:, :, None, :] - f_last[:, None, :, :]
    below_diag = jnp.arange(nk)[None, :, None] < 2 * jnp.arange(nq)[:, None, None]
    skip = (bound < -F32_ZERO_LOG2) & below_diag[None]
    lo = jnp.sum(jnp.cumprod(skip.astype(jnp.int32), axis=2), axis=2)
    fast = (gap <= FAST_GAP_LOG2).astype(jnp.int32)
    flat = lambda a: jnp.transpose(a, (0, 2, 1)).reshape(-1)
    return flat(lo), flat(fast)


def _fox(z, qx, kx, nw, lo, fast, *, batch, seq, tk=512):
    T = z.shape[0]
    tq = 2 * tk
    nq = seq // tq
    kcol = FOX_WIDTH // FOX_HEAD_DIM
    grid_spec = pltpu.PrefetchScalarGridSpec(
        num_scalar_prefetch=2,
        grid=(batch, FOX_HEADS, nq),
        in_specs=[
            pl.BlockSpec((tq, FOX_HEAD_DIM), lambda b, h, i, lo, fast: (b * nq + i, h)),
            pl.BlockSpec((tq, FOX_HEAD_DIM), lambda b, h, i, lo, fast: (b * nq + i, h)),
            pl.BlockSpec((seq, FOX_HEAD_DIM), lambda b, h, i, lo, fast: (b, kcol + h)),
            pl.BlockSpec((seq, FOX_HEAD_DIM), lambda b, h, i, lo, fast: (b, h)),
            pl.BlockSpec((seq, FOX_HEAD_DIM), lambda b, h, i, lo, fast: (b, 2 * kcol + h)),
            pl.BlockSpec((FOX_HEADS, FOX_HEAD_DIM), lambda b, h, i, lo, fast: (0, 0)),
        ],
        out_specs=pl.BlockSpec((tq, FOX_HEAD_DIM), lambda b, h, i, lo, fast: (b * nq + i, h)),
    )
    return pl.pallas_call(
        functools.partial(_fox_body, tk=tk, nq=nq),
        grid_spec=grid_spec,
        out_shape=jax.ShapeDtypeStruct((T, FOX_WIDTH), BF16),
        compiler_params=_cparams(("parallel", "parallel", "arbitrary")),
        name="fox",
    )(lo, fast, z, qx, z, kx, z, nw)


def _mlstm_body(zc_ref, zprev_ref, v_ref, o_ref, gc_ref, gr_ref, cw_ref, cb_ref, nw_ref, out_ref,
                full_scr, c_scr, n_scr, m_scr, *, cps):
    c = pl.program_id(1)
    L = MLSTM_CHUNK
    dk = MLSTM_QK_DIM
    dv = MLSTM_V_DIM
    rows = cps * L

    @pl.when(c == 0)
    def _():
        c_scr[...] = jnp.zeros_like(c_scr)
        n_scr[...] = jnp.zeros_like(n_scr)
        m_scr[...] = jnp.zeros_like(m_scr)

    full_scr[0:SUBLANES, :] = jnp.where(c == 0, 0.0, zprev_ref[...])
    full_scr[SUBLANES:SUBLANES + rows, :] = zc_ref[...]
    y = cb_ref[...]
    for j in range(MLSTM_CONV):
        y = y + cw_ref[j:j + 1, :] * full_scr[pl.ds(SUBLANES - (MLSTM_CONV - 1) + j, rows), :]
    qk_all = y * jax.nn.sigmoid(y)

    row = lax.broadcasted_iota(jnp.int32, (L, L), 0)
    col = lax.broadcasted_iota(jnp.int32, (L, L), 1)
    causal = row >= col
    assert cps == 1
    heads = range(MLSTM_HEADS)
    gcb = gc_ref[...]
    grb = gr_ref[...]
    qh = [qk_all[:, hh * dk:(hh + 1) * dk] for hh in heads]
    kh = [qk_all[:, MLSTM_QK_WIDTH + hh * dk:MLSTM_QK_WIDTH + (hh + 1) * dk] * (dk ** -0.5) for hh in heads]
    vh = [v_ref[:, hh * dv:(hh + 1) * dv] for hh in heads]
    i_col = [gcb[:, FOX_HEADS + hh:FOX_HEADS + hh + 1] for hh in heads]
    b_col = [gcb[:, FOX_HEADS + MLSTM_HEADS + hh:FOX_HEADS + MLSTM_HEADS + hh + 1] for hh in heads]
    i_row = [grb[FOX_HEADS + hh:FOX_HEADS + hh + 1, :] for hh in heads]
    b_row = [grb[FOX_HEADS + MLSTM_HEADS + hh:FOX_HEADS + MLSTM_HEADS + hh + 1, :] for hh in heads]
    b_last = [b_row[hh][:, L - 1:L] for hh in heads]
    c_prev = [c_scr[hh] for hh in heads]
    n_prev = [n_scr[hh] for hh in heads]
    m_prev = [m_scr[hh][:, 0:1] for hh in heads]

    m_loc = [jnp.max(b_last[hh] - b_row[hh] + i_row[hh], axis=-1, keepdims=True) for hh in heads]
    kw = [kh[hh] * jnp.exp(b_last[hh] - b_col[hh] + i_col[hh] - m_loc[hh]) for hh in heads]
    g_col = [b_col[hh] + m_prev[hh] for hh in heads]
    dmat = [jnp.where(causal, b_col[hh] - b_row[hh] + i_row[hh], -jnp.inf) for hh in heads]
    m_t = [jnp.maximum(g_col[hh], jnp.max(dmat[hh], axis=-1, keepdims=True)) for hh in heads]
    decay = [jnp.exp(dmat[hh] - m_t[hh]) for hh in heads]
    inter = [jnp.exp(g_col[hh] - m_t[hh]) for hh in heads]
    qb = [qh[hh].astype(BF16) for hh in heads]
    qk = [lax.dot_general(qb[hh], kh[hh].astype(BF16), (((1,), (1,)), ((), ())), preferred_element_type=F32)
          for hh in heads]
    carried = [jnp.dot(qb[hh], c_prev[hh].astype(BF16), preferred_element_type=F32) for hh in heads]
    c_loc = [lax.dot_general(kw[hh].astype(BF16), vh[hh], (((0,), (0,)), ((), ())), preferred_element_type=F32)
             for hh in heads]
    sm = [qk[hh] * decay[hh] for hh in heads]
    intra = [jnp.dot(sm[hh].astype(BF16), vh[hh], preferred_element_type=F32) for hh in heads]
    den = [jnp.sum(sm[hh], axis=-1, keepdims=True)
           + inter[hh] * jnp.sum(qh[hh] * n_prev[hh], axis=-1, keepdims=True) for hh in heads]
    for hh in heads:
        cell = (intra[hh] + inter[hh] * carried[hh]) / jnp.maximum(jnp.abs(den[hh]), jnp.exp(-m_t[hh]))
        gated = jax.nn.sigmoid(o_ref[:, hh * dv:(hh + 1) * dv].astype(F32)) * cell
        ms = jnp.mean(gated * gated, axis=-1, keepdims=True)
        out_ref[:, hh * dv:(hh + 1) * dv] = (
            (gated * lax.rsqrt(ms + NORM_EPS)) * nw_ref[:, hh * dv:(hh + 1) * dv]).astype(BF16)
    for hh in heads:
        m_new = jnp.maximum(b_last[hh] + m_prev[hh], m_loc[hh])
        a_prev = jnp.exp(b_last[hh] + m_prev[hh] - m_new)
        a_loc = jnp.exp(m_loc[hh] - m_new)
        c_scr[hh] = a_prev * c_prev[hh] + a_loc * c_loc[hh]
        n_scr[hh] = a_prev * n_prev[hh] + a_loc * jnp.sum(kw[hh], axis=0, keepdims=True)
        m_scr[hh] = jnp.broadcast_to(m_new, (1, LANES))


def _mlstm(zc, z, gc, gr, conv_w, conv_b, nw, *, batch, seq, cps=1):
    T = zc.shape[0]
    L = cps * MLSTM_CHUNK
    nc = seq // L
    per = L // SUBLANES
    vcol = 3 * FOX_WIDTH // MLSTM_V_WIDTH
    return pl.pallas_call(
        functools.partial(_mlstm_body, cps=cps),
        grid=(batch, nc),
        in_specs=[
            pl.BlockSpec((L, 2 * MLSTM_QK_WIDTH), lambda b, c: (b * nc + c, 0)),
            pl.BlockSpec((SUBLANES, 2 * MLSTM_QK_WIDTH), lambda b, c: (jnp.maximum((b * nc + c) * per - 1, 0), 0)),
            pl.BlockSpec((L, MLSTM_V_WIDTH), lambda b, c: (b * nc + c, vcol)),
            pl.BlockSpec((L, MLSTM_V_WIDTH), lambda b, c: (b * nc + c, vcol + 1)),
            pl.BlockSpec((L, GATE_LANES), lambda b, c: (b * nc + c, 0)),
            pl.BlockSpec((GATE_ROWS, L), lambda b, c: (0, b * nc + c)),
            pl.BlockSpec((MLSTM_CONV, 2 * MLSTM_QK_WIDTH), lambda b, c: (0, 0)),
            pl.BlockSpec((1, 2 * MLSTM_QK_WIDTH), lambda b, c: (0, 0)),
            pl.BlockSpec((1, MLSTM_V_WIDTH), lambda b, c: (0, 0)),
        ],
        out_specs=pl.BlockSpec((L, MLSTM_V_WIDTH), lambda b, c: (b * nc + c, 0)),
        out_shape=jax.ShapeDtypeStruct((T, MLSTM_V_WIDTH), BF16),
        scratch_shapes=[
            pltpu.VMEM((SUBLANES + L, 2 * MLSTM_QK_WIDTH), F32),
            pltpu.VMEM((MLSTM_HEADS, MLSTM_QK_DIM, MLSTM_V_DIM), F32),
            pltpu.VMEM((MLSTM_HEADS, 1, MLSTM_QK_DIM), F32),
            pltpu.VMEM((MLSTM_HEADS, 1, LANES), F32),
        ],
        compiler_params=_cparams(("parallel", "arbitrary")),
        name="mlstm",
    )(zc, zc, z, z, gc, gr, conv_w, conv_b, nw)


def _outproj_body(att_ref, cell_ref, x_ref, wa_ref, wb_ref, n2_ref, x1_ref, h2t_ref):
    y = (jnp.dot(att_ref[...], wa_ref[...], preferred_element_type=F32)
         + jnp.dot(cell_ref[...], wb_ref[...], preferred_element_type=F32))
    x1 = x_ref[...] + y
    x1_ref[...] = x1
    ms = jnp.mean(x1 * x1, axis=-1, keepdims=True)
    h2 = (x1 * lax.rsqrt(ms + NORM_EPS)) * n2_ref[...]
    h2t_ref[...] = h2.T.astype(BF16)


def _outproj(att, cell, x2, wa, wb, n2, *, tm=512):
    T = x2.shape[0]
    return pl.pallas_call(
        _outproj_body,
        grid=(T // tm,),
        in_specs=[
            pl.BlockSpec((tm, FOX_WIDTH), lambda i: (i, 0)),
            pl.BlockSpec((tm, MLSTM_V_WIDTH), lambda i: (i, 0)),
            pl.BlockSpec((tm, D_MODEL), lambda i: (i, 0)),
            pl.BlockSpec((FOX_WIDTH, D_MODEL), lambda i: (0, 0)),
            pl.BlockSpec((MLSTM_V_WIDTH, D_MODEL), lambda i: (0, 0)),
            pl.BlockSpec((1, D_MODEL), lambda i: (0, 0)),
        ],
        out_specs=[
            pl.BlockSpec((tm, D_MODEL), lambda i: (i, 0)),
            pl.BlockSpec((D_MODEL, tm), lambda i: (0, i)),
        ],
        out_shape=[
            jax.ShapeDtypeStruct((T, D_MODEL), F32),
            jax.ShapeDtypeStruct((D_MODEL, T), BF16),
        ],
        compiler_params=_cparams(("parallel",)),
        name="outproj",
    )(att, cell, x2, wa, wb, n2)


def _row_iota(n_rows, tb):
    return lax.broadcasted_iota(jnp.int32, (n_rows, tb), 0).astype(F32)


def _extract16(s, val_ref, idx_ref):
    n_rows, tb = s.shape
    iota = _row_iota(n_rows, tb)

    def body(r, sw):
        mx = jnp.max(sw, axis=0, keepdims=True)
        first = jnp.min(jnp.where(sw == mx, iota, float(n_rows)), axis=0, keepdims=True)
        val_ref[pl.ds(r, 1), :] = mx
        idx_ref[pl.ds(r, 1), :] = first
        return jnp.where(iota == first, -jnp.inf, sw)

    lax.fori_loop(0, PEER_TOPK, body, s)


def _scatter_rounds(idx, values, n_rows, fill):
    tb = idx.shape[1]
    iota = _row_iota(n_rows, tb)
    table = jnp.full((n_rows, tb), fill, F32)
    for r in range(PEER_TOPK):
        v = values[r:r + 1, :] if values is not None else float(r)
        table = jnp.where(iota == idx[r:r + 1, :], v, table)
    return table


def _extract16_distinct(arrays, val_refs):
    def body(r, carry):
        out = []
        for sw, val_ref in zip(carry, val_refs):
            mx = jnp.max(sw, axis=0, keepdims=True)
            val_ref[pl.ds(r, 1), :] = mx
            out.append(jnp.where(sw == mx, -jnp.inf, sw))
        return tuple(out)

    lax.fori_loop(0, PEER_TOPK, body, tuple(arrays), unroll=True)


def _count_ge(s, thr):
    return jnp.sum((s >= thr).astype(F32), axis=0, keepdims=True)


def _match_rounds(s, vals, values, fill):
    table = jnp.full(s.shape, fill, F32)
    for r in range(PEER_TOPK):
        v = values[r:r + 1, :] if values is not None else float(r)
        table = jnp.where(s == vals[r:r + 1, :], v, table)
    return table


def _peer_sel_body(h2t_ref, h2t_next_ref, wqt_ref, keys_ref, grp_ref, e1_ref, c1_ref, n0_ref, w0_ref,
                   qt_scr, sc_scr, vals_scr, idx_scr, cand_scr, pe_scr, *, tb):
    half = PEER_KEY_DIM // 2

    def queries(src_ref):
        qt_scr[...] = jnp.dot(wqt_ref[...], src_ref[...], preferred_element_type=F32).astype(BF16)

    @pl.when(pl.program_id(0) == 0)
    def _():
        queries(h2t_ref)

    for hp in range(2 * PEER_HEADS):
        sc_scr[hp] = jnp.dot(keys_ref[hp % 2], qt_scr[hp * half:(hp + 1) * half, :], preferred_element_type=F32)
    queries(h2t_next_ref)

    def head_tables(h, exact):
        s0 = sc_scr[2 * h]
        s1 = sc_scr[2 * h + 1]
        if exact:
            _extract16(s0, vals_scr.at[0], idx_scr.at[0])
            _extract16(s1, vals_scr.at[1], idx_scr.at[1])
        else:
            _extract16_distinct([s0], [vals_scr.at[0]])
            _extract16_distinct([s1], [vals_scr.at[1]])
        a = vals_scr[0]
        b = vals_scr[1]
        ea = jnp.exp(a - a[0:1, :])
        eb = jnp.exp(b - b[0:1, :])
        off = 0
        for r in range(PEER_TOPK):
            ncol = _CAND_COLS[r]
            cand_scr[off:off + ncol, :] = a[r:r + 1, :] + b[0:ncol, :]
            pe_scr[off:off + ncol, :] = ea[r:r + 1, :] * eb[0:ncol, :]
            off += ncol
        cand_scr[_N_CAND:_N_CAND_PAD, :] = jnp.full((_N_CAND_PAD - _N_CAND, tb), -jnp.inf, F32)
        pe_scr[_N_CAND:_N_CAND_PAD, :] = jnp.zeros((_N_CAND_PAD - _N_CAND, tb), F32)
        cand = cand_scr[...]
        last = PEER_TOPK - 1
        if exact:
            _extract16(cand, vals_scr.at[2], idx_scr.at[2])
            sel = _scatter_rounds(idx_scr[2], jnp.ones((PEER_TOPK, tb), F32), _N_CAND_PAD, 0.0)
            ties = None
        else:
            _extract16_distinct([cand], [vals_scr.at[2]])
            sel = (cand >= vals_scr[2, last:last + 1, :]).astype(F32)
            ties = ((_count_ge(s0, a[last:last + 1, :]) != float(PEER_TOPK)).astype(F32)
                    + (_count_ge(s1, b[last:last + 1, :]) != float(PEER_TOPK)).astype(F32)
                    + (jnp.sum(sel, axis=0, keepdims=True) != float(PEER_TOPK)).astype(F32))
        zsum = jnp.sum(sel * pe_scr[...], axis=0, keepdims=True)
        nr = jnp.dot(grp_ref[...], sel.astype(BF16), preferred_element_type=F32)
        if exact:
            n0 = _scatter_rounds(idx_scr[0], nr, PEER_N_KEYS, 0.0)
            c1 = _scatter_rounds(idx_scr[1], None, PEER_N_KEYS, float(PEER_TOPK))
        else:
            n0 = _match_rounds(s0, a, nr, 0.0)
            c1 = _match_rounds(s1, b, None, float(PEER_TOPK))
        n0_ref[h] = n0
        w0_ref[h] = jnp.exp(s0 - a[0:1, :]) * (0.5 / zsum)
        e1_ref[h] = jnp.exp(s1 - b[0:1, :]).astype(BF16)
        c1_ref[h] = c1.astype(BF16)
        return ties

    for h in range(PEER_HEADS):
        ties = head_tables(h, False)

        @pl.when(jnp.max(ties) > 0.0)
        def _():
            head_tables(h, True)


def _peer_sel(h2t, wqt, keys, grp, *, tb=256):
    T = h2t.shape[1]
    tab = jax.ShapeDtypeStruct((PEER_HEADS, PEER_N_KEYS, T), F32)
    tab16 = jax.ShapeDtypeStruct((PEER_HEADS, PEER_N_KEYS, T), BF16)
    tab_spec = pl.BlockSpec((PEER_HEADS, PEER_N_KEYS, tb), lambda i: (0, 0, i))
    n_blocks = T // tb
    return pl.pallas_call(
        functools.partial(_peer_sel_body, tb=tb),
        grid=(n_blocks,),
        in_specs=[
            pl.BlockSpec((D_MODEL, tb), lambda i: (0, 0)),
            pl.BlockSpec((D_MODEL, tb), lambda i: (0, jnp.minimum(i + 1, n_blocks - 1))),
            pl.BlockSpec((PEER_HEADS * PEER_KEY_DIM, D_MODEL), lambda i: (0, 0)),
            pl.BlockSpec((2, PEER_N_KEYS, PEER_KEY_DIM // 2), lambda i: (0, 0, 0)),
            pl.BlockSpec((PEER_TOPK, _N_CAND_PAD), lambda i: (0, 0)),
        ],
        out_specs=[tab_spec, tab_spec, tab_spec, tab_spec],
        out_shape=[tab16, tab16, tab, tab],
        scratch_shapes=[
            pltpu.VMEM((PEER_HEADS * PEER_KEY_DIM, tb), BF16),
            pltpu.VMEM((2 * PEER_HEADS, PEER_N_KEYS, tb), F32),
            pltpu.VMEM((3, PEER_TOPK, tb), F32),
            pltpu.VMEM((3, PEER_TOPK, tb), F32),
            pltpu.VMEM((_N_CAND_PAD, tb), F32),
            pltpu.VMEM((_N_CAND_PAD, tb), F32),
        ],
        compiler_params=_cparams(("arbitrary",)),
        name="peer_sel",
    )(h2t, h2t, wqt, keys, grp)


def _peer_dense_body(u_ref, vt_ref, h2t_ref, e1_ref, c1_ref, n0_ref, w0_ref, x1_ref, fw_ref, out_ref,
                     yt_ref, st_a, st_b, *, ec, tb, nk):
    k = pl.program_id(1)
    slabs = ec // PEER_N_KEYS

    def pre_activations(st_w):
        st_w[...] = jnp.dot(u_ref[...], h2t_ref[...], preferred_element_type=F32)

    def activation_slab(st_r, ii):
        g = jnp.zeros((PEER_N_KEYS, tb), BF16)
        for h in range(PEER_HEADS):
            n_row = n0_ref[h, ii:ii + 1, :].astype(BF16)
            w_row = w0_ref[h, ii:ii + 1, :].astype(BF16)
            g = g + jnp.where(c1_ref[h] < n_row, e1_ref[h] * w_row, jnp.zeros((), BF16))
        x = st_r[ii * PEER_N_KEYS:(ii + 1) * PEER_N_KEYS, :]
        act = x * (1.0 + lax.erf(x * math.sqrt(0.5)))
        return act.astype(BF16) * g

    def activate_and_project(st_r):
        at = jnp.concatenate([activation_slab(st_r, ii) for ii in range(slabs)], axis=0)
        yt_ref[...] += jnp.dot(vt_ref[...], at, preferred_element_type=F32)

    @pl.when(k == 0)
    def _():
        yt_ref[...] = jnp.zeros_like(yt_ref)
        pre_activations(st_a)

    @pl.when((k > 0) & (k < nk) & (k % 2 == 1))
    def _():
        pre_activations(st_b)
        activate_and_project(st_a)

    @pl.when((k > 0) & (k < nk) & (k % 2 == 0))
    def _():
        pre_activations(st_a)
        activate_and_project(st_b)

    @pl.when(k == nk)
    def _():
        activate_and_project(st_b if nk % 2 == 0 else st_a)
        x2 = x1_ref[...] + yt_ref[...].T
        ms = jnp.mean(x2 * x2, axis=-1, keepdims=True)
        out_ref[...] = (x2 * lax.rsqrt(ms + NORM_EPS)) * fw_ref[...]


def _peer_dense(u, vt, h2t, e1, c1, n0, w0, x1, final_w, *, tb=512, ec=1024):
    T = h2t.shape[1]
    nk = PEER_N_EXPERTS // ec
    slabs = ec // PEER_N_KEYS
    assert slabs == SUBLANES, "one f32 sublane tile of per-slab gate rows per expert chunk"
    tab_spec = pl.BlockSpec((PEER_HEADS, PEER_N_KEYS, tb), lambda i, k: (0, 0, i))
    row_spec = pl.BlockSpec((PEER_HEADS, slabs, tb), lambda i, k: (0, jnp.maximum(k - 1, 0), i))
    return pl.pallas_call(
        functools.partial(_peer_dense_body, ec=ec, tb=tb, nk=nk),
        grid=(T // tb, nk + 1),
        in_specs=[
            pl.BlockSpec((ec, D_MODEL), lambda i, k: (jnp.minimum(k, nk - 1), 0)),
            pl.BlockSpec((D_MODEL, ec), lambda i, k: (0, jnp.maximum(k - 1, 0))),
            pl.BlockSpec((D_MODEL, tb), lambda i, k: (0, i)),
            tab_spec, tab_spec, row_spec, row_spec,
            pl.BlockSpec((tb, D_MODEL), lambda i, k: (i, 0)),
            pl.BlockSpec((1, D_MODEL), lambda i, k: (0, 0)),
        ],
        out_specs=pl.BlockSpec((tb, D_MODEL), lambda i, k: (i, 0)),
        out_shape=jax.ShapeDtypeStruct((T, D_MODEL), F32),
        scratch_shapes=[pltpu.VMEM((D_MODEL, tb), F32),
                        pltpu.VMEM((ec, tb), F32), pltpu.VMEM((ec, tb), F32)],
        compiler_params=_cparams(("parallel", "arbitrary")),
        name="peer_dense",
    )(u, vt, h2t, e1, c1, n0, w0, x1, final_w)


def _group_matrix():
    g = np.zeros((PEER_TOPK, _N_CAND_PAD), np.float32)
    off = 0
    for r, ncol in enumerate(_CAND_COLS):
        g[r, off:off + ncol] = 1.0
        off += ncol
    return g


def _layer(x2, norm1_w, w_in, fox_f_bias, conv_w, conv_b, i_bias, f_bias, fox_nw, mlstm_nw, w_out, norm2_w,
           w_q, keys, u, v, final_w, *, batch, seq):
    splits = np.cumsum((FOX_WIDTH, FOX_WIDTH, FOX_WIDTH, FOX_HEADS, MLSTM_QK_WIDTH, MLSTM_QK_WIDTH,
                        MLSTM_V_WIDTH, MLSTM_HEADS, MLSTM_HEADS, MLSTM_V_WIDTH))[:-1]
    fq, fk, fv, ff, mq, mk, mv, mi, mf, mo = jnp.split(w_in, [int(p) for p in splits], axis=1)
    w_main = jnp.concatenate([fq, fk, fv, mv, mo, mq, mk], axis=1).astype(BF16)
    gate_pad = GATE_LANES - FOX_HEADS - 2 * MLSTM_HEADS
    w_gate = jnp.pad(jnp.concatenate([ff, mi, mf], axis=1), ((0, 0), (0, gate_pad))).astype(BF16)
    gate_bias = jnp.pad(jnp.concatenate([fox_f_bias, i_bias, f_bias]), (0, gate_pad)).reshape(1, GATE_LANES)
    tri = jnp.asarray(np.tril(np.ones((LANES, LANES), np.float32)), BF16)

    z, zc, g = _inproj(x2, norm1_w.reshape(1, D_MODEL), w_main, w_gate)
    pq, pk = _placement_matrices()
    qn, kn = _fox_norms(z, jnp.asarray(_head_group_matrix(), BF16))
    kmax = jnp.max(kn.reshape(batch, seq, GATE_LANES), axis=1)
    gc, gr, qx, kx = _gates(g, gate_bias, tri, jnp.asarray(pq, BF16), jnp.asarray(pk, BF16), qn,
                            jnp.repeat(kmax, SUBLANES, axis=0), batch=batch, seq=seq)
    lo, fast = _fox_plan(qn, kmax, gc, batch=batch, seq=seq, tk=FOX_KEY_BLOCK)
    att = _fox(z, qx, kx, fox_nw.reshape(FOX_HEADS, FOX_HEAD_DIM), lo, fast, batch=batch, seq=seq, tk=FOX_KEY_BLOCK)
    cell = _mlstm(zc, z, gc, gr, conv_w, conv_b.reshape(1, -1), mlstm_nw.reshape(1, -1), batch=batch, seq=seq)
    x1, h2t = _outproj(att, cell, x2, w_out[:FOX_WIDTH].astype(BF16), w_out[FOX_WIDTH:].astype(BF16),
                       norm2_w.reshape(1, D_MODEL))
    e1, c1, n0, w0 = _peer_sel(h2t, w_q.T.astype(BF16), keys.astype(BF16), jnp.asarray(_group_matrix(), BF16))
    return _peer_dense(u.astype(BF16), v.T.astype(BF16), h2t, e1, c1, n0, w0, x1, final_w)


def kernel(x, norm1_w, w_in, fox_f_bias, mlstm_conv_w, mlstm_conv_b, mlstm_i_bias, mlstm_f_bias, fox_out_norm_w,
           mlstm_out_norm_w, w_out, norm2_w, peer_w_q, peer_keys, peer_u, peer_v, final_norm_w):
    batch, seq, _ = x.shape
    assert w_in.shape[0] == 1, "single-layer block: the final norm is fused with the last residual add"
    x2 = x.reshape(batch * seq, D_MODEL)
    out = _layer(x2, norm1_w[0], w_in[0], fox_f_bias[0], mlstm_conv_w[0], mlstm_conv_b[0], mlstm_i_bias[0],
                 mlstm_f_bias[0], fox_out_norm_w[0], mlstm_out_norm_w[0], w_out[0], norm2_w[0],
                 peer_w_q[0], peer_keys[0], peer_u[0], peer_v[0], final_norm_w.reshape(1, D_MODEL),
                 batch=batch, seq=seq)
    return out.reshape(batch, seq, D_MODEL)
```

```python
import functools
import math

import numpy as np
import jax
import jax.numpy as jnp
from jax import lax
from jax.experimental import pallas as pl
from jax.experimental.pallas import tpu as pltpu

F32 = jnp.float32
BF16 = jnp.bfloat16

D_MODEL = 2048
FOX_HEADS = 8
FOX_HEAD_DIM = 128
FOX_WIDTH = FOX_HEADS * FOX_HEAD_DIM
MLSTM_HEADS = 4
MLSTM_QK_DIM = 128
MLSTM_V_DIM = 256
MLSTM_QK_WIDTH = MLSTM_HEADS * MLSTM_QK_DIM
MLSTM_V_WIDTH = MLSTM_HEADS * MLSTM_V_DIM
MLSTM_CONV = 4
MLSTM_CHUNK = 128
PEER_HEADS = 8
PEER_KEY_DIM = 256
PEER_N_KEYS = 128
PEER_TOPK = 16
PEER_N_EXPERTS = PEER_N_KEYS * PEER_N_KEYS
NORM_EPS = 1e-6
LOG2E = math.log2(math.e)
NORM_MARGIN = 1.0 + 2.0 ** -7
F32_ZERO_LOG2 = 150.0
FAST_GAP_LOG2 = 90.0
BOUND_SLACK_LOG2 = 4.0
FOX_KEY_BLOCK = 512

LANES = 128
SUBLANES = 8
GATE_LANES = LANES
GATE_ROWS = 16
VMEM_LIMIT = 56 * 1024 * 1024

_CAND_COLS = tuple(PEER_TOPK // (r + 1) for r in range(PEER_TOPK))
_N_CAND = sum(_CAND_COLS)
_N_CAND_PAD = -(-_N_CAND // SUBLANES) * SUBLANES


def _cparams(sem):
    return pltpu.CompilerParams(dimension_semantics=sem, vmem_limit_bytes=VMEM_LIMIT)


def _inproj_body(x_ref, nw_ref, w_ref, wg_ref, z_ref, zc_ref, g_ref, h_scr, *, n_main, q_blocks, q_scale):
    j = pl.program_id(1)

    @pl.when(j == 0)
    def _():
        x = x_ref[...]
        ms = jnp.mean(x * x, axis=-1, keepdims=True)
        hb = ((x * lax.rsqrt(ms + NORM_EPS)) * nw_ref[...]).astype(BF16)
        h_scr[...] = hb
        g_ref[...] = jnp.dot(hb, wg_ref[...], preferred_element_type=F32)

    z = jnp.dot(h_scr[...], w_ref[...], preferred_element_type=F32)

    @pl.when(j < n_main)
    def _():
        scale = jnp.where(j < q_blocks, q_scale, 1.0).astype(F32)
        z_ref[...] = (z * scale).astype(BF16)

    @pl.when(j >= n_main)
    def _():
        zc_ref[...] = z


def _inproj(x2, norm_w, w_main, w_gate, *, tm=1024, tn=1024):
    T = x2.shape[0]
    n_cols = w_main.shape[1]
    n_conv = 2 * MLSTM_QK_WIDTH
    n_main = (n_cols - n_conv) // tn
    n_blocks = n_cols // tn
    body = functools.partial(_inproj_body, n_main=n_main, q_blocks=FOX_WIDTH // tn,
                             q_scale=FOX_HEAD_DIM ** -0.5 * LOG2E)
    return pl.pallas_call(
        body,
        grid=(T // tm, n_blocks),
        in_specs=[
            pl.BlockSpec((tm, D_MODEL), lambda i, j: (i, 0)),
            pl.BlockSpec((1, D_MODEL), lambda i, j: (0, 0)),
            pl.BlockSpec((D_MODEL, tn), lambda i, j: (0, j)),
            pl.BlockSpec((D_MODEL, GATE_LANES), lambda i, j: (0, 0)),
        ],
        out_specs=[
            pl.BlockSpec((tm, tn), lambda i, j: (i, jnp.minimum(j, n_main - 1))),
            pl.BlockSpec((tm, tn), lambda i, j: (i, jnp.maximum(j - n_main, 0))),
            pl.BlockSpec((tm, GATE_LANES), lambda i, j: (i, 0)),
        ],
        out_shape=[
            jax.ShapeDtypeStruct((T, n_cols - n_conv), BF16),
            jax.ShapeDtypeStruct((T, n_conv), F32),
            jax.ShapeDtypeStruct((T, GATE_LANES), F32),
        ],
        scratch_shapes=[pltpu.VMEM((tm, D_MODEL), BF16)],
        compiler_params=_cparams(("parallel", "arbitrary")),
        name="inproj",
    )(x2, norm_w, w_main, w_gate)


def _split3(v):
    hi = v.astype(BF16)
    r1 = v - hi.astype(F32)
    mid = r1.astype(BF16)
    lo = (r1 - mid.astype(F32)).astype(BF16)
    return hi, mid, lo


def _fox_norms_body(q_ref, k_ref, grp_ref, qn_ref, kn_ref):
    for src, dst in ((q_ref, qn_ref), (k_ref, kn_ref)):
        x = src[...].astype(F32)
        ss = jnp.dot((x * x).astype(BF16), grp_ref[...], preferred_element_type=F32)
        dst[...] = jnp.sqrt(ss) * NORM_MARGIN


def _head_group_matrix():
    g = np.zeros((FOX_WIDTH, GATE_LANES), np.float32)
    for h in range(FOX_HEADS):
        g[h * FOX_HEAD_DIM:(h + 1) * FOX_HEAD_DIM, h] = 1.0
    return g


def _fox_norms(z, grp, *, rows=1024):
    T = z.shape[0]
    out = jax.ShapeDtypeStruct((T, GATE_LANES), F32)
    return pl.pallas_call(
        _fox_norms_body,
        grid=(T // rows,),
        in_specs=[
            pl.BlockSpec((rows, FOX_WIDTH), lambda i: (i, 0)),
            pl.BlockSpec((rows, FOX_WIDTH), lambda i: (i, 1)),
            pl.BlockSpec((FOX_WIDTH, GATE_LANES), lambda i: (0, 0)),
        ],
        out_specs=[pl.BlockSpec((rows, GATE_LANES), lambda i: (i, 0))] * 2,
        out_shape=[out, out],
        compiler_params=_cparams(("parallel",)),
        name="fox_norms",
    )(z, z, grp)


def _gates_body(g_ref, bias_ref, tri_ref, pq_ref, pk_ref, qn_ref, kmax_ref, gc_ref, gr_ref, qx_ref, kx_ref,
                carry_scr, *, rows):
    c = pl.program_id(1)

    @pl.when(c == 0)
    def _():
        carry_scr[...] = jnp.zeros_like(carry_scr)

    lane = lax.broadcasted_iota(jnp.int32, (LANES, GATE_LANES), 1)
    is_glob = lane < FOX_HEADS
    is_ls = is_glob | ((lane >= FOX_HEADS + MLSTM_HEADS) & (lane < FOX_HEADS + 2 * MLSTM_HEADS))
    tri = tri_ref[...]
    for s in range(rows // LANES):
        sl = slice(s * LANES, (s + 1) * LANES)
        g = g_ref[sl, :] + bias_ref[...]
        ls = jnp.minimum(g, 0.0) - jnp.log1p(jnp.exp(-jnp.abs(g)))
        v = jnp.where(is_ls, ls, 0.0)
        hi, mid, lo = _split3(v)
        cs = (jnp.dot(tri, hi, preferred_element_type=F32)
              + jnp.dot(tri, mid, preferred_element_type=F32)
              + jnp.dot(tri, lo, preferred_element_type=F32))
        glob = cs + carry_scr[...]
        carry_scr[...] = glob[LANES - 1:LANES, :]
        out = jnp.where(is_glob, glob, jnp.where(is_ls, cs, g))
        gc_ref[sl, :] = out
        gr_ref[:, sl] = out.T[0:GATE_ROWS, :]
        groups = (*_split3(glob * LOG2E), jnp.ones((LANES, GATE_LANES), BF16),
                  *_split3(qn_ref[sl, :] * kmax_ref[0:1, :] + 1.0))
        pieces = jnp.zeros((LANES, GATE_LANES), F32)
        for gi, grp in enumerate(groups):
            moved = pltpu.roll(grp.astype(F32), gi * FOX_HEADS, axis=1) if gi else grp.astype(F32)
            pieces = jnp.where((lane >= gi * FOX_HEADS) & (lane < (gi + 1) * FOX_HEADS), moved, pieces)
        pieces = pieces.astype(BF16)
        qx_ref[sl, :] = jnp.dot(pieces, pq_ref[...], preferred_element_type=F32).astype(BF16)
        kx_ref[sl, :] = jnp.dot(pieces, pk_ref[...], preferred_element_type=F32).astype(BF16)


_N_PIECES = 3
_PIECE_GROUPS = 2 * _N_PIECES + 1
_STAB_SLOT = 2 * _N_PIECES


def _placement_matrices():
    n = _N_PIECES
    assert _PIECE_GROUPS * FOX_HEADS <= GATE_LANES
    pq = np.zeros((GATE_LANES, FOX_WIDTH), np.float32)
    pk = np.zeros((GATE_LANES, FOX_WIDTH), np.float32)
    for h in range(FOX_HEADS):
        for p in range(n):
            pq[p * FOX_HEADS + h, h * FOX_HEAD_DIM + p] = 1.0
            pk[n * FOX_HEADS + h, h * FOX_HEAD_DIM + p] = 1.0
            pq[n * FOX_HEADS + h, h * FOX_HEAD_DIM + n + p] = 1.0
            pk[p * FOX_HEADS + h, h * FOX_HEAD_DIM + n + p] = -1.0
            pq[(n + 1 + p) * FOX_HEADS + h, h * FOX_HEAD_DIM + _STAB_SLOT + p] = -1.0
            pk[n * FOX_HEADS + h, h * FOX_HEAD_DIM + _STAB_SLOT + p] = 1.0
    return pq, pk


def _gates(g, bias, tri, pq, pk, qn, kmax, *, batch, seq, rows=1024):
    T = g.shape[0]
    nblk = seq // rows
    return pl.pallas_call(
        functools.partial(_gates_body, rows=rows),
        grid=(batch, nblk),
        in_specs=[
            pl.BlockSpec((rows, GATE_LANES), lambda b, c: (b * nblk + c, 0)),
            pl.BlockSpec((1, GATE_LANES), lambda b, c: (0, 0)),
            pl.BlockSpec((LANES, LANES), lambda b, c: (0, 0)),
            pl.BlockSpec((GATE_LANES, FOX_WIDTH), lambda b, c: (0, 0)),
            pl.BlockSpec((GATE_LANES, FOX_WIDTH), lambda b, c: (0, 0)),
            pl.BlockSpec((rows, GATE_LANES), lambda b, c: (b * nblk + c, 0)),
            pl.BlockSpec((SUBLANES, GATE_LANES), lambda b, c: (b, 0)),
        ],
        out_specs=[
            pl.BlockSpec((rows, GATE_LANES), lambda b, c: (b * nblk + c, 0)),
            pl.BlockSpec((GATE_ROWS, rows), lambda b, c: (0, b * nblk + c)),
            pl.BlockSpec((rows, FOX_WIDTH), lambda b, c: (b * nblk + c, 0)),
            pl.BlockSpec((rows, FOX_WIDTH), lambda b, c: (b * nblk + c, 0)),
        ],
        out_shape=[
            jax.ShapeDtypeStruct((T, GATE_LANES), F32),
            jax.ShapeDtypeStruct((GATE_ROWS, T), F32),
            jax.ShapeDtypeStruct((T, FOX_WIDTH), BF16),
            jax.ShapeDtypeStruct((T, FOX_WIDTH), BF16),
        ],
        scratch_shapes=[pltpu.VMEM((1, GATE_LANES), F32)],
        compiler_params=_cparams(("parallel", "arbitrary")),
        name="gates",
    )(g, bias, tri, pq, pk, qn, kmax)


def _fox_body(lo_ref, fast_ref, q_ref, qx_ref, k_ref, kx_ref, v_ref, nw_ref, o_ref, *, tk, nq):
    b = pl.program_id(0)
    h = pl.program_id(1)
    qi = pl.program_id(2)
    tile = (b * FOX_HEADS + h) * nq + qi
    lo = lo_ref[tile]
    row = lax.broadcasted_iota(jnp.int32, (tk, tk), 0)
    col = lax.broadcasted_iota(jnp.int32, (tk, tk), 1)

    def load_kv(ki):
        start = pl.multiple_of(ki * tk, tk)
        k = jnp.concatenate([k_ref[pl.ds(start, tk), :], kx_ref[pl.ds(start, tk), :]], axis=1)
        return k, v_ref[pl.ds(start, tk), :]

    def logits(q, k, masked):
        s = lax.dot_general(q, k, (((1,), (1,)), ((), ())), preferred_element_type=F32)
        return jnp.where(row >= col, s, -jnp.inf) if masked else s

    def finish(r, num, den):
        out = num / den
        ms = jnp.mean(out * out, axis=-1, keepdims=True)
        o_ref[r * tk:(r + 1) * tk, :] = ((out * lax.rsqrt(ms + NORM_EPS)) * nw_ref[pl.ds(h, 1), :]).astype(BF16)

    @pl.when(fast_ref[tile] == 1)
    def _():
        q_halves = [jnp.concatenate([q_ref[r * tk:(r + 1) * tk, :], qx_ref[r * tk:(r + 1) * tk, :]], axis=1)
                    for r in range(2)]
        ones_col = (lax.broadcasted_iota(jnp.int32, (tk, LANES), 1) == 0).astype(BF16)

        def weights(r, k, masked):
            return jnp.exp2(logits(q_halves[r], k, masked)).astype(BF16)

        def values(ki):
            return jnp.concatenate([load_kv(ki)[1], ones_col], axis=1)

        def add(acc, p, va):
            return acc + jnp.dot(p, va, preferred_element_type=F32)

        def pair(kp, accs):
            blocks = [2 * kp, 2 * kp + 1]
            ps = [[weights(r, load_kv(ki)[0], False) for r in range(2)] for ki in blocks]
            for j, ki in enumerate(blocks):
                va = values(ki)
                accs = tuple(add(accs[r], ps[j][r], va) for r in range(2))
            return accs

        zero = jnp.zeros((tk, 2 * FOX_HEAD_DIM), F32)
        aa, ab = lax.fori_loop(lo // 2, qi, pair, (zero, zero))
        k = load_kv(2 * qi)[0]
        pa, pb = weights(0, k, True), weights(1, k, False)
        pb2 = weights(1, load_kv(2 * qi + 1)[0], True)
        va = values(2 * qi)
        aa, ab = add(aa, pa, va), add(ab, pb, va)
        ab = add(ab, pb2, values(2 * qi + 1))
        for r, acc in enumerate((aa, ab)):
            finish(r, acc[:, 0:FOX_HEAD_DIM], acc[:, FOX_HEAD_DIM:FOX_HEAD_DIM + 1])

    @pl.when(fast_ref[tile] == 0)
    def _():
        _fox_online(q_ref, qx_ref, load_kv, logits, finish, lo, qi, tk)


def _fox_online(q_ref, qx_ref, load_kv, logits, finish, lo, qi, tk):
    lane = lax.broadcasted_iota(jnp.int32, (tk, LANES), 1)
    no_stab = (lane < _STAB_SLOT) | (lane >= _STAB_SLOT + _N_PIECES)
    q_halves = [jnp.concatenate([q_ref[r * tk:(r + 1) * tk, :],
                                 jnp.where(no_stab, qx_ref[r * tk:(r + 1) * tk, :], jnp.zeros((), BF16))], axis=1)
                for r in range(2)]

    def weights(s, carry):
        m, l, acc = carry
        cols = [s[:, c * LANES:(c + 1) * LANES] for c in range(tk // LANES)]
        m_new = jnp.maximum(m, jnp.max(functools.reduce(jnp.maximum, cols), axis=-1, keepdims=True))
        alpha = jnp.exp2(m - m_new)
        ps = [jnp.exp2(c - m_new) for c in cols]
        l = alpha * l + functools.reduce(jnp.add, ps)
        return m_new, l, alpha * acc, jnp.concatenate(ps, axis=1).astype(BF16)

    def update(q, k, v, carry, masked):
        m, l, acc, p = weights(logits(q, k, masked), carry)
        return m, l, acc + jnp.dot(p, v, preferred_element_type=F32)

    def both(ki, carries):
        k, v = load_kv(ki)
        s = [logits(q_halves[r], k, False) for r in range(2)]
        w = [weights(s[r], carries[r]) for r in range(2)]
        return tuple((m, l, acc + jnp.dot(p, v, preferred_element_type=F32)) for m, l, acc, p in w)

    init = (jnp.full((tk, LANES), -jnp.inf, F32), jnp.zeros((tk, LANES), F32), jnp.zeros((tk, FOX_HEAD_DIM), F32))
    ca, cb = lax.fori_loop(lo, 2 * qi, both, (init, init))
    k, v = load_kv(2 * qi)
    ca = update(q_halves[0], k, v, ca, True)
    cb = update(q_halves[1], k, v, cb, False)
    k, v = load_kv(2 * qi + 1)
    cb = update(q_halves[1], k, v, cb, True)
    for r, (_, l, acc) in enumerate((ca, cb)):
        finish(r, acc, jnp.sum(l, axis=-1, keepdims=True))


def _fox_plan(qn, kmax, gc, *, batch, seq, tk):
    tq = 2 * tk
    nq = seq // tq
    nk = seq // tk
    heads = slice(0, FOX_HEADS)
    qmax = jnp.max(qn.reshape(batch, nq, tq, GATE_LANES), axis=2)[..., heads]
    gap = 2.0 * qmax * kmax[:, None, heads] + BOUND_SLACK_LOG2
    f2 = (gc[:, heads] * LOG2E).reshape(batch, nk, tk, FOX_HEADS)
    f_first = f2[:, ::2, 0, :]
    f_last = f2[:, :, tk - 1, :]
    bound = gap[:, :, None, :] + f_first[:, :, None, :] - f_last[:, None, :, :]
    below_diag = jnp.arange(nk)[None, :, None] < 2 * jnp.arange(nq)[:, None, None]
    skip = (bound < -F32_ZERO_LOG2) & below_diag[None]
    lo = jnp.sum(jnp.cumprod(skip.astype(jnp.int32), axis=2), axis=2)
    fast = (gap <= FAST_GAP_LOG2).astype(jnp.int32)
    flat = lambda a: jnp.transpose(a, (0, 2, 1)).reshape(-1)
    return flat(lo), flat(fast)


def _fox(z, qx, kx, nw, lo, fast, *, batch, seq, tk=512):
    T = z.shape[0]
    tq = 2 * tk
    nq = seq // tq
    kcol = FOX_WIDTH // FOX_HEAD_DIM
    grid_spec = pltpu.PrefetchScalarGridSpec(
        num_scalar_prefetch=2,
        grid=(batch, FOX_HEADS, nq),
        in_specs=[
            pl.BlockSpec((tq, FOX_HEAD_DIM), lambda b, h, i, lo, fast: (b * nq + i, h)),
            pl.BlockSpec((tq, FOX_HEAD_DIM), lambda b, h, i, lo, fast: (b * nq + i, h)),
            pl.BlockSpec((seq, FOX_HEAD_DIM), lambda b, h, i, lo, fast: (b, kcol + h)),
            pl.BlockSpec((seq, FOX_HEAD_DIM), lambda b, h, i, lo, fast: (b, h)),
            pl.BlockSpec((seq, FOX_HEAD_DIM), lambda b, h, i, lo, fast: (b, 2 * kcol + h)),
            pl.BlockSpec((FOX_HEADS, FOX_HEAD_DIM), lambda b, h, i, lo, fast: (0, 0)),
        ],
        out_specs=pl.BlockSpec((tq, FOX_HEAD_DIM), lambda b, h, i, lo, fast: (b * nq + i, h)),
    )
    return pl.pallas_call(
        functools.partial(_fox_body, tk=tk, nq=nq),
        grid_spec=grid_spec,
        out_shape=jax.ShapeDtypeStruct((T, FOX_WIDTH), BF16),
        compiler_params=_cparams(("parallel", "parallel", "arbitrary")),
        name="fox",
    )(lo, fast, z, qx, z, kx, z, nw)


def _mlstm_body(zc_ref, zprev_ref, v_ref, o_ref, gc_ref, gr_ref, cw_ref, cb_ref, nw_ref, out_ref,
                full_scr, c_scr, n_scr, m_scr):
    c = pl.program_id(1)
    L = MLSTM_CHUNK
    dk = MLSTM_QK_DIM
    dv = MLSTM_V_DIM
    rows = L

    @pl.when(c == 0)
    def _():
        c_scr[...] = jnp.zeros_like(c_scr)
        n_scr[...] = jnp.zeros_like(n_scr)
        m_scr[...] = jnp.zeros_like(m_scr)

    full_scr[0:SUBLANES, :] = jnp.where(c == 0, 0.0, zprev_ref[...])
    full_scr[SUBLANES:SUBLANES + rows, :] = zc_ref[...]
    y = cb_ref[...]
    for j in range(MLSTM_CONV):
        y = y + cw_ref[j:j + 1, :] * full_scr[pl.ds(SUBLANES - (MLSTM_CONV - 1) + j, rows), :]
    qk_all = y * jax.nn.sigmoid(y)

    row = lax.broadcasted_iota(jnp.int32, (L, L), 0)
    col = lax.broadcasted_iota(jnp.int32, (L, L), 1)
    causal = row >= col
    heads = range(MLSTM_HEADS)
    gcb = gc_ref[...]
    grb = gr_ref[...]
    qh = [qk_all[:, hh * dk:(hh + 1) * dk] for hh in heads]
    kh = [qk_all[:, MLSTM_QK_WIDTH + hh * dk:MLSTM_QK_WIDTH + (hh + 1) * dk] * (dk ** -0.5) for hh in heads]
    vh = [v_ref[:, hh * dv:(hh + 1) * dv] for hh in heads]
    i_col = [gcb[:, FOX_HEADS + hh:FOX_HEADS + hh + 1] for hh in heads]
    b_col = [gcb[:, FOX_HEADS + MLSTM_HEADS + hh:FOX_HEADS + MLSTM_HEADS + hh + 1] for hh in heads]
    i_row = [grb[FOX_HEADS + hh:FOX_HEADS + hh + 1, :] for hh in heads]
    b_row = [grb[FOX_HEADS + MLSTM_HEADS + hh:FOX_HEADS + MLSTM_HEADS + hh + 1, :] for hh in heads]
    b_last = [b_row[hh][:, L - 1:L] for hh in heads]
    c_prev = [c_scr[hh] for hh in heads]
    n_prev = [n_scr[hh] for hh in heads]
    m_prev = [m_scr[hh][:, 0:1] for hh in heads]

    m_loc = [jnp.max(b_last[hh] - b_row[hh] + i_row[hh], axis=-1, keepdims=True) for hh in heads]
    kw = [kh[hh] * jnp.exp(b_last[hh] - b_col[hh] + i_col[hh] - m_loc[hh]) for hh in heads]
    g_col = [b_col[hh] + m_prev[hh] for hh in heads]
    dmat = [jnp.where(causal, b_col[hh] - b_row[hh] + i_row[hh], -jnp.inf) for hh in heads]
    m_t = [jnp.maximum(g_col[hh], jnp.max(dmat[hh], axis=-1, keepdims=True)) for hh in heads]
    decay = [jnp.exp(dmat[hh] - m_t[hh]) for hh in heads]
    inter = [jnp.exp(g_col[hh] - m_t[hh]) for hh in heads]
    qb = [qh[hh].astype(BF16) for hh in heads]
    qk = [lax.dot_general(qb[hh], kh[hh].astype(BF16), (((1,), (1,)), ((), ())), preferred_element_type=F32)
          for hh in heads]
    carried = [jnp.dot(qb[hh], c_prev[hh].astype(BF16), preferred_element_type=F32) for hh in heads]
    c_loc = [lax.dot_general(kw[hh].astype(BF16), vh[hh], (((0,), (0,)), ((), ())), preferred_element_type=F32)
             for hh in heads]
    sm = [qk[hh] * decay[hh] for hh in heads]
    intra = [jnp.dot(sm[hh].astype(BF16), vh[hh], preferred_element_type=F32) for hh in heads]
    den = [jnp.sum(sm[hh], axis=-1, keepdims=True)
           + inter[hh] * jnp.sum(qh[hh] * n_prev[hh], axis=-1, keepdims=True) for hh in heads]
    for hh in heads:
        cell = (intra[hh] + inter[hh] * carried[hh]) / jnp.maximum(jnp.abs(den[hh]), jnp.exp(-m_t[hh]))
        gated = jax.nn.sigmoid(o_ref[:, hh * dv:(hh + 1) * dv].astype(F32)) * cell
        ms = jnp.mean(gated * gated, axis=-1, keepdims=True)
        out_ref[:, hh * dv:(hh + 1) * dv] = (
            (gated * lax.rsqrt(ms + NORM_EPS)) * nw_ref[:, hh * dv:(hh + 1) * dv]).astype(BF16)
    for hh in heads:
        m_new = jnp.maximum(b_last[hh] + m_prev[hh], m_loc[hh])
        a_prev = jnp.exp(b_last[hh] + m_prev[hh] - m_new)
        a_loc = jnp.exp(m_loc[hh] - m_new)
        c_scr[hh] = a_prev * c_prev[hh] + a_loc * c_loc[hh]
        n_scr[hh] = a_prev * n_prev[hh] + a_loc * jnp.sum(kw[hh], axis=0, keepdims=True)
        m_scr[hh] = jnp.broadcast_to(m_new, (1, LANES))


def _mlstm(zc, z, gc, gr, conv_w, conv_b, nw, *, batch, seq):
    T = zc.shape[0]
    L = MLSTM_CHUNK
    nc = seq // L
    per = L // SUBLANES
    vcol = 3 * FOX_WIDTH // MLSTM_V_WIDTH
    return pl.pallas_call(
        _mlstm_body,
        grid=(batch, nc),
        in_specs=[
            pl.BlockSpec((L, 2 * MLSTM_QK_WIDTH), lambda b, c: (b * nc + c, 0)),
            pl.BlockSpec((SUBLANES, 2 * MLSTM_QK_WIDTH), lambda b, c: (jnp.maximum((b * nc + c) * per - 1, 0), 0)),
            pl.BlockSpec((L, MLSTM_V_WIDTH), lambda b, c: (b * nc + c, vcol)),
            pl.BlockSpec((L, MLSTM_V_WIDTH), lambda b, c: (b * nc + c, vcol + 1)),
            pl.BlockSpec((L, GATE_LANES), lambda b, c: (b * nc + c, 0)),
            pl.BlockSpec((GATE_ROWS, L), lambda b, c: (0, b * nc + c)),
            pl.BlockSpec((MLSTM_CONV, 2 * MLSTM_QK_WIDTH), lambda b, c: (0, 0)),
            pl.BlockSpec((1, 2 * MLSTM_QK_WIDTH), lambda b, c: (0, 0)),
            pl.BlockSpec((1, MLSTM_V_WIDTH), lambda b, c: (0, 0)),
        ],
        out_specs=pl.BlockSpec((L, MLSTM_V_WIDTH), lambda b, c: (b * nc + c, 0)),
        out_shape=jax.ShapeDtypeStruct((T, MLSTM_V_WIDTH), BF16),
        scratch_shapes=[
            pltpu.VMEM((SUBLANES + L, 2 * MLSTM_QK_WIDTH), F32),
            pltpu.VMEM((MLSTM_HEADS, MLSTM_QK_DIM, MLSTM_V_DIM), F32),
            pltpu.VMEM((MLSTM_HEADS, 1, MLSTM_QK_DIM), F32),
            pltpu.VMEM((MLSTM_HEADS, 1, LANES), F32),
        ],
        compiler_params=_cparams(("parallel", "arbitrary")),
        name="mlstm",
    )(zc, zc, z, z, gc, gr, conv_w, conv_b, nw)


def _outproj_body(att_ref, cell_ref, x_ref, wa_ref, wb_ref, n2_ref, x1_ref, h2t_ref):
    y = (jnp.dot(att_ref[...], wa_ref[...], preferred_element_type=F32)
         + jnp.dot(cell_ref[...], wb_ref[...], preferred_element_type=F32))
    x1 = x_ref[...] + y
    x1_ref[...] = x1
    ms = jnp.mean(x1 * x1, axis=-1, keepdims=True)
    h2 = (x1 * lax.rsqrt(ms + NORM_EPS)) * n2_ref[...]
    h2t_ref[...] = h2.T.astype(BF16)


def _outproj(att, cell, x2, wa, wb, n2, *, tm=512):
    T = x2.shape[0]
    return pl.pallas_call(
        _outproj_body,
        grid=(T // tm,),
        in_specs=[
            pl.BlockSpec((tm, FOX_WIDTH), lambda i: (i, 0)),
            pl.BlockSpec((tm, MLSTM_V_WIDTH), lambda i: (i, 0)),
            pl.BlockSpec((tm, D_MODEL), lambda i: (i, 0)),
            pl.BlockSpec((FOX_WIDTH, D_MODEL), lambda i: (0, 0)),
            pl.BlockSpec((MLSTM_V_WIDTH, D_MODEL), lambda i: (0, 0)),
            pl.BlockSpec((1, D_MODEL), lambda i: (0, 0)),
        ],
        out_specs=[
            pl.BlockSpec((tm, D_MODEL), lambda i: (i, 0)),
            pl.BlockSpec((D_MODEL, tm), lambda i: (0, i)),
        ],
        out_shape=[
            jax.ShapeDtypeStruct((T, D_MODEL), F32),
            jax.ShapeDtypeStruct((D_MODEL, T), BF16),
        ],
        compiler_params=_cparams(("parallel",)),
        name="outproj",
    )(att, cell, x2, wa, wb, n2)


def _row_iota(n_rows, tb):
    return lax.broadcasted_iota(jnp.int32, (n_rows, tb), 0).astype(F32)


def _extract16(s, val_ref, idx_ref):
    n_rows, tb = s.shape
    iota = _row_iota(n_rows, tb)

    def body(r, sw):
        mx = jnp.max(sw, axis=0, keepdims=True)
        first = jnp.min(jnp.where(sw == mx, iota, float(n_rows)), axis=0, keepdims=True)
        val_ref[pl.ds(r, 1), :] = mx
        idx_ref[pl.ds(r, 1), :] = first
        return jnp.where(iota == first, -jnp.inf, sw)

    lax.fori_loop(0, PEER_TOPK, body, s)


def _scatter_rounds(idx, values, n_rows, fill):
    tb = idx.shape[1]
    iota = _row_iota(n_rows, tb)
    table = jnp.full((n_rows, tb), fill, F32)
    for r in range(PEER_TOPK):
        v = values[r:r + 1, :] if values is not None else float(r)
        table = jnp.where(iota == idx[r:r + 1, :], v, table)
    return table


def _extract16_distinct(arrays, val_refs):
    def body(r, carry):
        out = []
        for sw, val_ref in zip(carry, val_refs):
            mx = jnp.max(sw, axis=0, keepdims=True)
            val_ref[pl.ds(r, 1), :] = mx
            out.append(jnp.where(sw == mx, -jnp.inf, sw))
        return tuple(out)

    lax.fori_loop(0, PEER_TOPK, body, tuple(arrays), unroll=True)


def _count_ge(s, thr):
    return jnp.sum((s >= thr).astype(F32), axis=0, keepdims=True)


def _match_rounds(s, vals, values, fill):
    table = jnp.full(s.shape, fill, F32)
    for r in range(PEER_TOPK):
        v = values[r:r + 1, :] if values is not None else float(r)
        table = jnp.where(s == vals[r:r + 1, :], v, table)
    return table


def _peer_sel_body(h2t_ref, h2t_next_ref, wqt_ref, keys_ref, grp_ref, e1_ref, c1_ref, n0_ref, w0_ref,
                   qt_scr, sc_scr, vals_scr, idx_scr, cand_scr, pe_scr, *, tb):
    half = PEER_KEY_DIM // 2

    def queries(src_ref):
        qt_scr[...] = jnp.dot(wqt_ref[...], src_ref[...], preferred_element_type=F32).astype(BF16)

    @pl.when(pl.program_id(0) == 0)
    def _():
        queries(h2t_ref)

    for hp in range(2 * PEER_HEADS):
        sc_scr[hp] = jnp.dot(keys_ref[hp % 2], qt_scr[hp * half:(hp + 1) * half, :], preferred_element_type=F32)
    queries(h2t_next_ref)

    def head_tables(h, exact):
        s0 = sc_scr[2 * h]
        s1 = sc_scr[2 * h + 1]
        if exact:
            _extract16(s0, vals_scr.at[0], idx_scr.at[0])
            _extract16(s1, vals_scr.at[1], idx_scr.at[1])
        else:
            _extract16_distinct([s0], [vals_scr.at[0]])
            _extract16_distinct([s1], [vals_scr.at[1]])
        a = vals_scr[0]
        b = vals_scr[1]
        ea = jnp.exp(a - a[0:1, :])
        eb = jnp.exp(b - b[0:1, :])
        off = 0
        for r in range(PEER_TOPK):
            ncol = _CAND_COLS[r]
            cand_scr[off:off + ncol, :] = a[r:r + 1, :] + b[0:ncol, :]
            pe_scr[off:off + ncol, :] = ea[r:r + 1, :] * eb[0:ncol, :]
            off += ncol
        cand_scr[_N_CAND:_N_CAND_PAD, :] = jnp.full((_N_CAND_PAD - _N_CAND, tb), -jnp.inf, F32)
        pe_scr[_N_CAND:_N_CAND_PAD, :] = jnp.zeros((_N_CAND_PAD - _N_CAND, tb), F32)
        cand = cand_scr[...]
        last = PEER_TOPK - 1
        if exact:
            _extract16(cand, vals_scr.at[2], idx_scr.at[2])
            sel = _scatter_rounds(idx_scr[2], jnp.ones((PEER_TOPK, tb), F32), _N_CAND_PAD, 0.0)
            ties = None
        else:
            _extract16_distinct([cand], [vals_scr.at[2]])
            sel = (cand >= vals_scr[2, last:last + 1, :]).astype(F32)
            ties = ((_count_ge(s0, a[last:last + 1, :]) != float(PEER_TOPK)).astype(F32)
                    + (_count_ge(s1, b[last:last + 1, :]) != float(PEER_TOPK)).astype(F32)
                    + (jnp.sum(sel, axis=0, keepdims=True) != float(PEER_TOPK)).astype(F32))
        zsum = jnp.sum(sel * pe_scr[...], axis=0, keepdims=True)
        nr = jnp.dot(grp_ref[...], sel.astype(BF16), preferred_element_type=F32)
        if exact:
            n0 = _scatter_rounds(idx_scr[0], nr, PEER_N_KEYS, 0.0)
            c1 = _scatter_rounds(idx_scr[1], None, PEER_N_KEYS, float(PEER_TOPK))
        else:
            n0 = _match_rounds(s0, a, nr, 0.0)
            c1 = _match_rounds(s1, b, None, float(PEER_TOPK))
        n0_ref[h] = n0
        w0_ref[h] = jnp.exp(s0 - a[0:1, :]) * (0.5 / zsum)
        e1_ref[h] = jnp.exp(s1 - b[0:1, :]).astype(BF16)
        c1_ref[h] = c1.astype(BF16)
        return ties

    for h in range(PEER_HEADS):
        ties = head_tables(h, False)

        @pl.when(jnp.max(ties) > 0.0)
        def _():
            head_tables(h, True)


def _peer_sel(h2t, wqt, keys, grp, *, tb=256):
    T = h2t.shape[1]
    tab = jax.ShapeDtypeStruct((PEER_HEADS, PEER_N_KEYS, T), F32)
    tab16 = jax.ShapeDtypeStruct((PEER_HEADS, PEER_N_KEYS, T), BF16)
    tab_spec = pl.BlockSpec((PEER_HEADS, PEER_N_KEYS, tb), lambda i: (0, 0, i))
    n_blocks = T // tb
    return pl.pallas_call(
        functools.partial(_peer_sel_body, tb=tb),
        grid=(n_blocks,),
        in_specs=[
            pl.BlockSpec((D_MODEL, tb), lambda i: (0, 0)),
            pl.BlockSpec((D_MODEL, tb), lambda i: (0, jnp.minimum(i + 1, n_blocks - 1))),
            pl.BlockSpec((PEER_HEADS * PEER_KEY_DIM, D_MODEL), lambda i: (0, 0)),
            pl.BlockSpec((2, PEER_N_KEYS, PEER_KEY_DIM // 2), lambda i: (0, 0, 0)),
            pl.BlockSpec((PEER_TOPK, _N_CAND_PAD), lambda i: (0, 0)),
        ],
        out_specs=[tab_spec, tab_spec, tab_spec, tab_spec],
        out_shape=[tab16, tab16, tab, tab],
        scratch_shapes=[
            pltpu.VMEM((PEER_HEADS * PEER_KEY_DIM, tb), BF16),
            pltpu.VMEM((2 * PEER_HEADS, PEER_N_KEYS, tb), F32),
            pltpu.VMEM((3, PEER_TOPK, tb), F32),
            pltpu.VMEM((3, PEER_TOPK, tb), F32),
            pltpu.VMEM((_N_CAND_PAD, tb), F32),
            pltpu.VMEM((_N_CAND_PAD, tb), F32),
        ],
        compiler_params=_cparams(("arbitrary",)),
        name="peer_sel",
    )(h2t, h2t, wqt, keys, grp)


def _peer_dense_body(u_ref, vt_ref, h2t_ref, e1_ref, c1_ref, n0_ref, w0_ref, x1_ref, fw_ref, out_ref,
                     yt_ref, st_a, st_b, *, ec, tb, nk):
    k = pl.program_id(1)
    slabs = ec // PEER_N_KEYS

    def pre_activations(st_w):
        st_w[...] = jnp.dot(u_ref[...], h2t_ref[...], preferred_element_type=F32)

    def activation_slab(st_r, ii):
        g = jnp.zeros((PEER_N_KEYS, tb), BF16)
        for h in range(PEER_HEADS):
            n_row = n0_ref[h, ii:ii + 1, :].astype(BF16)
            w_row = w0_ref[h, ii:ii + 1, :].astype(BF16)
            g = g + jnp.where(c1_ref[h] < n_row, e1_ref[h] * w_row, jnp.zeros((), BF16))
        x = st_r[ii * PEER_N_KEYS:(ii + 1) * PEER_N_KEYS, :]
        act = x * (1.0 + lax.erf(x * math.sqrt(0.5)))
        return act.astype(BF16) * g

    def activate_and_project(st_r):
        at = jnp.concatenate([activation_slab(st_r, ii) for ii in range(slabs)], axis=0)
        yt_ref[...] += jnp.dot(vt_ref[...], at, preferred_element_type=F32)

    @pl.when(k == 0)
    def _():
        yt_ref[...] = jnp.zeros_like(yt_ref)
        pre_activations(st_a)

    @pl.when((k > 0) & (k < nk) & (k % 2 == 1))
    def _():
        pre_activations(st_b)
        activate_and_project(st_a)

    @pl.when((k > 0) & (k < nk) & (k % 2 == 0))
    def _():
        pre_activations(st_a)
        activate_and_project(st_b)

    @pl.when(k == nk)
    def _():
        activate_and_project(st_b if nk % 2 == 0 else st_a)
        x2 = x1_ref[...] + yt_ref[...].T
        ms = jnp.mean(x2 * x2, axis=-1, keepdims=True)
        out_ref[...] = (x2 * lax.rsqrt(ms + NORM_EPS)) * fw_ref[...]


def _peer_dense(u, vt, h2t, e1, c1, n0, w0, x1, final_w, *, tb=512, ec=1024):
    T = h2t.shape[1]
    nk = PEER_N_EXPERTS // ec
    slabs = ec // PEER_N_KEYS
    assert slabs == SUBLANES, "one f32 sublane tile of per-slab gate rows per expert chunk"
    tab_spec = pl.BlockSpec((PEER_HEADS, PEER_N_KEYS, tb), lambda i, k: (0, 0, i))
    row_spec = pl.BlockSpec((PEER_HEADS, slabs, tb), lambda i, k: (0, jnp.maximum(k - 1, 0), i))
    return pl.pallas_call(
        functools.partial(_peer_dense_body, ec=ec, tb=tb, nk=nk),
        grid=(T // tb, nk + 1),
        in_specs=[
            pl.BlockSpec((ec, D_MODEL), lambda i, k: (jnp.minimum(k, nk - 1), 0)),
            pl.BlockSpec((D_MODEL, ec), lambda i, k: (0, jnp.maximum(k - 1, 0))),
            pl.BlockSpec((D_MODEL, tb), lambda i, k: (0, i)),
            tab_spec, tab_spec, row_spec, row_spec,
            pl.BlockSpec((tb, D_MODEL), lambda i, k: (i, 0)),
            pl.BlockSpec((1, D_MODEL), lambda i, k: (0, 0)),
        ],
        out_specs=pl.BlockSpec((tb, D_MODEL), lambda i, k: (i, 0)),
        out_shape=jax.ShapeDtypeStruct((T, D_MODEL), F32),
        scratch_shapes=[pltpu.VMEM((D_MODEL, tb), F32),
                        pltpu.VMEM((ec, tb), F32), pltpu.VMEM((ec, tb), F32)],
        compiler_params=_cparams(("parallel", "arbitrary")),
        name="peer_dense",
    )(u, vt, h2t, e1, c1, n0, w0, x1, final_w)


def _group_matrix():
    g = np.zeros((PEER_TOPK, _N_CAND_PAD), np.float32)
    off = 0
    for r, ncol in enumerate(_CAND_COLS):
        g[r, off:off + ncol] = 1.0
        off += ncol
    return g


def _layer(x2, norm1_w, w_in, fox_f_bias, conv_w, conv_b, i_bias, f_bias, fox_nw, mlstm_nw, w_out, norm2_w,
           w_q, keys, u, v, final_w, *, batch, seq):
    splits = np.cumsum((FOX_WIDTH, FOX_WIDTH, FOX_WIDTH, FOX_HEADS, MLSTM_QK_WIDTH, MLSTM_QK_WIDTH,
                        MLSTM_V_WIDTH, MLSTM_HEADS, MLSTM_HEADS, MLSTM_V_WIDTH))[:-1]
    fq, fk, fv, ff, mq, mk, mv, mi, mf, mo = jnp.split(w_in, [int(p) for p in splits], axis=1)
    w_main = jnp.concatenate([fq, fk, fv, mv, mo, mq, mk], axis=1).astype(BF16)
    gate_pad = GATE_LANES - FOX_HEADS - 2 * MLSTM_HEADS
    w_gate = jnp.pad(jnp.concatenate([ff, mi, mf], axis=1), ((0, 0), (0, gate_pad))).astype(BF16)
    gate_bias = jnp.pad(jnp.concatenate([fox_f_bias, i_bias, f_bias]), (0, gate_pad)).reshape(1, GATE_LANES)
    tri = jnp.asarray(np.tril(np.ones((LANES, LANES), np.float32)), BF16)

    z, zc, g = _inproj(x2, norm1_w.reshape(1, D_MODEL), w_main, w_gate)
    pq, pk = _placement_matrices()
    qn, kn = _fox_norms(z, jnp.asarray(_head_group_matrix(), BF16))
    kmax = jnp.max(kn.reshape(batch, seq, GATE_LANES), axis=1)
    gc, gr, qx, kx = _gates(g, gate_bias, tri, jnp.asarray(pq, BF16), jnp.asarray(pk, BF16), qn,
                            jnp.repeat(kmax, SUBLANES, axis=0), batch=batch, seq=seq)
    lo, fast = _fox_plan(qn, kmax, gc, batch=batch, seq=seq, tk=FOX_KEY_BLOCK)
    att = _fox(z, qx, kx, fox_nw.reshape(FOX_HEADS, FOX_HEAD_DIM), lo, fast, batch=batch, seq=seq, tk=FOX_KEY_BLOCK)
    cell = _mlstm(zc, z, gc, gr, conv_w, conv_b.reshape(1, -1), mlstm_nw.reshape(1, -1), batch=batch, seq=seq)
    x1, h2t = _outproj(att, cell, x2, w_out[:FOX_WIDTH].astype(BF16), w_out[FOX_WIDTH:].astype(BF16),
                       norm2_w.reshape(1, D_MODEL))
    e1, c1, n0, w0 = _peer_sel(h2t, w_q.T.astype(BF16), keys.astype(BF16), jnp.asarray(_group_matrix(), BF16))
    return _peer_dense(u.astype(BF16), v.T.astype(BF16), h2t, e1, c1, n0, w0, x1, final_w)


def kernel(x, norm1_w, w_in, fox_f_bias, mlstm_conv_w, mlstm_conv_b, mlstm_i_bias, mlstm_f_bias, fox_out_norm_w,
           mlstm_out_norm_w, w_out, norm2_w, peer_w_q, peer_keys, peer_u, peer_v, final_norm_w):
    batch, seq, _ = x.shape
    assert w_in.shape[0] == 1, "single-layer block: the final norm is fused with the last residual add"
    x2 = x.reshape(batch * seq, D_MODEL)
    out = _layer(x2, norm1_w[0], w_in[0], fox_f_bias[0], mlstm_conv_w[0], mlstm_conv_b[0], mlstm_i_bias[0],
                 mlstm_f_bias[0], fox_out_norm_w[0], mlstm_out_norm_w[0], w_out[0], norm2_w[0],
                 peer_w_q[0], peer_keys[0], peer_u[0], peer_v[0], final_norm_w.reshape(1, D_MODEL),
                 batch=batch, seq=seq)
    return out.reshape(batch, seq, D_MODEL)
```

```python
import functools
import math

import numpy as np
import jax
import jax.numpy as jnp
from jax import lax
from jax.experimental import pallas as pl
from jax.experimental.pallas import tpu as pltpu

F32 = jnp.float32
BF16 = jnp.bfloat16

D_MODEL = 2048
FOX_HEADS = 8
FOX_HEAD_DIM = 128
FOX_WIDTH = FOX_HEADS * FOX_HEAD_DIM
MLSTM_HEADS = 4
MLSTM_QK_DIM = 128
MLSTM_V_DIM = 256
MLSTM_QK_WIDTH = MLSTM_HEADS * MLSTM_QK_DIM
MLSTM_V_WIDTH = MLSTM_HEADS * MLSTM_V_DIM
MLSTM_CONV = 4
MLSTM_CHUNK = 128
PEER_HEADS = 8
PEER_KEY_DIM = 256
PEER_N_KEYS = 128
PEER_TOPK = 16
PEER_N_EXPERTS = PEER_N_KEYS * PEER_N_KEYS
NORM_EPS = 1e-6
LOG2E = math.log2(math.e)
NORM_MARGIN = 1.0 + 2.0 ** -7
F32_ZERO_LOG2 = 150.0
FAST_GAP_LOG2 = 90.0
BOUND_SLACK_LOG2 = 4.0
FOX_KEY_BLOCK = 512

LANES = 128
SUBLANES = 8
GATE_LANES = LANES
GATE_ROWS = 16
VMEM_LIMIT = 56 * 1024 * 1024

_CAND_COLS = tuple(PEER_TOPK // (r + 1) for r in range(PEER_TOPK))
_N_CAND = sum(_CAND_COLS)
_N_CAND_PAD = -(-_N_CAND // SUBLANES) * SUBLANES


def _cparams(sem):
    return pltpu.CompilerParams(dimension_semantics=sem, vmem_limit_bytes=VMEM_LIMIT)


def _inproj_body(x_ref, nw_ref, w_ref, wg_ref, z_ref, zc_ref, g_ref, h_scr, *, n_main, q_blocks, q_scale):
    j = pl.program_id(1)

    @pl.when(j == 0)
    def _():
        x = x_ref[...]
        ms = jnp.mean(x * x, axis=-1, keepdims=True)
        hb = ((x * lax.rsqrt(ms + NORM_EPS)) * nw_ref[...]).astype(BF16)
        h_scr[...] = hb
        g_ref[...] = jnp.dot(hb, wg_ref[...], preferred_element_type=F32)

    z = jnp.dot(h_scr[...], w_ref[...], preferred_element_type=F32)

    @pl.when(j < n_main)
    def _():
        scale = jnp.where(j < q_blocks, q_scale, 1.0).astype(F32)
        z_ref[...] = (z * scale).astype(BF16)

    @pl.when(j >= n_main)
    def _():
        zc_ref[...] = z


def _inproj(x2, norm_w, w_main, w_gate, *, tm=1024, tn=1024):
    T = x2.shape[0]
    n_cols = w_main.shape[1]
    n_conv = 2 * MLSTM_QK_WIDTH
    n_main = (n_cols - n_conv) // tn
    n_blocks = n_cols // tn
    body = functools.partial(_inproj_body, n_main=n_main, q_blocks=FOX_WIDTH // tn,
                             q_scale=FOX_HEAD_DIM ** -0.5 * LOG2E)
    return pl.pallas_call(
        body,
        grid=(T // tm, n_blocks),
        in_specs=[
            pl.BlockSpec((tm, D_MODEL), lambda i, j: (i, 0)),
            pl.BlockSpec((1, D_MODEL), lambda i, j: (0, 0)),
            pl.BlockSpec((D_MODEL, tn), lambda i, j: (0, j)),
            pl.BlockSpec((D_MODEL, GATE_LANES), lambda i, j: (0, 0)),
        ],
        out_specs=[
            pl.BlockSpec((tm, tn), lambda i, j: (i, jnp.minimum(j, n_main - 1))),
            pl.BlockSpec((tm, tn), lambda i, j: (i, jnp.maximum(j - n_main, 0))),
            pl.BlockSpec((tm, GATE_LANES), lambda i, j: (i, 0)),
        ],
        out_shape=[
            jax.ShapeDtypeStruct((T, n_cols - n_conv), BF16),
            jax.ShapeDtypeStruct((T, n_conv), F32),
            jax.ShapeDtypeStruct((T, GATE_LANES), F32),
        ],
        scratch_shapes=[pltpu.VMEM((tm, D_MODEL), BF16)],
        compiler_params=_cparams(("parallel", "arbitrary")),
        name="inproj",
    )(x2, norm_w, w_main, w_gate)


def _split3(v):
    hi = v.astype(BF16)
    r1 = v - hi.astype(F32)
    mid = r1.astype(BF16)
    lo = (r1 - mid.astype(F32)).astype(BF16)
    return hi, mid, lo


def _fox_norms_body(q_ref, k_ref, grp_ref, qn_ref, kn_ref):
    for src, dst in ((q_ref, qn_ref), (k_ref, kn_ref)):
        x = src[...].astype(F32)
        ss = jnp.dot((x * x).astype(BF16), grp_ref[...], preferred_element_type=F32)
        dst[...] = jnp.sqrt(ss) * NORM_MARGIN


def _head_group_matrix():
    g = np.zeros((FOX_WIDTH, GATE_LANES), np.float32)
    for h in range(FOX_HEADS):
        g[h * FOX_HEAD_DIM:(h + 1) * FOX_HEAD_DIM, h] = 1.0
    return g


def _fox_norms(z, grp, *, rows=1024):
    T = z.shape[0]
    out = jax.ShapeDtypeStruct((T, GATE_LANES), F32)
    return pl.pallas_call(
        _fox_norms_body,
        grid=(T // rows,),
        in_specs=[
            pl.BlockSpec((rows, FOX_WIDTH), lambda i: (i, 0)),
            pl.BlockSpec((rows, FOX_WIDTH), lambda i: (i, 1)),
            pl.BlockSpec((FOX_WIDTH, GATE_LANES), lambda i: (0, 0)),
        ],
        out_specs=[pl.BlockSpec((rows, GATE_LANES), lambda i: (i, 0))] * 2,
        out_shape=[out, out],
        compiler_params=_cparams(("parallel",)),
        name="fox_norms",
    )(z, z, grp)


def _gates_body(g_ref, bias_ref, tri_ref, pq_ref, pk_ref, qn_ref, kmax_ref, gc_ref, gr_ref, qx_ref, kx_ref,
                carry_scr, *, rows):
    c = pl.program_id(1)

    @pl.when(c == 0)
    def _():
        carry_scr[...] = jnp.zeros_like(carry_scr)

    lane = lax.broadcasted_iota(jnp.int32, (LANES, GATE_LANES), 1)
    is_glob = lane < FOX_HEADS
    is_ls = is_glob | ((lane >= FOX_HEADS + MLSTM_HEADS) & (lane < FOX_HEADS + 2 * MLSTM_HEADS))
    tri = tri_ref[...]
    for s in range(rows // LANES):
        sl = slice(s * LANES, (s + 1) * LANES)
        g = g_ref[sl, :] + bias_ref[...]
        ls = jnp.minimum(g, 0.0) - jnp.log1p(jnp.exp(-jnp.abs(g)))
        v = jnp.where(is_ls, ls, 0.0)
        hi, mid, lo = _split3(v)
        cs = (jnp.dot(tri, hi, preferred_element_type=F32)
              + jnp.dot(tri, mid, preferred_element_type=F32)
              + jnp.dot(tri, lo, preferred_element_type=F32))
        glob = cs + carry_scr[...]
        carry_scr[...] = glob[LANES - 1:LANES, :]
        out = jnp.where(is_glob, glob, jnp.where(is_ls, cs, g))
        gc_ref[sl, :] = out
        gr_ref[:, sl] = out.T[0:GATE_ROWS, :]
        groups = (*_split3(glob * LOG2E), jnp.ones((LANES, GATE_LANES), BF16),
                  *_split3(qn_ref[sl, :] * kmax_ref[0:1, :] + 1.0))
        pieces = jnp.zeros((LANES, GATE_LANES), F32)
        for gi, grp in enumerate(groups):
            moved = pltpu.roll(grp.astype(F32), gi * FOX_HEADS, axis=1) if gi else grp.astype(F32)
            pieces = jnp.where((lane >= gi * FOX_HEADS) & (lane < (gi + 1) * FOX_HEADS), moved, pieces)
        pieces = pieces.astype(BF16)
        qx_ref[sl, :] = jnp.dot(pieces, pq_ref[...], preferred_element_type=F32).astype(BF16)
        kx_ref[sl, :] = jnp.dot(pieces, pk_ref[...], preferred_element_type=F32).astype(BF16)


_N_PIECES = 3
_PIECE_GROUPS = 2 * _N_PIECES + 1
_STAB_SLOT = 2 * _N_PIECES


def _placement_matrices():
    n = _N_PIECES
    assert _PIECE_GROUPS * FOX_HEADS <= GATE_LANES
    pq = np.zeros((GATE_LANES, FOX_WIDTH), np.float32)
    pk = np.zeros((GATE_LANES, FOX_WIDTH), np.float32)
    for h in range(FOX_HEADS):
        for p in range(n):
            pq[p * FOX_HEADS + h, h * FOX_HEAD_DIM + p] = 1.0
            pk[n * FOX_HEADS + h, h * FOX_HEAD_DIM + p] = 1.0
            pq[n * FOX_HEADS + h, h * FOX_HEAD_DIM + n + p] = 1.0
            pk[p * FOX_HEADS + h, h * FOX_HEAD_DIM + n + p] = -1.0
            pq[(n + 1 + p) * FOX_HEADS + h, h * FOX_HEAD_DIM + _STAB_SLOT + p] = -1.0
            pk[n * FOX_HEADS + h, h * FOX_HEAD_DIM + _STAB_SLOT + p] = 1.0
    return pq, pk


def _gates(g, bias, tri, pq, pk, qn, kmax, *, batch, seq, rows=1024):
    T = g.shape[0]
    nblk = seq // rows
    return pl.pallas_call(
        functools.partial(_gates_body, rows=rows),
        grid=(batch, nblk),
        in_specs=[
            pl.BlockSpec((rows, GATE_LANES), lambda b, c: (b * nblk + c, 0)),
            pl.BlockSpec((1, GATE_LANES), lambda b, c: (0, 0)),
            pl.BlockSpec((LANES, LANES), lambda b, c: (0, 0)),
            pl.BlockSpec((GATE_LANES, FOX_WIDTH), lambda b, c: (0, 0)),
            pl.BlockSpec((GATE_LANES, FOX_WIDTH), lambda b, c: (0, 0)),
            pl.BlockSpec((rows, GATE_LANES), lambda b, c: (b * nblk + c, 0)),
            pl.BlockSpec((SUBLANES, GATE_LANES), lambda b, c: (b, 0)),
        ],
        out_specs=[
            pl.BlockSpec((rows, GATE_LANES), lambda b, c: (b * nblk + c, 0)),
            pl.BlockSpec((GATE_ROWS, rows), lambda b, c: (0, b * nblk + c)),
            pl.BlockSpec((rows, FOX_WIDTH), lambda b, c: (b * nblk + c, 0)),
            pl.BlockSpec((rows, FOX_WIDTH), lambda b, c: (b * nblk + c, 0)),
        ],
        out_shape=[
            jax.ShapeDtypeStruct((T, GATE_LANES), F32),
            jax.ShapeDtypeStruct((GATE_ROWS, T), F32),
            jax.ShapeDtypeStruct((T, FOX_WIDTH), BF16),
            jax.ShapeDtypeStruct((T, FOX_WIDTH), BF16),
        ],
        scratch_shapes=[pltpu.VMEM((1, GATE_LANES), F32)],
        compiler_params=_cparams(("parallel", "arbitrary")),
        name="gates",
    )(g, bias, tri, pq, pk, qn, kmax)


def _fox_body(lo_ref, fast_ref, q_ref, qx_ref, k_ref, kx_ref, v_ref, nw_ref, o_ref, *, tk, nq):
    b = pl.program_id(0)
    h = pl.program_id(1)
    qi = pl.program_id(2)
    tile = (b * FOX_HEADS + h) * nq + qi
    lo = lo_ref[tile]
    row = lax.broadcasted_iota(jnp.int32, (tk, tk), 0)
    col = lax.broadcasted_iota(jnp.int32, (tk, tk), 1)

    def load_kv(ki):
        start = pl.multiple_of(ki * tk, tk)
        k = jnp.concatenate([k_ref[pl.ds(start, tk), :], kx_ref[pl.ds(start, tk), :]], axis=1)
        return k, v_ref[pl.ds(start, tk), :]

    def logits(q, k, masked):
        s = lax.dot_general(q, k, (((1,), (1,)), ((), ())), preferred_element_type=F32)
        return jnp.where(row >= col, s, -jnp.inf) if masked else s

    def finish(r, num, den):
        out = num / den
        ms = jnp.mean(out * out, axis=-1, keepdims=True)
        o_ref[r * tk:(r + 1) * tk, :] = ((out * lax.rsqrt(ms + NORM_EPS)) * nw_ref[pl.ds(h, 1), :]).astype(BF16)

    @pl.when(fast_ref[tile] == 1)
    def _():
        q_halves = [jnp.concatenate([q_ref[r * tk:(r + 1) * tk, :], qx_ref[r * tk:(r + 1) * tk, :]], axis=1)
                    for r in range(2)]
        ones_col = (lax.broadcasted_iota(jnp.int32, (tk, LANES), 1) == 0).astype(BF16)

        def weights(r, k, masked):
            return jnp.exp2(logits(q_halves[r], k, masked)).astype(BF16)

        def values(ki):
            return jnp.concatenate([load_kv(ki)[1], ones_col], axis=1)

        def add(acc, p, va):
            return acc + jnp.dot(p, va, preferred_element_type=F32)

        def pair(kp, accs):
            blocks = [2 * kp, 2 * kp + 1]
            ps = [[weights(r, load_kv(ki)[0], False) for r in range(2)] for ki in blocks]
            for j, ki in enumerate(blocks):
                va = values(ki)
                accs = tuple(add(accs[r], ps[j][r], va) for r in range(2))
            return accs

        zero = jnp.zeros((tk, 2 * FOX_HEAD_DIM), F32)
        aa, ab = lax.fori_loop(lo // 2, qi, pair, (zero, zero))
        k = load_kv(2 * qi)[0]
        pa, pb = weights(0, k, True), weights(1, k, False)
        pb2 = weights(1, load_kv(2 * qi + 1)[0], True)
        va = values(2 * qi)
        aa, ab = add(aa, pa, va), add(ab, pb, va)
        ab = add(ab, pb2, values(2 * qi + 1))
        for r, acc in enumerate((aa, ab)):
            finish(r, acc[:, 0:FOX_HEAD_DIM], acc[:, FOX_HEAD_DIM:FOX_HEAD_DIM + 1])

    @pl.when(fast_ref[tile] == 0)
    def _():
        _fox_online(q_ref, qx_ref, load_kv, logits, finish, lo, qi, tk)


def _fox_online(q_ref, qx_ref, load_kv, logits, finish, lo, qi, tk):
    lane = lax.broadcasted_iota(jnp.int32, (tk, LANES), 1)
    no_stab = (lane < _STAB_SLOT) | (lane >= _STAB_SLOT + _N_PIECES)
    q_halves = [jnp.concatenate([q_ref[r * tk:(r + 1) * tk, :],
                                 jnp.where(no_stab, qx_ref[r * tk:(r + 1) * tk, :], jnp.zeros((), BF16))], axis=1)
                for r in range(2)]

    def weights(s, carry):
        m, l, acc = carry
        cols = [s[:, c * LANES:(c + 1) * LANES] for c in range(tk // LANES)]
        m_new = jnp.maximum(m, jnp.max(functools.reduce(jnp.maximum, cols), axis=-1, keepdims=True))
        alpha = jnp.exp2(m - m_new)
        ps = [jnp.exp2(c - m_new) for c in cols]
        l = alpha * l + functools.reduce(jnp.add, ps)
        return m_new, l, alpha * acc, jnp.concatenate(ps, axis=1).astype(BF16)

    def update(q, k, v, carry, masked):
        m, l, acc, p = weights(logits(q, k, masked), carry)
        return m, l, acc + jnp.dot(p, v, preferred_element_type=F32)

    def both(ki, carries):
        k, v = load_kv(ki)
        s = [logits(q_halves[r], k, False) for r in range(2)]
        w = [weights(s[r], carries[r]) for r in range(2)]
        return tuple((m, l, acc + jnp.dot(p, v, preferred_element_type=F32)) for m, l, acc, p in w)

    init = (jnp.full((tk, LANES), -jnp.inf, F32), jnp.zeros((tk, LANES), F32), jnp.zeros((tk, FOX_HEAD_DIM), F32))
    ca, cb = lax.fori_loop(lo, 2 * qi, both, (init, init))
    k, v = load_kv(2 * qi)
    ca = update(q_halves[0], k, v, ca, True)
    cb = update(q_halves[1], k, v, cb, False)
    k, v = load_kv(2 * qi + 1)
    cb = update(q_halves[1], k, v, cb, True)
    for r, (_, l, acc) in enumerate((ca, cb)):
        finish(r, acc, jnp.sum(l, axis=-1, keepdims=True))


def _fox_plan(qn, kmax, gc, *, batch, seq, tk):
    tq = 2 * tk
    nq = seq // tq
    nk = seq // tk
    heads = slice(0, FOX_HEADS)
    qmax = jnp.max(qn.reshape(batch, nq, tq, GATE_LANES), axis=2)[..., heads]
    gap = 2.0 * qmax * kmax[:, None, heads] + BOUND_SLACK_LOG2
    f2 = (gc[:, heads] * LOG2E).reshape(batch, nk, tk, FOX_HEADS)
    f_first = f2[:, ::2, 0, :]
    f_last = f2[:, :, tk - 1, :]
    bound = gap[:, :, None, :] + f_first[:, :, None, :] - f_last[:, None, :, :]
    below_diag = jnp.arange(nk)[None, :, None] < 2 * jnp.arange(nq)[:, None, None]
    skip = (bound < -F32_ZERO_LOG2) & below_diag[None]
    lo = jnp.sum(jnp.cumprod(skip.astype(jnp.int32), axis=2), axis=2)
    fast = (gap <= FAST_GAP_LOG2).astype(jnp.int32)
    flat = lambda a: jnp.transpose(a, (0, 2, 1)).reshape(-1)
    return flat(lo), flat(fast)


def _fox(z, qx, kx, nw, lo, fast, *, batch, seq, tk=512):
    T = z.shape[0]
    tq = 2 * tk
    nq = seq // tq
    kcol = FOX_WIDTH // FOX_HEAD_DIM
    grid_spec = pltpu.PrefetchScalarGridSpec(
        num_scalar_prefetch=2,
        grid=(batch, FOX_HEADS, nq),
        in_specs=[
            pl.BlockSpec((tq, FOX_HEAD_DIM), lambda b, h, i, lo, fast: (b * nq + i, h)),
            pl.BlockSpec((tq, FOX_HEAD_DIM), lambda b, h, i, lo, fast: (b * nq + i, h)),
            pl.BlockSpec((seq, FOX_HEAD_DIM), lambda b, h, i, lo, fast: (b, kcol + h)),
            pl.BlockSpec((seq, FOX_HEAD_DIM), lambda b, h, i, lo, fast: (b, h)),
            pl.BlockSpec((seq, FOX_HEAD_DIM), lambda b, h, i, lo, fast: (b, 2 * kcol + h)),
            pl.BlockSpec((FOX_HEADS, FOX_HEAD_DIM), lambda b, h, i, lo, fast: (0, 0)),
        ],
        out_specs=pl.BlockSpec((tq, FOX_HEAD_DIM), lambda b, h, i, lo, fast: (b * nq + i, h)),
    )
    return pl.pallas_call(
        functools.partial(_fox_body, tk=tk, nq=nq),
        grid_spec=grid_spec,
        out_shape=jax.ShapeDtypeStruct((T, FOX_WIDTH), BF16),
        compiler_params=_cparams(("parallel", "parallel", "arbitrary")),
        name="fox",
    )(lo, fast, z, qx, z, kx, z, nw)


def _mlstm_body(zc_ref, zprev_ref, v_ref, o_ref, gc_ref, gr_ref, cw_ref, cb_ref, nw_ref, out_ref,
                full_scr, c_scr, n_scr, m_scr):
    c = pl.program_id(1)
    L = MLSTM_CHUNK
    dk = MLSTM_QK_DIM
    dv = MLSTM_V_DIM
    rows = L

    @pl.when(c == 0)
    def _():
        c_scr[...] = jnp.zeros_like(c_scr)
        n_scr[...] = jnp.zeros_like(n_scr)
        m_scr[...] = jnp.zeros_like(m_scr)

    full_scr[0:SUBLANES, :] = jnp.where(c == 0, 0.0, zprev_ref[...])
    full_scr[SUBLANES:SUBLANES + rows, :] = zc_ref[...]
    y = cb_ref[...]
    for j in range(MLSTM_CONV):
        y = y + cw_ref[j:j + 1, :] * full_scr[pl.ds(SUBLANES - (MLSTM_CONV - 1) + j, rows), :]
    qk_all = y * jax.nn.sigmoid(y)

    row = lax.broadcasted_iota(jnp.int32, (L, L), 0)
    col = lax.broadcasted_iota(jnp.int32, (L, L), 1)
    causal = row >= col
    heads = range(MLSTM_HEADS)
    gcb = gc_ref[...]
    grb = gr_ref[...]
    qh = [qk_all[:, hh * dk:(hh + 1) * dk] for hh in heads]
    kh = [qk_all[:, MLSTM_QK_WIDTH + hh * dk:MLSTM_QK_WIDTH + (hh + 1) * dk] * (dk ** -0.5) for hh in heads]
    vh = [v_ref[:, hh * dv:(hh + 1) * dv] for hh in heads]
    i_col = [gcb[:, FOX_HEADS + hh:FOX_HEADS + hh + 1] for hh in heads]
    b_col = [gcb[:, FOX_HEADS + MLSTM_HEADS + hh:FOX_HEADS + MLSTM_HEADS + hh + 1] for hh in heads]
    i_row = [grb[FOX_HEADS + hh:FOX_HEADS + hh + 1, :] for hh in heads]
    b_row = [grb[FOX_HEADS + MLSTM_HEADS + hh:FOX_HEADS + MLSTM_HEADS + hh + 1, :] for hh in heads]
    b_last = [b_row[hh][:, L - 1:L] for hh in heads]
    c_prev = [c_scr[hh] for hh in heads]
    n_prev = [n_scr[hh] for hh in heads]
    m_prev = [m_scr[hh][:, 0:1] for hh in heads]

    m_loc = [jnp.max(b_last[hh] - b_row[hh] + i_row[hh], axis=-1, keepdims=True) for hh in heads]
    kw = [kh[hh] * jnp.exp(b_last[hh] - b_col[hh] + i_col[hh] - m_loc[hh]) for hh in heads]
    g_col = [b_col[hh] + m_prev[hh] for hh in heads]
    dmat = [jnp.where(causal, b_col[hh] - b_row[hh] + i_row[hh], -jnp.inf) for hh in heads]
    m_t = [jnp.maximum(g_col[hh], jnp.max(dmat[hh], axis=-1, keepdims=True)) for hh in heads]
    decay = [jnp.exp(dmat[hh] - m_t[hh]) for hh in heads]
    inter = [jnp.exp(g_col[hh] - m_t[hh]) for hh in heads]
    qb = [qh[hh].astype(BF16) for hh in heads]
    qk = [lax.dot_general(qb[hh], kh[hh].astype(BF16), (((1,), (1,)), ((), ())), preferred_element_type=F32)
          for hh in heads]
    carried = [jnp.dot(qb[hh], c_prev[hh].astype(BF16), preferred_element_type=F32) for hh in heads]
    c_loc = [lax.dot_general(kw[hh].astype(BF16), vh[hh], (((0,), (0,)), ((), ())), preferred_element_type=F32)
             for hh in heads]
    sm = [qk[hh] * decay[hh] for hh in heads]
    intra = [jnp.dot(sm[hh].astype(BF16), vh[hh], preferred_element_type=F32) for hh in heads]
    den = [jnp.sum(sm[hh], axis=-1, keepdims=True)
           + inter[hh] * jnp.sum(qh[hh] * n_prev[hh], axis=-1, keepdims=True) for hh in heads]
    for hh in heads:
        cell = (intra[hh] + inter[hh] * carried[hh]) / jnp.maximum(jnp.abs(den[hh]), jnp.exp(-m_t[hh]))
        gated = jax.nn.sigmoid(o_ref[:, hh * dv:(hh + 1) * dv].astype(F32)) * cell
        ms = jnp.mean(gated * gated, axis=-1, keepdims=True)
        out_ref[:, hh * dv:(hh + 1) * dv] = (
            (gated * lax.rsqrt(ms + NORM_EPS)) * nw_ref[:, hh * dv:(hh + 1) * dv]).astype(BF16)
    for hh in heads:
        m_new = jnp.maximum(b_last[hh] + m_prev[hh], m_loc[hh])
        a_prev = jnp.exp(b_last[hh] + m_prev[hh] - m_new)
        a_loc = jnp.exp(m_loc[hh] - m_new)
        c_scr[hh] = a_prev * c_prev[hh] + a_loc * c_loc[hh]
        n_scr[hh] = a_prev * n_prev[hh] + a_loc * jnp.sum(kw[hh], axis=0, keepdims=True)
        m_scr[hh] = jnp.broadcast_to(m_new, (1, LANES))


def _mlstm(zc, z, gc, gr, conv_w, conv_b, nw, *, batch, seq):
    T = zc.shape[0]
    L = MLSTM_CHUNK
    nc = seq // L
    per = L // SUBLANES
    vcol = 3 * FOX_WIDTH // MLSTM_V_WIDTH
    return pl.pallas_call(
        _mlstm_body,
        grid=(batch, nc),
        in_specs=[
            pl.BlockSpec((L, 2 * MLSTM_QK_WIDTH), lambda b, c: (b * nc + c, 0)),
            pl.BlockSpec((SUBLANES, 2 * MLSTM_QK_WIDTH), lambda b, c: (jnp.maximum((b * nc + c) * per - 1, 0), 0)),
            pl.BlockSpec((L, MLSTM_V_WIDTH), lambda b, c: (b * nc + c, vcol)),
            pl.BlockSpec((L, MLSTM_V_WIDTH), lambda b, c: (b * nc + c, vcol + 1)),
            pl.BlockSpec((L, GATE_LANES), lambda b, c: (b * nc + c, 0)),
            pl.BlockSpec((GATE_ROWS, L), lambda b, c: (0, b * nc + c)),
            pl.BlockSpec((MLSTM_CONV, 2 * MLSTM_QK_WIDTH), lambda b, c: (0, 0)),
            pl.BlockSpec((1, 2 * MLSTM_QK_WIDTH), lambda b, c: (0, 0)),
            pl.BlockSpec((1, MLSTM_V_WIDTH), lambda b, c: (0, 0)),
        ],
        out_specs=pl.BlockSpec((L, MLSTM_V_WIDTH), lambda b, c: (b * nc + c, 0)),
        out_shape=jax.ShapeDtypeStruct((T, MLSTM_V_WIDTH), BF16),
        scratch_shapes=[
            pltpu.VMEM((SUBLANES + L, 2 * MLSTM_QK_WIDTH), F32),
            pltpu.VMEM((MLSTM_HEADS, MLSTM_QK_DIM, MLSTM_V_DIM), F32),
            pltpu.VMEM((MLSTM_HEADS, 1, MLSTM_QK_DIM), F32),
            pltpu.VMEM((MLSTM_HEADS, 1, LANES), F32),
        ],
        compiler_params=_cparams(("parallel", "arbitrary")),
        name="mlstm",
    )(zc, zc, z, z, gc, gr, conv_w, conv_b, nw)


def _outproj_body(att_ref, cell_ref, x_ref, wa_ref, wb_ref, n2_ref, x1_ref, h2t_ref):
    y = (jnp.dot(att_ref[...], wa_ref[...], preferred_element_type=F32)
         + jnp.dot(cell_ref[...], wb_ref[...], preferred_element_type=F32))
    x1 = x_ref[...] + y
    x1_ref[...] = x1
    ms = jnp.mean(x1 * x1, axis=-1, keepdims=True)
    h2 = (x1 * lax.rsqrt(ms + NORM_EPS)) * n2_ref[...]
    h2t_ref[...] = h2.T.astype(BF16)


def _outproj(att, cell, x2, wa, wb, n2, *, tm=512):
    T = x2.shape[0]
    return pl.pallas_call(
        _outproj_body,
        grid=(T // tm,),
        in_specs=[
            pl.BlockSpec((tm, FOX_WIDTH), lambda i: (i, 0)),
            pl.BlockSpec((tm, MLSTM_V_WIDTH), lambda i: (i, 0)),
            pl.BlockSpec((tm, D_MODEL), lambda i: (i, 0)),
            pl.BlockSpec((FOX_WIDTH, D_MODEL), lambda i: (0, 0)),
            pl.BlockSpec((MLSTM_V_WIDTH, D_MODEL), lambda i: (0, 0)),
            pl.BlockSpec((1, D_MODEL), lambda i: (0, 0)),
        ],
        out_specs=[
            pl.BlockSpec((tm, D_MODEL), lambda i: (i, 0)),
            pl.BlockSpec((D_MODEL, tm), lambda i: (0, i)),
        ],
        out_shape=[
            jax.ShapeDtypeStruct((T, D_MODEL), F32),
            jax.ShapeDtypeStruct((D_MODEL, T), BF16),
        ],
        compiler_params=_cparams(("parallel",)),
        name="outproj",
    )(att, cell, x2, wa, wb, n2)


def _row_iota(n_rows, tb):
    return lax.broadcasted_iota(jnp.int32, (n_rows, tb), 0).astype(F32)


def _extract16(s, val_ref, idx_ref):
    n_rows, tb = s.shape
    iota = _row_iota(n_rows, tb)

    def body(r, sw):
        mx = jnp.max(sw, axis=0, keepdims=True)
        first = jnp.min(jnp.where(sw == mx, iota, float(n_rows)), axis=0, keepdims=True)
        val_ref[pl.ds(r, 1), :] = mx
        idx_ref[pl.ds(r, 1), :] = first
        return jnp.where(iota == first, -jnp.inf, sw)

    lax.fori_loop(0, PEER_TOPK, body, s)


def _scatter_rounds(idx, values, n_rows, fill):
    tb = idx.shape[1]
    iota = _row_iota(n_rows, tb)
    table = jnp.full((n_rows, tb), fill, F32)
    for r in range(PEER_TOPK):
        v = values[r:r + 1, :] if values is not None else float(r)
        table = jnp.where(iota == idx[r:r + 1, :], v, table)
    return table


def _extract16_distinct(arrays, val_refs):
    def body(r, carry):
        out = []
        for (sw, rank), val_ref in zip(carry, val_refs):
            mx = jnp.max(sw, axis=0, keepdims=True)
            val_ref[pl.ds(r, 1), :] = mx
            hit = sw == mx
            out.append((jnp.where(hit, -jnp.inf, sw), jnp.where(hit, lax.convert_element_type(r, F32), rank)))
        return tuple(out)

    init = tuple((s, jnp.full(s.shape, float(PEER_TOPK), F32)) for s in arrays)
    return [rank for _, rank in lax.fori_loop(0, PEER_TOPK, body, init, unroll=True)]


def _count_ge(s, thr):
    return jnp.sum((s >= thr).astype(F32), axis=0, keepdims=True)


def _match_rounds(s, vals, values, fill):
    table = jnp.full(s.shape, fill, F32)
    for r in range(PEER_TOPK):
        v = values[r:r + 1, :] if values is not None else float(r)
        table = jnp.where(s == vals[r:r + 1, :], v, table)
    return table


def _peer_sel_body(h2t_ref, h2t_next_ref, wqt_ref, keys_ref, grp_ref, e1_ref, c1_ref, n0_ref, w0_ref,
                   qt_scr, sc_scr, vals_scr, idx_scr, cand_scr, pe_scr, *, tb):
    half = PEER_KEY_DIM // 2

    def queries(src_ref):
        qt_scr[...] = jnp.dot(wqt_ref[...], src_ref[...], preferred_element_type=F32).astype(BF16)

    @pl.when(pl.program_id(0) == 0)
    def _():
        queries(h2t_ref)

    for hp in range(2 * PEER_HEADS):
        sc_scr[hp] = jnp.dot(keys_ref[hp % 2], qt_scr[hp * half:(hp + 1) * half, :], preferred_element_type=F32)
    queries(h2t_next_ref)

    def head_tables(h, exact):
        s0 = sc_scr[2 * h]
        s1 = sc_scr[2 * h + 1]
        if exact:
            _extract16(s0, vals_scr.at[0], idx_scr.at[0])
            _extract16(s1, vals_scr.at[1], idx_scr.at[1])
        else:
            _extract16_distinct([s0], [vals_scr.at[0]])
            rank1, = _extract16_distinct([s1], [vals_scr.at[1]])
        a = vals_scr[0]
        b = vals_scr[1]
        ea = jnp.exp(a - a[0:1, :])
        eb = jnp.exp(b - b[0:1, :])
        off = 0
        for r in range(PEER_TOPK):
            ncol = _CAND_COLS[r]
            cand_scr[off:off + ncol, :] = a[r:r + 1, :] + b[0:ncol, :]
            pe_scr[off:off + ncol, :] = ea[r:r + 1, :] * eb[0:ncol, :]
            off += ncol
        cand_scr[_N_CAND:_N_CAND_PAD, :] = jnp.full((_N_CAND_PAD - _N_CAND, tb), -jnp.inf, F32)
        pe_scr[_N_CAND:_N_CAND_PAD, :] = jnp.zeros((_N_CAND_PAD - _N_CAND, tb), F32)
        cand = cand_scr[...]
        last = PEER_TOPK - 1
        if exact:
            _extract16(cand, vals_scr.at[2], idx_scr.at[2])
            sel = _scatter_rounds(idx_scr[2], jnp.ones((PEER_TOPK, tb), F32), _N_CAND_PAD, 0.0)
            ties = None
        else:
            _extract16_distinct([cand], [vals_scr.at[2]])
            sel = (cand >= vals_scr[2, last:last + 1, :]).astype(F32)
            ties = ((_count_ge(s0, a[last:last + 1, :]) != float(PEER_TOPK)).astype(F32)
                    + (_count_ge(s1, b[last:last + 1, :]) != float(PEER_TOPK)).astype(F32)
                    + (jnp.sum(sel, axis=0, keepdims=True) != float(PEER_TOPK)).astype(F32))
        zsum = jnp.sum(sel * pe_scr[...], axis=0, keepdims=True)
        nr = jnp.dot(grp_ref[...], sel.astype(BF16), preferred_element_type=F32)
        if exact:
            n0 = _scatter_rounds(idx_scr[0], nr, PEER_N_KEYS, 0.0)
            c1 = _scatter_rounds(idx_scr[1], None, PEER_N_KEYS, float(PEER_TOPK))
        else:
            n0 = _match_rounds(s0, a, nr, 0.0)
            c1 = rank1
        n0_ref[h] = n0
        w0_ref[h] = jnp.exp(s0 - a[0:1, :]) * (0.5 / zsum)
        e1_ref[h] = jnp.exp(s1 - b[0:1, :]).astype(BF16)
        c1_ref[h] = c1.astype(BF16)
        return ties

    for h in range(PEER_HEADS):
        ties = head_tables(h, False)

        @pl.when(jnp.max(ties) > 0.0)
        def _():
            head_tables(h, True)


def _peer_sel(h2t, wqt, keys, grp, *, tb=256):
    T = h2t.shape[1]
    tab = jax.ShapeDtypeStruct((PEER_HEADS, PEER_N_KEYS, T), F32)
    tab16 = jax.ShapeDtypeStruct((PEER_HEADS, PEER_N_KEYS, T), BF16)
    tab_spec = pl.BlockSpec((PEER_HEADS, PEER_N_KEYS, tb), lambda i: (0, 0, i))
    n_blocks = T // tb
    return pl.pallas_call(
        functools.partial(_peer_sel_body, tb=tb),
        grid=(n_blocks,),
        in_specs=[
            pl.BlockSpec((D_MODEL, tb), lambda i: (0, 0)),
            pl.BlockSpec((D_MODEL, tb), lambda i: (0, jnp.minimum(i + 1, n_blocks - 1))),
            pl.BlockSpec((PEER_HEADS * PEER_KEY_DIM, D_MODEL), lambda i: (0, 0)),
            pl.BlockSpec((2, PEER_N_KEYS, PEER_KEY_DIM // 2), lambda i: (0, 0, 0)),
            pl.BlockSpec((PEER_TOPK, _N_CAND_PAD), lambda i: (0, 0)),
        ],
        out_specs=[tab_spec, tab_spec, tab_spec, tab_spec],
        out_shape=[tab16, tab16, tab, tab],
        scratch_shapes=[
            pltpu.VMEM((PEER_HEADS * PEER_KEY_DIM, tb), BF16),
            pltpu.VMEM((2 * PEER_HEADS, PEER_N_KEYS, tb), F32),
            pltpu.VMEM((3, PEER_TOPK, tb), F32),
            pltpu.VMEM((3, PEER_TOPK, tb), F32),
            pltpu.VMEM((_N_CAND_PAD, tb), F32),
            pltpu.VMEM((_N_CAND_PAD, tb), F32),
        ],
        compiler_params=_cparams(("arbitrary",)),
        name="peer_sel",
    )(h2t, h2t, wqt, keys, grp)


def _peer_dense_body(u_ref, vt_ref, h2t_ref, e1_ref, c1_ref, n0_ref, w0_ref, x1_ref, fw_ref, out_ref,
                     yt_ref, st_a, st_b, *, ec, tb, nk):
    k = pl.program_id(1)
    slabs = ec // PEER_N_KEYS

    def pre_activations(st_w):
        st_w[...] = jnp.dot(u_ref[...], h2t_ref[...], preferred_element_type=F32)

    def bf16_rows(ref, h, ii):
        row = jnp.broadcast_to(ref[h, ii:ii + 1, :], (2 * SUBLANES, tb)).astype(BF16)
        return jnp.broadcast_to(row[None], (PEER_N_KEYS // (2 * SUBLANES), 2 * SUBLANES, tb)).reshape(PEER_N_KEYS, tb)

    def activation_slab(st_r, ii):
        g = jnp.zeros((PEER_N_KEYS, tb), BF16)
        for h in range(PEER_HEADS):
            n_row = bf16_rows(n0_ref, h, ii)
            w_row = bf16_rows(w0_ref, h, ii)
            g = g + jnp.where(c1_ref[h] < n_row, e1_ref[h] * w_row, jnp.zeros((), BF16))
        x = st_r[ii * PEER_N_KEYS:(ii + 1) * PEER_N_KEYS, :]
        act = x * (1.0 + lax.erf(x * math.sqrt(0.5)))
        return act.astype(BF16) * g

    def activate_and_project(st_r):
        at = jnp.concatenate([activation_slab(st_r, ii) for ii in range(slabs)], axis=0)
        yt_ref[...] += jnp.dot(vt_ref[...], at, preferred_element_type=F32)

    @pl.when(k == 0)
    def _():
        yt_ref[...] = jnp.zeros_like(yt_ref)
        pre_activations(st_a)

    @pl.when((k > 0) & (k < nk) & (k % 2 == 1))
    def _():
        pre_activations(st_b)
        activate_and_project(st_a)

    @pl.when((k > 0) & (k < nk) & (k % 2 == 0))
    def _():
        pre_activations(st_a)
        activate_and_project(st_b)

    @pl.when(k == nk)
    def _():
        activate_and_project(st_b if nk % 2 == 0 else st_a)
        x2 = x1_ref[...] + yt_ref[...].T
        ms = jnp.mean(x2 * x2, axis=-1, keepdims=True)
        out_ref[...] = (x2 * lax.rsqrt(ms + NORM_EPS)) * fw_ref[...]


def _peer_dense(u, vt, h2t, e1, c1, n0, w0, x1, final_w, *, tb=512, ec=1024):
    T = h2t.shape[1]
    nk = PEER_N_EXPERTS // ec
    slabs = ec // PEER_N_KEYS
    assert slabs == SUBLANES, "one f32 sublane tile of per-slab gate rows per expert chunk"
    tab_spec = pl.BlockSpec((PEER_HEADS, PEER_N_KEYS, tb), lambda i, k: (0, 0, i))
    row_spec = pl.BlockSpec((PEER_HEADS, slabs, tb), lambda i, k: (0, jnp.maximum(k - 1, 0), i))
    return pl.pallas_call(
        functools.partial(_peer_dense_body, ec=ec, tb=tb, nk=nk),
        grid=(T // tb, nk + 1),
        in_specs=[
            pl.BlockSpec((ec, D_MODEL), lambda i, k: (jnp.minimum(k, nk - 1), 0)),
            pl.BlockSpec((D_MODEL, ec), lambda i, k: (0, jnp.maximum(k - 1, 0))),
            pl.BlockSpec((D_MODEL, tb), lambda i, k: (0, i)),
            tab_spec, tab_spec, row_spec, row_spec,
            pl.BlockSpec((tb, D_MODEL), lambda i, k: (i, 0)),
            pl.BlockSpec((1, D_MODEL), lambda i, k: (0, 0)),
        ],
        out_specs=pl.BlockSpec((tb, D_MODEL), lambda i, k: (i, 0)),
        out_shape=jax.ShapeDtypeStruct((T, D_MODEL), F32),
        scratch_shapes=[pltpu.VMEM((D_MODEL, tb), F32),
                        pltpu.VMEM((ec, tb), F32), pltpu.VMEM((ec, tb), F32)],
        compiler_params=_cparams(("parallel", "arbitrary")),
        name="peer_dense",
    )(u, vt, h2t, e1, c1, n0, w0, x1, final_w)


def _group_matrix():
    g = np.zeros((PEER_TOPK, _N_CAND_PAD), np.float32)
    off = 0
    for r, ncol in enumerate(_CAND_COLS):
        g[r, off:off + ncol] = 1.0
        off += ncol
    return g


def _layer(x2, norm1_w, w_in, fox_f_bias, conv_w, conv_b, i_bias, f_bias, fox_nw, mlstm_nw, w_out, norm2_w,
           w_q, keys, u, v, final_w, *, batch, seq):
    splits = np.cumsum((FOX_WIDTH, FOX_WIDTH, FOX_WIDTH, FOX_HEADS, MLSTM_QK_WIDTH, MLSTM_QK_WIDTH,
                        MLSTM_V_WIDTH, MLSTM_HEADS, MLSTM_HEADS, MLSTM_V_WIDTH))[:-1]
    fq, fk, fv, ff, mq, mk, mv, mi, mf, mo = jnp.split(w_in, [int(p) for p in splits], axis=1)
    w_main = jnp.concatenate([fq, fk, fv, mv, mo, mq, mk], axis=1).astype(BF16)
    gate_pad = GATE_LANES - FOX_HEADS - 2 * MLSTM_HEADS
    w_gate = jnp.pad(jnp.concatenate([ff, mi, mf], axis=1), ((0, 0), (0, gate_pad))).astype(BF16)
    gate_bias = jnp.pad(jnp.concatenate([fox_f_bias, i_bias, f_bias]), (0, gate_pad)).reshape(1, GATE_LANES)
    tri = jnp.asarray(np.tril(np.ones((LANES, LANES), np.float32)), BF16)

    z, zc, g = _inproj(x2, norm1_w.reshape(1, D_MODEL), w_main, w_gate)
    pq, pk = _placement_matrices()
    qn, kn = _fox_norms(z, jnp.asarray(_head_group_matrix(), BF16))
    kmax = jnp.max(kn.reshape(batch, seq, GATE_LANES), axis=1)
    gc, gr, qx, kx = _gates(g, gate_bias, tri, jnp.asarray(pq, BF16), jnp.asarray(pk, BF16), qn,
                            jnp.repeat(kmax, SUBLANES, axis=0), batch=batch, seq=seq)
    lo, fast = _fox_plan(qn, kmax, gc, batch=batch, seq=seq, tk=FOX_KEY_BLOCK)
    att = _fox(z, qx, kx, fox_nw.reshape(FOX_HEADS, FOX_HEAD_DIM), lo, fast, batch=batch, seq=seq, tk=FOX_KEY_BLOCK)
    cell = _mlstm(zc, z, gc, gr, conv_w, conv_b.reshape(1, -1), mlstm_nw.reshape(1, -1), batch=batch, seq=seq)
    x1, h2t = _outproj(att, cell, x2, w_out[:FOX_WIDTH].astype(BF16), w_out[FOX_WIDTH:].astype(BF16),
                       norm2_w.reshape(1, D_MODEL))
    e1, c1, n0, w0 = _peer_sel(h2t, w_q.T.astype(BF16), keys.astype(BF16), jnp.asarray(_group_matrix(), BF16))
    return _peer_dense(u.astype(BF16), v.T.astype(BF16), h2t, e1, c1, n0, w0, x1, final_w)


def kernel(x, norm1_w, w_in, fox_f_bias, mlstm_conv_w, mlstm_conv_b, mlstm_i_bias, mlstm_f_bias, fox_out_norm_w,
           mlstm_out_norm_w, w_out, norm2_w, peer_w_q, peer_keys, peer_u, peer_v, final_norm_w):
    batch, seq, _ = x.shape
    assert w_in.shape[0] == 1, "single-layer block: the final norm is fused with the last residual add"
    x2 = x.reshape(batch * seq, D_MODEL)
    out = _layer(x2, norm1_w[0], w_in[0], fox_f_bias[0], mlstm_conv_w[0], mlstm_conv_b[0], mlstm_i_bias[0],
                 mlstm_f_bias[0], fox_out_norm_w[0], mlstm_out_norm_w[0], w_out[0], norm2_w[0],
                 peer_w_q[0], peer_keys[0], peer_u[0], peer_v[0], final_norm_w.reshape(1, D_MODEL),
                 batch=batch, seq=seq)
    return out.reshape(batch, seq, D_MODEL)
```

```python
import functools
import math

import numpy as np
import jax
import jax.numpy as jnp
from jax import lax
from jax.experimental import pallas as pl
from jax.experimental.pallas import tpu as pltpu

F32 = jnp.float32
BF16 = jnp.bfloat16

D_MODEL = 2048
FOX_HEADS = 8
FOX_HEAD_DIM = 128
FOX_WIDTH = FOX_HEADS * FOX_HEAD_DIM
MLSTM_HEADS = 4
MLSTM_QK_DIM = 128
MLSTM_V_DIM = 256
MLSTM_QK_WIDTH = MLSTM_HEADS * MLSTM_QK_DIM
MLSTM_V_WIDTH = MLSTM_HEADS * MLSTM_V_DIM
MLSTM_CONV = 4
MLSTM_CHUNK = 128
PEER_HEADS = 8
PEER_KEY_DIM = 256
PEER_N_KEYS = 128
PEER_TOPK = 16
PEER_N_EXPERTS = PEER_N_KEYS * PEER_N_KEYS
NORM_EPS = 1e-6
LOG2E = math.log2(math.e)
NORM_MARGIN = 1.0 + 2.0 ** -7
F32_ZERO_LOG2 = 150.0
FAST_GAP_LOG2 = 90.0
BOUND_SLACK_LOG2 = 4.0
FOX_KEY_BLOCK = 512

LANES = 128
SUBLANES = 8
GATE_LANES = LANES
GATE_ROWS = 16
VMEM_LIMIT = 56 * 1024 * 1024

_CAND_COLS = tuple(PEER_TOPK // (r + 1) for r in range(PEER_TOPK))
_N_CAND = sum(_CAND_COLS)
_N_CAND_PAD = -(-_N_CAND // SUBLANES) * SUBLANES


def _cparams(sem):
    return pltpu.CompilerParams(dimension_semantics=sem, vmem_limit_bytes=VMEM_LIMIT)


def _inproj_body(x_ref, nw_ref, w_ref, wg_ref, grp_ref, z_ref, zc_ref, g_ref, qn_ref, kn_ref, h_scr, *,
                 n_main, q_blocks, q_scale):
    j = pl.program_id(1)

    def head_norms(zb, dst):
        x = zb.astype(F32)
        ss = jnp.dot((x * x).astype(BF16), grp_ref[...], preferred_element_type=F32)
        dst[...] = jnp.sqrt(ss) * NORM_MARGIN

    @pl.when(j == 0)
    def _():
        x = x_ref[...]
        ms = jnp.mean(x * x, axis=-1, keepdims=True)
        hb = ((x * lax.rsqrt(ms + NORM_EPS)) * nw_ref[...]).astype(BF16)
        h_scr[...] = hb
        g_ref[...] = jnp.dot(hb, wg_ref[...], preferred_element_type=F32)

    z = jnp.dot(h_scr[...], w_ref[...], preferred_element_type=F32)

    @pl.when(j < n_main)
    def _():
        scale = jnp.where(j < q_blocks, q_scale, 1.0).astype(F32)
        zb = (z * scale).astype(BF16)
        z_ref[...] = zb

        @pl.when(j == 0)
        def _():
            head_norms(zb, qn_ref)

        @pl.when(j == 1)
        def _():
            head_norms(zb, kn_ref)

    @pl.when(j >= n_main)
    def _():
        zc_ref[...] = z


def _inproj(x2, norm_w, w_main, w_gate, grp, *, tm=1024, tn=FOX_WIDTH):
    T = x2.shape[0]
    n_cols = w_main.shape[1]
    n_conv = 2 * MLSTM_QK_WIDTH
    n_main = (n_cols - n_conv) // tn
    n_blocks = n_cols // tn
    assert tn == FOX_WIDTH, "column blocks 0 and 1 are exactly the attention queries and keys (norm bounds)"
    body = functools.partial(_inproj_body, n_main=n_main, q_blocks=FOX_WIDTH // tn,
                             q_scale=FOX_HEAD_DIM ** -0.5 * LOG2E)
    return pl.pallas_call(
        body,
        grid=(T // tm, n_blocks),
        in_specs=[
            pl.BlockSpec((tm, D_MODEL), lambda i, j: (i, 0)),
            pl.BlockSpec((1, D_MODEL), lambda i, j: (0, 0)),
            pl.BlockSpec((D_MODEL, tn), lambda i, j: (0, j)),
            pl.BlockSpec((D_MODEL, GATE_LANES), lambda i, j: (0, 0)),
            pl.BlockSpec((FOX_WIDTH, GATE_LANES), lambda i, j: (0, 0)),
        ],
        out_specs=[
            pl.BlockSpec((tm, tn), lambda i, j: (i, jnp.minimum(j, n_main - 1))),
            pl.BlockSpec((tm, tn), lambda i, j: (i, jnp.maximum(j - n_main, 0))),
            pl.BlockSpec((tm, GATE_LANES), lambda i, j: (i, 0)),
            pl.BlockSpec((tm, GATE_LANES), lambda i, j: (i, 0)),
            pl.BlockSpec((tm, GATE_LANES), lambda i, j: (i, 0)),
        ],
        out_shape=[
            jax.ShapeDtypeStruct((T, n_cols - n_conv), BF16),
            jax.ShapeDtypeStruct((T, n_conv), F32),
            jax.ShapeDtypeStruct((T, GATE_LANES), F32),
            jax.ShapeDtypeStruct((T, GATE_LANES), F32),
            jax.ShapeDtypeStruct((T, GATE_LANES), F32),
        ],
        scratch_shapes=[pltpu.VMEM((tm, D_MODEL), BF16)],
        compiler_params=_cparams(("parallel", "arbitrary")),
        name="inproj",
    )(x2, norm_w, w_main, w_gate, grp)


def _split3(v):
    hi = v.astype(BF16)
    r1 = v - hi.astype(F32)
    mid = r1.astype(BF16)
    lo = (r1 - mid.astype(F32)).astype(BF16)
    return hi, mid, lo


def _head_group_matrix():
    g = np.zeros((FOX_WIDTH, GATE_LANES), np.float32)
    for h in range(FOX_HEADS):
        g[h * FOX_HEAD_DIM:(h + 1) * FOX_HEAD_DIM, h] = 1.0
    return g


def _gates_body(g_ref, bias_ref, tri_ref, pq_ref, pk_ref, qn_ref, kmax_ref, gc_ref, gr_ref, qx_ref, kx_ref,
                carry_scr, *, rows):
    c = pl.program_id(1)

    @pl.when(c == 0)
    def _():
        carry_scr[...] = jnp.zeros_like(carry_scr)

    lane = lax.broadcasted_iota(jnp.int32, (LANES, GATE_LANES), 1)
    is_glob = lane < FOX_HEADS
    is_ls = is_glob | ((lane >= FOX_HEADS + MLSTM_HEADS) & (lane < FOX_HEADS + 2 * MLSTM_HEADS))
    tri = tri_ref[...]
    for s in range(rows // LANES):
        sl = slice(s * LANES, (s + 1) * LANES)
        g = g_ref[sl, :] + bias_ref[...]
        ls = jnp.minimum(g, 0.0) - jnp.log1p(jnp.exp(-jnp.abs(g)))
        v = jnp.where(is_ls, ls, 0.0)
        hi, mid, lo = _split3(v)
        cs = (jnp.dot(tri, hi, preferred_element_type=F32)
              + jnp.dot(tri, mid, preferred_element_type=F32)
              + jnp.dot(tri, lo, preferred_element_type=F32))
        glob = cs + carry_scr[...]
        carry_scr[...] = glob[LANES - 1:LANES, :]
        out = jnp.where(is_glob, glob, jnp.where(is_ls, cs, g))
        gc_ref[sl, :] = out
        gr_ref[:, sl] = out.T[0:GATE_ROWS, :]
        groups = (*_split3(glob * LOG2E), jnp.ones((LANES, GATE_LANES), BF16),
                  *_split3(qn_ref[sl, :] * kmax_ref[0:1, :] + 1.0))
        pieces = jnp.zeros((LANES, GATE_LANES), F32)
        for gi, grp in enumerate(groups):
            moved = pltpu.roll(grp.astype(F32), gi * FOX_HEADS, axis=1) if gi else grp.astype(F32)
            pieces = jnp.where((lane >= gi * FOX_HEADS) & (lane < (gi + 1) * FOX_HEADS), moved, pieces)
        pieces = pieces.astype(BF16)
        qx_ref[sl, :] = jnp.dot(pieces, pq_ref[...], preferred_element_type=F32).astype(BF16)
        kx_ref[sl, :] = jnp.dot(pieces, pk_ref[...], preferred_element_type=F32).astype(BF16)


_N_PIECES = 3
_PIECE_GROUPS = 2 * _N_PIECES + 1
_STAB_SLOT = 2 * _N_PIECES


def _placement_matrices():
    n = _N_PIECES
    assert _PIECE_GROUPS * FOX_HEADS <= GATE_LANES
    pq = np.zeros((GATE_LANES, FOX_WIDTH), np.float32)
    pk = np.zeros((GATE_LANES, FOX_WIDTH), np.float32)
    for h in range(FOX_HEADS):
        for p in range(n):
            pq[p * FOX_HEADS + h, h * FOX_HEAD_DIM + p] = 1.0
            pk[n * FOX_HEADS + h, h * FOX_HEAD_DIM + p] = 1.0
            pq[n * FOX_HEADS + h, h * FOX_HEAD_DIM + n + p] = 1.0
            pk[p * FOX_HEADS + h, h * FOX_HEAD_DIM + n + p] = -1.0
            pq[(n + 1 + p) * FOX_HEADS + h, h * FOX_HEAD_DIM + _STAB_SLOT + p] = -1.0
            pk[n * FOX_HEADS + h, h * FOX_HEAD_DIM + _STAB_SLOT + p] = 1.0
    return pq, pk


def _gates(g, bias, tri, pq, pk, qn, kmax, *, batch, seq, rows=1024):
    T = g.shape[0]
    nblk = seq // rows
    return pl.pallas_call(
        functools.partial(_gates_body, rows=rows),
        grid=(batch, nblk),
        in_specs=[
            pl.BlockSpec((rows, GATE_LANES), lambda b, c: (b * nblk + c, 0)),
            pl.BlockSpec((1, GATE_LANES), lambda b, c: (0, 0)),
            pl.BlockSpec((LANES, LANES), lambda b, c: (0, 0)),
            pl.BlockSpec((GATE_LANES, FOX_WIDTH), lambda b, c: (0, 0)),
            pl.BlockSpec((GATE_LANES, FOX_WIDTH), lambda b, c: (0, 0)),
            pl.BlockSpec((rows, GATE_LANES), lambda b, c: (b * nblk + c, 0)),
            pl.BlockSpec((SUBLANES, GATE_LANES), lambda b, c: (b, 0)),
        ],
        out_specs=[
            pl.BlockSpec((rows, GATE_LANES), lambda b, c: (b * nblk + c, 0)),
            pl.BlockSpec((GATE_ROWS, rows), lambda b, c: (0, b * nblk + c)),
            pl.BlockSpec((rows, FOX_WIDTH), lambda b, c: (b * nblk + c, 0)),
            pl.BlockSpec((rows, FOX_WIDTH), lambda b, c: (b * nblk + c, 0)),
        ],
        out_shape=[
            jax.ShapeDtypeStruct((T, GATE_LANES), F32),
            jax.ShapeDtypeStruct((GATE_ROWS, T), F32),
            jax.ShapeDtypeStruct((T, FOX_WIDTH), BF16),
            jax.ShapeDtypeStruct((T, FOX_WIDTH), BF16),
        ],
        scratch_shapes=[pltpu.VMEM((1, GATE_LANES), F32)],
        compiler_params=_cparams(("parallel", "arbitrary")),
        name="gates",
    )(g, bias, tri, pq, pk, qn, kmax)


def _fox_body(lo_ref, fast_ref, q_ref, qx_ref, k_ref, kx_ref, v_ref, nw_ref, o_ref, *, tk, nq):
    b = pl.program_id(0)
    h = pl.program_id(1)
    qi = pl.program_id(2)
    tile = (b * FOX_HEADS + h) * nq + qi
    lo = lo_ref[tile]
    row = lax.broadcasted_iota(jnp.int32, (tk, tk), 0)
    col = lax.broadcasted_iota(jnp.int32, (tk, tk), 1)

    def load_kv(ki):
        start = pl.multiple_of(ki * tk, tk)
        k = jnp.concatenate([k_ref[pl.ds(start, tk), :], kx_ref[pl.ds(start, tk), :]], axis=1)
        return k, v_ref[pl.ds(start, tk), :]

    def logits(q, k, masked):
        s = lax.dot_general(q, k, (((1,), (1,)), ((), ())), preferred_element_type=F32)
        return jnp.where(row >= col, s, -jnp.inf) if masked else s

    def finish(r, num, den):
        out = num / den
        ms = jnp.mean(out * out, axis=-1, keepdims=True)
        o_ref[r * tk:(r + 1) * tk, :] = ((out * lax.rsqrt(ms + NORM_EPS)) * nw_ref[pl.ds(h, 1), :]).astype(BF16)

    @pl.when(fast_ref[tile] == 1)
    def _():
        q_halves = [jnp.concatenate([q_ref[r * tk:(r + 1) * tk, :], qx_ref[r * tk:(r + 1) * tk, :]], axis=1)
                    for r in range(2)]
        ones_col = jnp.ones((tk, LANES), BF16)

        def weights(r, k, masked):
            return jnp.exp2(logits(q_halves[r], k, masked)).astype(BF16)

        def values(ki):
            return jnp.concatenate([load_kv(ki)[1], ones_col], axis=1)

        def add(acc, p, va):
            return acc + jnp.dot(p, va, preferred_element_type=F32)

        def pair(kp, accs):
            blocks = [2 * kp, 2 * kp + 1]
            ps = [[weights(r, load_kv(ki)[0], False) for r in range(2)] for ki in blocks]
            for j, ki in enumerate(blocks):
                va = values(ki)
                accs = tuple(add(accs[r], ps[j][r], va) for r in range(2))
            return accs

        zero = jnp.zeros((tk, 2 * FOX_HEAD_DIM), F32)
        aa, ab = lax.fori_loop(lo // 2, qi, pair, (zero, zero))
        k = load_kv(2 * qi)[0]
        pa, pb = weights(0, k, True), weights(1, k, False)
        pb2 = weights(1, load_kv(2 * qi + 1)[0], True)
        va = values(2 * qi)
        aa, ab = add(aa, pa, va), add(ab, pb, va)
        ab = add(ab, pb2, values(2 * qi + 1))
        for r, acc in enumerate((aa, ab)):
            finish(r, acc[:, 0:FOX_HEAD_DIM], acc[:, FOX_HEAD_DIM:])

    @pl.when(fast_ref[tile] == 0)
    def _():
        _fox_online(q_ref, qx_ref, load_kv, logits, finish, lo, qi, tk)


def _fox_online(q_ref, qx_ref, load_kv, logits, finish, lo, qi, tk):
    lane = lax.broadcasted_iota(jnp.int32, (tk, LANES), 1)
    no_stab = (lane < _STAB_SLOT) | (lane >= _STAB_SLOT + _N_PIECES)
    q_halves = [jnp.concatenate([q_ref[r * tk:(r + 1) * tk, :],
                                 jnp.where(no_stab, qx_ref[r * tk:(r + 1) * tk, :], jnp.zeros((), BF16))], axis=1)
                for r in range(2)]

    def weights(s, carry):
        m, l, acc = carry
        cols = [s[:, c * LANES:(c + 1) * LANES] for c in range(tk // LANES)]
        m_new = jnp.maximum(m, jnp.max(functools.reduce(jnp.maximum, cols), axis=-1, keepdims=True))
        alpha = jnp.exp2(m - m_new)
        ps = [jnp.exp2(c - m_new) for c in cols]
        l = alpha * l + functools.reduce(jnp.add, ps)
        return m_new, l, alpha * acc, jnp.concatenate(ps, axis=1).astype(BF16)

    def update(q, k, v, carry, masked):
        m, l, acc, p = weights(logits(q, k, masked), carry)
        return m, l, acc + jnp.dot(p, v, preferred_element_type=F32)

    def both(ki, carries):
        k, v = load_kv(ki)
        s = [logits(q_halves[r], k, False) for r in range(2)]
        w = [weights(s[r], carries[r]) for r in range(2)]
        return tuple((m, l, acc + jnp.dot(p, v, preferred_element_type=F32)) for m, l, acc, p in w)

    init = (jnp.full((tk, LANES), -jnp.inf, F32), jnp.zeros((tk, LANES), F32), jnp.zeros((tk, FOX_HEAD_DIM), F32))
    ca, cb = lax.fori_loop(lo, 2 * qi, both, (init, init))
    k, v = load_kv(2 * qi)
    ca = update(q_halves[0], k, v, ca, True)
    cb = update(q_halves[1], k, v, cb, False)
    k, v = load_kv(2 * qi + 1)
    cb = update(q_halves[1], k, v, cb, True)
    for r, (_, l, acc) in enumerate((ca, cb)):
        finish(r, acc, jnp.sum(l, axis=-1, keepdims=True))


def _fox_plan(qn, kmax, gc, *, batch, seq, tk):
    tq = 2 * tk
    nq = seq // tq
    nk = seq // tk
    heads = slice(0, FOX_HEADS)
    qmax = jnp.max(qn.reshape(batch, nq, tq, GATE_LANES), axis=2)[..., heads]
    gap = 2.0 * qmax * kmax[:, None, heads] + BOUND_SLACK_LOG2
    f2 = (gc[:, heads] * LOG2E).reshape(batch, nk, tk, FOX_HEADS)
    f_first = f2[:, ::2, 0, :]
    f_last = f2[:, :, tk - 1, :]
    bound = gap[:, :, None, :] + f_first[:, :, None, :] - f_last[:, None, :, :]
    below_diag = jnp.arange(nk)[None, :, None] < 2 * jnp.arange(nq)[:, None, None]
    skip = (bound < -F32_ZERO_LOG2) & below_diag[None]
    lo = jnp.sum(jnp.cumprod(skip.astype(jnp.int32), axis=2), axis=2)
    fast = (gap <= FAST_GAP_LOG2).astype(jnp.int32)
    flat = lambda a: jnp.transpose(a, (0, 2, 1)).reshape(-1)
    return flat(lo), flat(fast)


def _fox(z, qx, kx, nw, lo, fast, *, batch, seq, tk=512):
    T = z.shape[0]
    tq = 2 * tk
    nq = seq // tq
    kcol = FOX_WIDTH // FOX_HEAD_DIM
    grid_spec = pltpu.PrefetchScalarGridSpec(
        num_scalar_prefetch=2,
        grid=(batch, FOX_HEADS, nq),
        in_specs=[
            pl.BlockSpec((tq, FOX_HEAD_DIM), lambda b, h, i, lo, fast: (b * nq + i, h)),
            pl.BlockSpec((tq, FOX_HEAD_DIM), lambda b, h, i, lo, fast: (b * nq + i, h)),
            pl.BlockSpec((seq, FOX_HEAD_DIM), lambda b, h, i, lo, fast: (b, kcol + h)),
            pl.BlockSpec((seq, FOX_HEAD_DIM), lambda b, h, i, lo, fast: (b, h)),
            pl.BlockSpec((seq, FOX_HEAD_DIM), lambda b, h, i, lo, fast: (b, 2 * kcol + h)),
            pl.BlockSpec((FOX_HEADS, FOX_HEAD_DIM), lambda b, h, i, lo, fast: (0, 0)),
        ],
        out_specs=pl.BlockSpec((tq, FOX_HEAD_DIM), lambda b, h, i, lo, fast: (b * nq + i, h)),
    )
    return pl.pallas_call(
        functools.partial(_fox_body, tk=tk, nq=nq),
        grid_spec=grid_spec,
        out_shape=jax.ShapeDtypeStruct((T, FOX_WIDTH), BF16),
        compiler_params=_cparams(("parallel", "parallel", "arbitrary")),
        name="fox",
    )(lo, fast, z, qx, z, kx, z, nw)


def _mlstm_body(zc_ref, zprev_ref, v_ref, o_ref, gc_ref, gr_ref, cw_ref, cb_ref, nw_ref, out_ref,
                full_scr, c_scr, n_scr, m_scr):
    c = pl.program_id(1)
    L = MLSTM_CHUNK
    dk = MLSTM_QK_DIM
    dv = MLSTM_V_DIM
    rows = L

    @pl.when(c == 0)
    def _():
        c_scr[...] = jnp.zeros_like(c_scr)
        n_scr[...] = jnp.zeros_like(n_scr)
        m_scr[...] = jnp.zeros_like(m_scr)

    full_scr[0:SUBLANES, :] = jnp.where(c == 0, 0.0, zprev_ref[...])
    full_scr[SUBLANES:SUBLANES + rows, :] = zc_ref[...]
    full = full_scr[...]
    y = cb_ref[...]
    for j in range(MLSTM_CONV):
        shift = MLSTM_CONV - 1 - j
        tap = (pltpu.roll(full, shift, axis=0) if shift else full)[SUBLANES:SUBLANES + rows, :]
        y = y + cw_ref[j:j + 1, :] * tap
    qk_all = y * jax.nn.sigmoid(y)

    row = lax.broadcasted_iota(jnp.int32, (L, L), 0)
    col = lax.broadcasted_iota(jnp.int32, (L, L), 1)
    causal = row >= col
    heads = range(MLSTM_HEADS)
    gcb = gc_ref[...]
    grb = gr_ref[...]
    qh = [qk_all[:, hh * dk:(hh + 1) * dk] for hh in heads]
    kh = [qk_all[:, MLSTM_QK_WIDTH + hh * dk:MLSTM_QK_WIDTH + (hh + 1) * dk] * (dk ** -0.5) for hh in heads]
    vh = [v_ref[:, hh * dv:(hh + 1) * dv] for hh in heads]
    i_col = [gcb[:, FOX_HEADS + hh:FOX_HEADS + hh + 1] for hh in heads]
    b_col = [gcb[:, FOX_HEADS + MLSTM_HEADS + hh:FOX_HEADS + MLSTM_HEADS + hh + 1] for hh in heads]
    i_row = [grb[FOX_HEADS + hh:FOX_HEADS + hh + 1, :] for hh in heads]
    b_row = [grb[FOX_HEADS + MLSTM_HEADS + hh:FOX_HEADS + MLSTM_HEADS + hh + 1, :] for hh in heads]
    b_last = [b_row[hh][:, L - 1:L] for hh in heads]
    c_prev = [c_scr[hh] for hh in heads]
    n_prev = [n_scr[hh] for hh in heads]
    m_prev = [m_scr[hh][:, 0:1] for hh in heads]

    m_loc = [jnp.max(b_last[hh] - b_row[hh] + i_row[hh], axis=-1, keepdims=True) for hh in heads]
    kw = [kh[hh] * jnp.exp(b_last[hh] - b_col[hh] + i_col[hh] - m_loc[hh]) for hh in heads]
    g_col = [b_col[hh] + m_prev[hh] for hh in heads]
    dmat = [jnp.where(causal, b_col[hh] - b_row[hh] + i_row[hh], -jnp.inf) for hh in heads]
    m_t = [jnp.maximum(g_col[hh], jnp.max(dmat[hh], axis=-1, keepdims=True)) for hh in heads]
    decay = [jnp.exp(dmat[hh] - m_t[hh]) for hh in heads]
    inter = [jnp.exp(g_col[hh] - m_t[hh]) for hh in heads]
    qb = [qh[hh].astype(BF16) for hh in heads]
    qk = [lax.dot_general(qb[hh], kh[hh].astype(BF16), (((1,), (1,)), ((), ())), preferred_element_type=F32)
          for hh in heads]
    carried = [jnp.dot(qb[hh], c_prev[hh].astype(BF16), preferred_element_type=F32) for hh in heads]
    c_loc = [lax.dot_general(kw[hh].astype(BF16), vh[hh], (((0,), (0,)), ((), ())), preferred_element_type=F32)
             for hh in heads]
    sm = [qk[hh] * decay[hh] for hh in heads]
    intra = [jnp.dot(sm[hh].astype(BF16), vh[hh], preferred_element_type=F32) for hh in heads]
    den = [jnp.sum(sm[hh], axis=-1, keepdims=True)
           + inter[hh] * jnp.sum(qh[hh] * n_prev[hh], axis=-1, keepdims=True) for hh in heads]
    for hh in heads:
        cell = (intra[hh] + inter[hh] * carried[hh]) / jnp.maximum(jnp.abs(den[hh]), jnp.exp(-m_t[hh]))
        gated = jax.nn.sigmoid(o_ref[:, hh * dv:(hh + 1) * dv].astype(F32)) * cell
        ms = jnp.mean(gated * gated, axis=-1, keepdims=True)
        out_ref[:, hh * dv:(hh + 1) * dv] = (
            (gated * lax.rsqrt(ms + NORM_EPS)) * nw_ref[:, hh * dv:(hh + 1) * dv]).astype(BF16)
    for hh in heads:
        m_new = jnp.maximum(b_last[hh] + m_prev[hh], m_loc[hh])
        a_prev = jnp.exp(b_last[hh] + m_prev[hh] - m_new)
        a_loc = jnp.exp(m_loc[hh] - m_new)
        c_scr[hh] = a_prev * c_prev[hh] + a_loc * c_loc[hh]
        n_scr[hh] = a_prev * n_prev[hh] + a_loc * jnp.sum(kw[hh], axis=0, keepdims=True)
        m_scr[hh] = jnp.broadcast_to(m_new, (1, LANES))


def _mlstm(zc, z, gc, gr, conv_w, conv_b, nw, *, batch, seq):
    T = zc.shape[0]
    L = MLSTM_CHUNK
    nc = seq // L
    per = L // SUBLANES
    vcol = 3 * FOX_WIDTH // MLSTM_V_WIDTH
    return pl.pallas_call(
        _mlstm_body,
        grid=(batch, nc),
        in_specs=[
            pl.BlockSpec((L, 2 * MLSTM_QK_WIDTH), lambda b, c: (b * nc + c, 0)),
            pl.BlockSpec((SUBLANES, 2 * MLSTM_QK_WIDTH), lambda b, c: (jnp.maximum((b * nc + c) * per - 1, 0), 0)),
            pl.BlockSpec((L, MLSTM_V_WIDTH), lambda b, c: (b * nc + c, vcol)),
            pl.BlockSpec((L, MLSTM_V_WIDTH), lambda b, c: (b * nc + c, vcol + 1)),
            pl.BlockSpec((L, GATE_LANES), lambda b, c: (b * nc + c, 0)),
            pl.BlockSpec((GATE_ROWS, L), lambda b, c: (0, b * nc + c)),
            pl.BlockSpec((MLSTM_CONV, 2 * MLSTM_QK_WIDTH), lambda b, c: (0, 0)),
            pl.BlockSpec((1, 2 * MLSTM_QK_WIDTH), lambda b, c: (0, 0)),
            pl.BlockSpec((1, MLSTM_V_WIDTH), lambda b, c: (0, 0)),
        ],
        out_specs=pl.BlockSpec((L, MLSTM_V_WIDTH), lambda b, c: (b * nc + c, 0)),
        out_shape=jax.ShapeDtypeStruct((T, MLSTM_V_WIDTH), BF16),
        scratch_shapes=[
            pltpu.VMEM((SUBLANES + L, 2 * MLSTM_QK_WIDTH), F32),
            pltpu.VMEM((MLSTM_HEADS, MLSTM_QK_DIM, MLSTM_V_DIM), F32),
            pltpu.VMEM((MLSTM_HEADS, 1, MLSTM_QK_DIM), F32),
            pltpu.VMEM((MLSTM_HEADS, 1, LANES), F32),
        ],
        compiler_params=_cparams(("parallel", "arbitrary")),
        name="mlstm",
    )(zc, zc, z, z, gc, gr, conv_w, conv_b, nw)


def _outproj_body(att_ref, cell_ref, x_ref, wa_ref, wb_ref, n2_ref, x1_ref, h2t_ref):
    half = x_ref.shape[0] // 2
    rows = [slice(r * half, (r + 1) * half) for r in range(2)]
    ys = [jnp.dot(att_ref[rs, :], wa_ref[...], preferred_element_type=F32)
          + jnp.dot(cell_ref[rs, :], wb_ref[...], preferred_element_type=F32) for rs in rows]
    for rs, y in zip(rows, ys):
        x1 = x_ref[rs, :] + y
        x1_ref[rs, :] = x1
        ms = jnp.mean(x1 * x1, axis=-1, keepdims=True)
        h2 = (x1 * lax.rsqrt(ms + NORM_EPS)) * n2_ref[...]
        h2t_ref[:, rs] = h2.T.astype(BF16)


def _outproj(att, cell, x2, wa, wb, n2, *, tm=512):
    T = x2.shape[0]
    return pl.pallas_call(
        _outproj_body,
        grid=(T // tm,),
        in_specs=[
            pl.BlockSpec((tm, FOX_WIDTH), lambda i: (i, 0)),
            pl.BlockSpec((tm, MLSTM_V_WIDTH), lambda i: (i, 0)),
            pl.BlockSpec((tm, D_MODEL), lambda i: (i, 0)),
            pl.BlockSpec((FOX_WIDTH, D_MODEL), lambda i: (0, 0)),
            pl.BlockSpec((MLSTM_V_WIDTH, D_MODEL), lambda i: (0, 0)),
            pl.BlockSpec((1, D_MODEL), lambda i: (0, 0)),
        ],
        out_specs=[
            pl.BlockSpec((tm, D_MODEL), lambda i: (i, 0)),
            pl.BlockSpec((D_MODEL, tm), lambda i: (0, i)),
        ],
        out_shape=[
            jax.ShapeDtypeStruct((T, D_MODEL), F32),
            jax.ShapeDtypeStruct((D_MODEL, T), BF16),
        ],
        compiler_params=_cparams(("parallel",)),
        name="outproj",
    )(att, cell, x2, wa, wb, n2)


def _row_iota(n_rows, tb):
    return lax.broadcasted_iota(jnp.int32, (n_rows, tb), 0).astype(F32)


def _extract16(s, val_ref, idx_ref):
    n_rows, tb = s.shape
    iota = _row_iota(n_rows, tb)

    def body(r, sw):
        mx = jnp.max(sw, axis=0, keepdims=True)
        first = jnp.min(jnp.where(sw == mx, iota, float(n_rows)), axis=0, keepdims=True)
        val_ref[pl.ds(r, 1), :] = mx
        idx_ref[pl.ds(r, 1), :] = first
        return jnp.where(iota == first, -jnp.inf, sw)

    lax.fori_loop(0, PEER_TOPK, body, s)


def _scatter_rounds(idx, values, n_rows, fill):
    tb = idx.shape[1]
    iota = _row_iota(n_rows, tb)
    table = jnp.full((n_rows, tb), fill, F32)
    for r in range(PEER_TOPK):
        v = values[r:r + 1, :] if values is not None else float(r)
        table = jnp.where(iota == idx[r:r + 1, :], v, table)
    return table


def _extract16_distinct(arrays, val_refs):
    def body(r, carry):
        out = []
        for (sw, rank), val_ref in zip(carry, val_refs):
            mx = jnp.max(sw, axis=0, keepdims=True)
            val_ref[pl.ds(r, 1), :] = mx
            hit = sw == mx
            out.append((jnp.where(hit, -jnp.inf, sw), jnp.where(hit, lax.convert_element_type(r, F32), rank)))
        return tuple(out)

    init = tuple((s, jnp.full(s.shape, float(PEER_TOPK), F32)) for s in arrays)
    return [rank for _, rank in lax.fori_loop(0, PEER_TOPK, body, init, unroll=True)]


def _count_ge(s, thr):
    return jnp.sum((s >= thr).astype(F32), axis=0, keepdims=True)


def _match_rounds(s, vals, values, fill):
    table = jnp.full(s.shape, fill, F32)
    for r in range(PEER_TOPK):
        v = values[r:r + 1, :] if values is not None else float(r)
        table = jnp.where(s == vals[r:r + 1, :], v, table)
    return table


def _peer_sel_body(h2t_ref, h2t_next_ref, wqt_ref, keys_ref, grp_ref, e1_ref, c1_ref, n0_ref, w0_ref,
                   qt_scr, sc_scr, vals_scr, idx_scr, cand_scr, pe_scr, *, tb):
    half = PEER_KEY_DIM // 2

    def queries(src_ref):
        qt_scr[...] = jnp.dot(wqt_ref[...], src_ref[...], preferred_element_type=F32).astype(BF16)

    @pl.when(pl.program_id(0) == 0)
    def _():
        queries(h2t_ref)

    for hp in range(2 * PEER_HEADS):
        sc_scr[hp] = jnp.dot(keys_ref[hp % 2], qt_scr[hp * half:(hp + 1) * half, :], preferred_element_type=F32)
    queries(h2t_next_ref)

    def head_tables(h, exact):
        s0 = sc_scr[2 * h]
        s1 = sc_scr[2 * h + 1]
        if exact:
            _extract16(s0, vals_scr.at[0], idx_scr.at[0])
            _extract16(s1, vals_scr.at[1], idx_scr.at[1])
        else:
            _extract16_distinct([s0], [vals_scr.at[0]])
            rank1, = _extract16_distinct([s1], [vals_scr.at[1]])
        a = vals_scr[0]
        b = vals_scr[1]
        ea = jnp.exp(a - a[0:1, :])
        eb = jnp.exp(b - b[0:1, :])
        off = 0
        for r in range(PEER_TOPK):
            ncol = _CAND_COLS[r]
            cand_scr[off:off + ncol, :] = a[r:r + 1, :] + b[0:ncol, :]
            pe_scr[off:off + ncol, :] = ea[r:r + 1, :] * eb[0:ncol, :]
            off += ncol
        cand_scr[_N_CAND:_N_CAND_PAD, :] = jnp.full((_N_CAND_PAD - _N_CAND, tb), -jnp.inf, F32)
        pe_scr[_N_CAND:_N_CAND_PAD, :] = jnp.zeros((_N_CAND_PAD - _N_CAND, tb), F32)
        cand = cand_scr[...]
        last = PEER_TOPK - 1
        if exact:
            _extract16(cand, vals_scr.at[2], idx_scr.at[2])
            sel = _scatter_rounds(idx_scr[2], jnp.ones((PEER_TOPK, tb), F32), _N_CAND_PAD, 0.0)
            ties = None
        else:
            _extract16_distinct([cand], [vals_scr.at[2]])
            sel = (cand >= vals_scr[2, last:last + 1, :]).astype(F32)
            ties = ((_count_ge(s0, a[last:last + 1, :]) != float(PEER_TOPK)).astype(F32)
                    + (_count_ge(s1, b[last:last + 1, :]) != float(PEER_TOPK)).astype(F32)
                    + (jnp.sum(sel, axis=0, keepdims=True) != float(PEER_TOPK)).astype(F32))
        zsum = jnp.sum(sel * pe_scr[...], axis=0, keepdims=True)
        nr = jnp.dot(grp_ref[...], sel.astype(BF16), preferred_element_type=F32)
        if exact:
            n0 = _scatter_rounds(idx_scr[0], nr, PEER_N_KEYS, 0.0)
            c1 = _scatter_rounds(idx_scr[1], None, PEER_N_KEYS, float(PEER_TOPK))
        else:
            n0 = _match_rounds(s0, a, nr, 0.0)
            c1 = rank1
        n0_ref[h] = n0
        w0_ref[h] = jnp.exp(s0 - a[0:1, :]) * (0.5 / zsum)
        e1_ref[h] = jnp.exp(s1 - b[0:1, :]).astype(BF16)
        c1_ref[h] = c1.astype(BF16)
        return ties

    for h in range(PEER_HEADS):
        ties = head_tables(h, False)

        @pl.when(jnp.max(ties) > 0.0)
        def _():
            head_tables(h, True)


def _peer_sel(h2t, wqt, keys, grp, *, tb=256):
    T = h2t.shape[1]
    tab = jax.ShapeDtypeStruct((PEER_HEADS, PEER_N_KEYS, T), F32)
    tab16 = jax.ShapeDtypeStruct((PEER_HEADS, PEER_N_KEYS, T), BF16)
    tab_spec = pl.BlockSpec((PEER_HEADS, PEER_N_KEYS, tb), lambda i: (0, 0, i))
    n_blocks = T // tb
    return pl.pallas_call(
        functools.partial(_peer_sel_body, tb=tb),
        grid=(n_blocks,),
        in_specs=[
            pl.BlockSpec((D_MODEL, tb), lambda i: (0, 0)),
            pl.BlockSpec((D_MODEL, tb), lambda i: (0, jnp.minimum(i + 1, n_blocks - 1))),
            pl.BlockSpec((PEER_HEADS * PEER_KEY_DIM, D_MODEL), lambda i: (0, 0)),
            pl.BlockSpec((2, PEER_N_KEYS, PEER_KEY_DIM // 2), lambda i: (0, 0, 0)),
            pl.BlockSpec((PEER_TOPK, _N_CAND_PAD), lambda i: (0, 0)),
        ],
        out_specs=[tab_spec, tab_spec, tab_spec, tab_spec],
        out_shape=[tab16, tab16, tab, tab],
        scratch_shapes=[
            pltpu.VMEM((PEER_HEADS * PEER_KEY_DIM, tb), BF16),
            pltpu.VMEM((2 * PEER_HEADS, PEER_N_KEYS, tb), F32),
            pltpu.VMEM((3, PEER_TOPK, tb), F32),
            pltpu.VMEM((3, PEER_TOPK, tb), F32),
            pltpu.VMEM((_N_CAND_PAD, tb), F32),
            pltpu.VMEM((_N_CAND_PAD, tb), F32),
        ],
        compiler_params=_cparams(("arbitrary",)),
        name="peer_sel",
    )(h2t, h2t, wqt, keys, grp)


def _peer_dense_body(u_ref, vt_ref, h2t_ref, e1_ref, c1_ref, n0_ref, w0_ref, x1_ref, fw_ref, out_ref,
                     yt_ref, st_a, st_b, *, ec, tb, nk):
    k = pl.program_id(1)
    slabs = ec // PEER_N_KEYS

    def pre_activations(st_w):
        st_w[...] = jnp.dot(u_ref[...], h2t_ref[...], preferred_element_type=F32)

    def bf16_rows(ref, h, ii):
        row = jnp.broadcast_to(ref[h, ii:ii + 1, :], (2 * SUBLANES, tb)).astype(BF16)
        return jnp.broadcast_to(row[None], (PEER_N_KEYS // (2 * SUBLANES), 2 * SUBLANES, tb)).reshape(PEER_N_KEYS, tb)

    def activation_slab(st_r, ii):
        g = jnp.zeros((PEER_N_KEYS, tb), BF16)
        for h in range(PEER_HEADS):
            n_row = bf16_rows(n0_ref, h, ii)
            w_row = bf16_rows(w0_ref, h, ii)
            g = g + jnp.where(c1_ref[h] < n_row, e1_ref[h] * w_row, jnp.zeros((), BF16))
        x = st_r[ii * PEER_N_KEYS:(ii + 1) * PEER_N_KEYS, :]
        act = x * (1.0 + lax.erf(x * math.sqrt(0.5)))
        return act.astype(BF16) * g

    def activate_and_project(st_r):
        at = jnp.concatenate([activation_slab(st_r, ii) for ii in range(slabs)], axis=0)
        yt_ref[...] += jnp.dot(vt_ref[...], at, preferred_element_type=F32)

    @pl.when(k == 0)
    def _():
        yt_ref[...] = jnp.zeros_like(yt_ref)
        pre_activations(st_a)

    @pl.when((k > 0) & (k < nk) & (k % 2 == 1))
    def _():
        pre_activations(st_b)
        activate_and_project(st_a)

    @pl.when((k > 0) & (k < nk) & (k % 2 == 0))
    def _():
        pre_activations(st_a)
        activate_and_project(st_b)

    @pl.when(k == nk)
    def _():
        activate_and_project(st_b if nk % 2 == 0 else st_a)
        x2 = x1_ref[...] + yt_ref[...].T
        ms = jnp.mean(x2 * x2, axis=-1, keepdims=True)
        out_ref[...] = (x2 * lax.rsqrt(ms + NORM_EPS)) * fw_ref[...]


def _peer_dense(u, vt, h2t, e1, c1, n0, w0, x1, final_w, *, tb=512, ec=1024):
    T = h2t.shape[1]
    nk = PEER_N_EXPERTS // ec
    slabs = ec // PEER_N_KEYS
    assert slabs == SUBLANES, "one f32 sublane tile of per-slab gate rows per expert chunk"
    tab_spec = pl.BlockSpec((PEER_HEADS, PEER_N_KEYS, tb), lambda i, k: (0, 0, i))
    row_spec = pl.BlockSpec((PEER_HEADS, slabs, tb), lambda i, k: (0, jnp.maximum(k - 1, 0), i))
    return pl.pallas_call(
        functools.partial(_peer_dense_body, ec=ec, tb=tb, nk=nk),
        grid=(T // tb, nk + 1),
        in_specs=[
            pl.BlockSpec((ec, D_MODEL), lambda i, k: (jnp.minimum(k, nk - 1), 0)),
            pl.BlockSpec((D_MODEL, ec), lambda i, k: (0, jnp.maximum(k - 1, 0))),
            pl.BlockSpec((D_MODEL, tb), lambda i, k: (0, i)),
            tab_spec, tab_spec, row_spec, row_spec,
            pl.BlockSpec((tb, D_MODEL), lambda i, k: (i, 0)),
            pl.BlockSpec((1, D_MODEL), lambda i, k: (0, 0)),
        ],
        out_specs=pl.BlockSpec((tb, D_MODEL), lambda i, k: (i, 0)),
        out_shape=jax.ShapeDtypeStruct((T, D_MODEL), F32),
        scratch_shapes=[pltpu.VMEM((D_MODEL, tb), F32),
                        pltpu.VMEM((ec, tb), F32), pltpu.VMEM((ec, tb), F32)],
        compiler_params=_cparams(("parallel", "arbitrary")),
        name="peer_dense",
    )(u, vt, h2t, e1, c1, n0, w0, x1, final_w)


def _group_matrix():
    g = np.zeros((PEER_TOPK, _N_CAND_PAD), np.float32)
    off = 0
    for r, ncol in enumerate(_CAND_COLS):
        g[r, off:off + ncol] = 1.0
        off += ncol
    return g


def _layer(x2, norm1_w, w_in, fox_f_bias, conv_w, conv_b, i_bias, f_bias, fox_nw, mlstm_nw, w_out, norm2_w,
           w_q, keys, u, v, final_w, *, batch, seq):
    splits = np.cumsum((FOX_WIDTH, FOX_WIDTH, FOX_WIDTH, FOX_HEADS, MLSTM_QK_WIDTH, MLSTM_QK_WIDTH,
                        MLSTM_V_WIDTH, MLSTM_HEADS, MLSTM_HEADS, MLSTM_V_WIDTH))[:-1]
    fq, fk, fv, ff, mq, mk, mv, mi, mf, mo = jnp.split(w_in, [int(p) for p in splits], axis=1)
    w_main = jnp.concatenate([fq, fk, fv, mv, mo, mq, mk], axis=1).astype(BF16)
    gate_pad = GATE_LANES - FOX_HEADS - 2 * MLSTM_HEADS
    w_gate = jnp.pad(jnp.concatenate([ff, mi, mf], axis=1), ((0, 0), (0, gate_pad))).astype(BF16)
    gate_bias = jnp.pad(jnp.concatenate([fox_f_bias, i_bias, f_bias]), (0, gate_pad)).reshape(1, GATE_LANES)
    tri = jnp.asarray(np.tril(np.ones((LANES, LANES), np.float32)), BF16)

    z, zc, g, qn, kn = _inproj(x2, norm1_w.reshape(1, D_MODEL), w_main, w_gate,
                               jnp.asarray(_head_group_matrix(), BF16))
    pq, pk = _placement_matrices()
    kmax = jnp.max(kn.reshape(batch, seq, GATE_LANES), axis=1)
    gc, gr, qx, kx = _gates(g, gate_bias, tri, jnp.asarray(pq, BF16), jnp.asarray(pk, BF16), qn,
                            jnp.repeat(kmax, SUBLANES, axis=0), batch=batch, seq=seq)
    lo, fast = _fox_plan(qn, kmax, gc, batch=batch, seq=seq, tk=FOX_KEY_BLOCK)
    att = _fox(z, qx, kx, fox_nw.reshape(FOX_HEADS, FOX_HEAD_DIM), lo, fast, batch=batch, seq=seq, tk=FOX_KEY_BLOCK)
    cell = _mlstm(zc, z, gc, gr, conv_w, conv_b.reshape(1, -1), mlstm_nw.reshape(1, -1), batch=batch, seq=seq)
    x1, h2t = _outproj(att, cell, x2, w_out[:FOX_WIDTH].astype(BF16), w_out[FOX_WIDTH:].astype(BF16),
                       norm2_w.reshape(1, D_MODEL))
    e1, c1, n0, w0 = _peer_sel(h2t, w_q.T.astype(BF16), keys.astype(BF16), jnp.asarray(_group_matrix(), BF16))
    return _peer_dense(u.astype(BF16), v.T.astype(BF16), h2t, e1, c1, n0, w0, x1, final_w)


def kernel(x, norm1_w, w_in, fox_f_bias, mlstm_conv_w, mlstm_conv_b, mlstm_i_bias, mlstm_f_bias, fox_out_norm_w,
           mlstm_out_norm_w, w_out, norm2_w, peer_w_q, peer_keys, peer_u, peer_v, final_norm_w):
    batch, seq, _ = x.shape
    assert w_in.shape[0] == 1, "single-layer block: the final norm is fused with the last residual add"
    x2 = x.reshape(batch * seq, D_MODEL)
    out = _layer(x2, norm1_w[0], w_in[0], fox_f_bias[0], mlstm_conv_w[0], mlstm_conv_b[0], mlstm_i_bias[0],
                 mlstm_f_bias[0], fox_out_norm_w[0], mlstm_out_norm_w[0], w_out[0], norm2_w[0],
                 peer_w_q[0], peer_keys[0], peer_u[0], peer_v[0], final_norm_w.reshape(1, D_MODEL),
                 batch=batch, seq=seq)
    return out.reshape(batch, seq, D_MODEL)
```
